```python
import math
import jax, jax.numpy as jnp
from jax import lax
import numpy as np

D_MODEL = 1024
BATCH = 32
SEQ = 256
DEPTH = 1
DEC_BATCH = 8
DEC_SEQ = 1024
PAST_LEN = 256

GRID_W = 64
D_S5 = D_MODEL
S5_GROUP = 16
N_S5_GROUPS = D_S5 // S5_GROUP
S5_STATE = 64
D_POOL = D_MODEL // 2
POOL_WINDOWS = (2, 4, 8, 16)
N_POOL_GROUPS = len(POOL_WINDOWS)
POOL_GROUP = D_POOL // N_POOL_GROUPS
D_IN = D_S5 + D_POOL + 2 * D_MODEL
D_FF = 4 * D_MODEL
N_MOD = 6
DEEPNORM_ALPHA = (2.0 * DEPTH) ** 0.25
DEEPNORM_BETA = (8.0 * DEPTH) ** -0.25
LN_EPS = 1e-6

kernel_name = 'hybrid_s5_pool_diffusion_step'


def layer_norm(x, g=None, b=None):
    xf = x.astype(jnp.float32)
    mu = jnp.mean(xf, axis=-1, keepdims=True)
    var = jnp.mean(jnp.square(xf - mu), axis=-1, keepdims=True)
    y = (xf - mu) * lax.rsqrt(var + LN_EPS)
    if g is not None:
        y = y * g.astype(jnp.float32) + b.astype(jnp.float32)
    return y.astype(x.dtype)


def adaln(cvec, w, b):
    m = jax.nn.silu(cvec) @ w + b
    return m.reshape(cvec.shape[0], N_MOD, 1, D_MODEL)


def _combine(e1, e2):
    a1, b1 = e1
    a2, b2 = e2
    return a1 * a2, a2 * b1 + b2


def s5_direction(u, s0, lam_re, lam_im, log_dt, b_re, b_im, c_re, c_im, d, reverse):
    f32 = jnp.float32
    bsz, n, _ = u.shape
    uf = u.astype(f32).reshape(bsz, n, N_S5_GROUPS, S5_GROUP)
    lam = lax.complex(lam_re.astype(f32), lam_im.astype(f32))
    dt = jnp.exp(log_dt.astype(f32))[:, None]
    lam_bar = jnp.exp(lam * dt)
    b_mat = lax.complex(b_re.astype(f32), b_im.astype(f32))
    c_mat = lax.complex(c_re.astype(f32), c_im.astype(f32))
    b_bar = ((lam_bar - 1.0) / lam)[:, :, None] * b_mat
    bu = jnp.einsum('blgh,gph->blgp', uf.astype(jnp.complex64), b_bar)
    first, last = (n - 1, 0) if reverse else (0, n - 1)
    if s0 is not None:
        bu = bu.at[:, first].add(lam_bar * s0)
    a = jnp.broadcast_to(lam_bar, bu.shape)
    _, s = lax.associative_scan(_combine, (a, bu), reverse=reverse, axis=1)
    y = jnp.einsum('blgp,ghp->blgh', s, c_mat).real + d.astype(f32).reshape(N_S5_GROUPS, S5_GROUP) * uf
    return y.reshape(bsz, n, D_S5), s[:, last]


def pool_mix(u, w_pool, pool_scale, grid_w):
    bsz, n_tok, _ = u.shape
    if grid_w is not None:
        rows = n_tok // grid_w
        v = u.reshape(bsz, rows, grid_w, D_POOL)
    else:
        v = u
    vf = v.astype(jnp.float32)
    n = vf.shape[-2]
    cs = jnp.cumsum(vf, axis=-2)
    cs = jnp.concatenate([jnp.zeros_like(cs[..., :1, :]), cs], axis=-2)
    t = jnp.arange(n)
    outs = []
    for gi, w in enumerate(POOL_WINDOWS):
        lo = jnp.clip(t - w // 2, 0, n)
        hi = jnp.clip(t - w // 2 + w, 0, n)
        cg = cs[..., gi * POOL_GROUP:(gi + 1) * POOL_GROUP]
        mean = (jnp.take(cg, hi, axis=-2) - jnp.take(cg, lo, axis=-2)) / (hi - lo).astype(jnp.float32)[:, None]
        outs.append(mean - vf[..., gi * POOL_GROUP:(gi + 1) * POOL_GROUP])
    p = jnp.stack(outs, axis=-2)
    p = jnp.einsum('...gi,gio->...go', p, w_pool).reshape(vf.shape) * pool_scale
    return p.reshape(bsz, n_tok, D_POOL)


def token_mixer(h, s0_f, s0_b, grid_w, lp):
    z = h @ lp['w_in'] + lp['b_in']
    u_a = z[..., :D_S5]
    u_b = z[..., D_S5:D_S5 + D_POOL]
    g_a = z[..., D_S5 + D_POOL:D_S5 + D_POOL + D_MODEL]
    g_b = z[..., D_S5 + D_POOL + D_MODEL:]
    s5 = lp['s5']
    y_f, sf = s5_direction(u_a, s0_f, *[p[0] for p in s5], reverse=False)
    y_b, sb = s5_direction(u_a, s0_b, *[p[1] for p in s5], reverse=True)
    v = jax.nn.gelu(y_f + y_b)
    v = v * jax.nn.sigmoid(v @ lp['w_glu'] + lp['b_glu'])
    ya = v @ lp['w_proj_a']
    yb = pool_mix(u_b, lp['w_pool'], lp['pool_scale'], grid_w) @ lp['w_proj_b']
    merged = jax.nn.sigmoid(g_a) * ya + jax.nn.sigmoid(g_b) * yb
    return merged @ lp['w_out'] + lp['b_out'], sf, sb


def trunk_layer(x, mod, s0_f, s0_b, grid_w, lp):
    h = layer_norm(x) * (1.0 + mod[:, 1]) + mod[:, 0]
    tm, sf, sb = token_mixer(h, s0_f, s0_b, grid_w, lp)
    x = layer_norm(DEEPNORM_ALPHA * x + mod[:, 2] * tm, lp['ln1_g'], lp['ln1_b'])
    h2 = layer_norm(x) * (1.0 + mod[:, 4]) + mod[:, 3]
    f = jnp.square(jax.nn.relu(h2 @ lp['w_mlp1'] + lp['b_mlp1'])) @ lp['w_mlp2'] + lp['b_mlp2']
    x = layer_norm(DEEPNORM_ALPHA * x + mod[:, 5] * f, lp['ln2_g'], lp['ln2_b'])
    return x, sf, sb


def setup_inputs(seed: int = 0) -> dict:
    key = jax.random.key(seed)
    ks = jax.random.split(key, 40)
    f32 = jnp.float32

    def nrm(k, shape, scale):
        return jax.random.normal(k, shape, f32) * scale

    L, G, P, H = DEPTH, N_S5_GROUPS, S5_STATE, S5_GROUP
    lam_im_base = (math.pi * jnp.arange(P, dtype=f32))[None, None, None, :]
    return {
        'x_prompt': nrm(ks[0], (BATCH, SEQ, D_MODEL), 1.0),
        'x_sample': nrm(ks[1], (DEC_BATCH, DEC_SEQ, D_MODEL), 1.0),
        'state_s5': nrm(ks[2], (DEC_BATCH, L, 2, 2, G, P), 0.1),
        'c': nrm(ks[3], (DEC_BATCH, D_MODEL), 1.0),
        'c_ctx': nrm(ks[4], (D_MODEL,), 1.0),
        'w_ada': nrm(ks[5], (L, D_MODEL, N_MOD * D_MODEL), D_MODEL ** -0.5),
        'b_ada': nrm(ks[6], (L, N_MOD * D_MODEL), 0.01),
        'w_in': nrm(ks[7], (L, D_MODEL, D_IN), D_MODEL ** -0.5),
        'b_in': nrm(ks[8], (L, D_IN), 0.01),
        's5_lam_re': -0.5 + nrm(ks[9], (L, 2, G, P), 0.01),
        's5_lam_im': lam_im_base + nrm(ks[10], (L, 2, G, P), 0.01),
        's5_log_dt': jax.random.uniform(ks[11], (L, 2, G), f32, math.log(1e-3), math.log(1e-1)),
        's5_b_re': nrm(ks[12], (L, 2, G, P, H), (2.0 * H) ** -0.5),
        's5_b_im': nrm(ks[13], (L, 2, G, P, H), (2.0 * H) ** -0.5),
        's5_c_re': nrm(ks[14], (L, 2, G, H, P), P ** -0.5),
        's5_c_im': nrm(ks[15], (L, 2, G, H, P), P ** -0.5),
        's5_d': nrm(ks[16], (L, 2, D_S5), 1.0),
        'w_glu': nrm(ks[17], (L, D_S5, D_S5), D_S5 ** -0.5),
        'b_glu': nrm(ks[18], (L, D_S5), 0.01),
        'w_proj_a': nrm(ks[19], (L, D_S5, D_MODEL), D_S5 ** -0.5),
        'w_pool': nrm(ks[20], (L, N_POOL_GROUPS, POOL_GROUP, POOL_GROUP), POOL_GROUP ** -0.5),
        'pool_scale': 1.0 + nrm(ks[21], (L, D_POOL), 0.1),
        'w_proj_b': nrm(ks[22], (L, D_POOL, D_MODEL), D_POOL ** -0.5),
        'w_out': nrm(ks[23], (L, D_MODEL, D_MODEL), D_MODEL ** -0.5 * DEEPNORM_BETA),
        'b_out': nrm(ks[24], (L, D_MODEL), 0.01),
        'ln1_g': 1.0 + nrm(ks[25], (L, D_MODEL), 0.01),
        'ln1_b': nrm(ks[26], (L, D_MODEL), 0.01),
        'w_mlp1': nrm(ks[27], (L, D_MODEL, D_FF), D_MODEL ** -0.5),
        'b_mlp1': nrm(ks[28], (L, D_FF), 0.01),
        'w_mlp2': nrm(ks[29], (L, D_FF, D_MODEL), D_FF ** -0.5 * DEEPNORM_BETA),
        'b_mlp2': nrm(ks[30], (L, D_MODEL), 0.01),
        'ln2_g': 1.0 + nrm(ks[31], (L, D_MODEL), 0.01),
        'ln2_b': nrm(ks[32], (L, D_MODEL), 0.01),
    }


def reference(x_prompt, x_sample, state_s5, c, c_ctx, w_ada, b_ada, w_in, b_in,
              s5_lam_re, s5_lam_im, s5_log_dt, s5_b_re, s5_b_im, s5_c_re, s5_c_im, s5_d,
              w_glu, b_glu, w_proj_a, w_pool, pool_scale, w_proj_b, w_out, b_out,
              ln1_g, ln1_b, w_mlp1, b_mlp1, w_mlp2, b_mlp2, ln2_g, ln2_b):
    f32 = jnp.float32
    yp = x_prompt
    ys = x_sample
    new_states = []
    for l in range(DEPTH):
        lp = {
            'w_in': w_in[l], 'b_in': b_in[l],
            's5': (s5_lam_re[l], s5_lam_im[l], s5_log_dt[l], s5_b_re[l], s5_b_im[l],
                   s5_c_re[l], s5_c_im[l], s5_d[l]),
            'w_glu': w_glu[l], 'b_glu': b_glu[l], 'w_proj_a': w_proj_a[l],
            'w_pool': w_pool[l], 'pool_scale': pool_scale[l], 'w_proj_b': w_proj_b[l],
            'w_out': w_out[l], 'b_out': b_out[l],
            'ln1_g': ln1_g[l], 'ln1_b': ln1_b[l],
            'w_mlp1': w_mlp1[l], 'b_mlp1': b_mlp1[l], 'w_mlp2': w_mlp2[l], 'b_mlp2': b_mlp2[l],
            'ln2_g': ln2_g[l], 'ln2_b': ln2_b[l],
        }
        mod_ctx = adaln(c_ctx[None, :], w_ada[l], b_ada[l])
        yp, sf, sb = trunk_layer(yp, mod_ctx, None, None, None, lp)
        new_states.append(jnp.stack([jnp.stack([sf.real, sf.imag], axis=1),
                                     jnp.stack([sb.real, sb.imag], axis=1)], axis=1))
        st = state_s5[:, l].astype(f32)
        s0_f = lax.complex(st[:, 0, 0], st[:, 0, 1])
        s0_b = lax.complex(st[:, 1, 0], st[:, 1, 1])
        mod_lat = adaln(c, w_ada[l], b_ada[l])
        ys, _, _ = trunk_layer(ys, mod_lat, s0_f, s0_b, GRID_W, lp)
    new_state_s5 = jnp.stack(new_states, axis=1)
    return (yp, ys, new_state_s5)
```

```python
import functools

import jax
import jax.numpy as jnp
from jax import lax
from jax.experimental import pallas as pl
from jax.experimental.pallas import tpu as pltpu

F32 = jnp.float32
BF16 = jnp.bfloat16

D_MODEL = 1024
S5_GROUP = 16
N_GROUPS = D_MODEL // S5_GROUP
S5_STATE = 64
D_POOL = D_MODEL // 2
POOL_WINDOWS = (2, 4, 8, 16)
POOL_GROUP = D_POOL // len(POOL_WINDOWS)
D_IN = D_MODEL + D_POOL + 2 * D_MODEL
D_REST = D_IN - D_MODEL
D_FF = 4 * D_MODEL
N_MOD = 6
GRID_W = 64
DEEPNORM_ALPHA = 2.0 ** 0.25
LN_EPS = 1e-6

CHUNK = 16
CW = CHUNK * S5_GROUP
SW = 4 * S5_STATE
N_POW = 32
TOK_BLOCK = 512
POOL_TILE = 256
VMEM_LIMIT = 56 * 1024 * 1024


def _layer_norm(x):
    mu = jnp.mean(x, axis=-1, keepdims=True)
    xc = x - mu
    var = jnp.mean(xc * xc, axis=-1, keepdims=True)
    return xc * lax.rsqrt(var + LN_EPS)


def _dot(a, b):
    return jnp.dot(a, b, preferred_element_type=F32)


def _dot_nt(a, b, precision=None):
    return lax.dot_general(a, b, (((1,), (1,)), ((), ())), precision=precision,
                           preferred_element_type=F32)


def _mods_kernel(c_ref, w_ref, b_ref, o_ref):
    s = jax.nn.silu(c_ref[...])
    o_ref[...] = jnp.dot(s, w_ref[...], precision=lax.Precision.HIGHEST,
                         preferred_element_type=F32) + b_ref[...]


def _mods(cvec, w_ada, b_ada):
    rows = cvec.shape[0]
    n_out = w_ada.shape[1]
    return pl.pallas_call(
        _mods_kernel,
        grid=(n_out // D_MODEL,),
        in_specs=[
            pl.BlockSpec((rows, D_MODEL), lambda j: (0, 0)),
            pl.BlockSpec((D_MODEL, D_MODEL), lambda j: (0, j)),
            pl.BlockSpec((1, D_MODEL), lambda j: (0, j)),
        ],
        out_specs=pl.BlockSpec((rows, D_MODEL), lambda j: (0, j)),
        out_shape=jax.ShapeDtypeStruct((rows, n_out), F32),
        name="mods",
    )(cvec, w_ada, b_ada)


def _s5_prep_kernel(lre_ref, lim_ref, ldt_ref, bre_ref, bim_ref, cre_ref, cim_ref,
                    df_ref, db_ref, m_ref, wb_ref, wct_ref, lam_ref,
                    pwr_ref, pwi_ref, ge_ref):
    lre = lre_ref[0]
    lim = lim_ref[0]
    dt = jnp.exp(ldt_ref[0])
    a = lre * dt
    b = lim * dt
    col = lax.broadcasted_iota(jnp.int32, (1, SW), 1)
    is_im = col >= 2 * S5_STATE
    is_b = (col & (2 * S5_STATE - 1)) >= S5_STATE

    k = lax.broadcasted_iota(jnp.int32, (N_POW, SW), 0).astype(F32)
    mag = jnp.exp(k * a)
    pwr_ref[...] = mag * jnp.cos(k * b)
    pwi_ref[...] = mag * jnp.sin(k * b)

    def power(kf, kb):
        def row(ref, kk):
            return jnp.zeros((1, SW), F32) if kk is None else ref[kk:kk + 1, :]
        if kf == kb:
            return row(pwr_ref, kf), row(pwi_ref, kf)
        return (jnp.where(is_b, row(pwr_ref, kb), row(pwr_ref, kf)),
                jnp.where(is_b, row(pwi_ref, kb), row(pwi_ref, kf)))

    lbr = pwr_ref[1:2, :]
    lbi = pwi_ref[1:2, :]
    den = lre * lre + lim * lim
    nr = lbr - 1.0
    cr = (nr * lre + lbi * lim) / den
    ci = (lbi * lre - nr * lim) / den
    bre = bre_ref[0]
    bim = bim_ref[0]
    bbr = cr * bre - ci * bim
    bbi = cr * bim + ci * bre
    bx = jnp.where(is_im, bbi, bbr)
    by = jnp.where(is_im, bbr, -bbi)
    cre = cre_ref[0]
    cim = cim_ref[0]
    cx = jnp.where(is_im, -cim, cre)
    cy = jnp.where(is_im, -cre, -cim)

    for t in range(CHUNK):
        rows = slice(t * S5_GROUP, (t + 1) * S5_GROUP)
        pr, pi = power(CHUNK - 1 - t, t)
        wb_ref[0, rows, :] = (pr * bx + pi * by).astype(BF16)
        pr, pi = power(t + 1, CHUNK - t)
        wct_ref[0, rows, :] = (pr * cx + pi * cy).astype(BF16)

    for j in range(2 * CHUNK):
        rows = slice(j * S5_GROUP, (j + 1) * S5_GROUP)
        if j == 2 * CHUNK - 1:
            ge_ref[rows, :] = jnp.zeros((S5_GROUP, SW), F32)
            continue
        pr, pi = power(j - (CHUNK - 1) if j >= CHUNK - 1 else None,
                       (CHUNK - 1) - j if j <= CHUNK - 1 else None)
        ge_ref[rows, :] = pr * cx + pi * cy
    e = _dot_nt(bx, ge_ref[...], precision=lax.Precision.HIGHEST)

    dsum = df_ref[0] + db_ref[0]
    r16 = lax.broadcasted_iota(jnp.int32, (S5_GROUP, CW), 0)
    c16 = lax.broadcasted_iota(jnp.int32, (S5_GROUP, CW), 1)
    for tau in range(CHUNK):
        start = (CHUNK - 1 - tau) * S5_GROUP
        blk = e[:, start:start + CW]
        blk = blk + jnp.where(c16 == r16 + tau * S5_GROUP, dsum, 0.0)
        m_ref[0, tau * S5_GROUP:(tau + 1) * S5_GROUP, :] = blk.astype(BF16)

    lam_ref[0] = jnp.concatenate(
        [pwr_ref[CHUNK:CHUNK + 1, :], pwi_ref[CHUNK:CHUNK + 1, :],
         jnp.zeros((6, SW), F32)], axis=0)


def _s5_prep(lre, lim, ldt, bre, bim, cre, cim, df, db):
    g = N_GROUPS
    row_spec = pl.BlockSpec((1, 1, SW), lambda i: (i, 0, 0))
    mat_in = pl.BlockSpec((1, S5_GROUP, SW), lambda i: (i, 0, 0))
    mat_out = pl.BlockSpec((1, CW, SW), lambda i: (i, 0, 0))
    return pl.pallas_call(
        _s5_prep_kernel,
        grid=(g,),
        in_specs=[row_spec, row_spec, row_spec, mat_in, mat_in, mat_in, mat_in,
                  row_spec, row_spec],
        out_specs=[mat_out, mat_out, mat_out, pl.BlockSpec((1, 8, SW), lambda i: (i, 0, 0))],
        out_shape=[jax.ShapeDtypeStruct((g, CW, CW), BF16),
                   jax.ShapeDtypeStruct((g, CW, SW), BF16),
                   jax.ShapeDtypeStruct((g, CW, SW), BF16),
                   jax.ShapeDtypeStruct((g, 8, SW), F32)],
        scratch_shapes=[pltpu.VMEM((N_POW, SW), F32), pltpu.VMEM((N_POW, SW), F32),
                        pltpu.VMEM((2 * CW, SW), F32)],
        name="s5_prep",
    )(lre, lim, ldt, bre, bim, cre, cim, df, db)


def _in_proj_kernel(x_ref, mod_ref, w_ref, b_ref, ua_ref, zr_ref):
    mod = mod_ref[0]
    h = _layer_norm(x_ref[...]) * (1.0 + mod[1:2]) + mod[0:1]
    z = _dot(h.astype(BF16), w_ref[...]) + b_ref[...]
    ua_ref[...] = z[:, :D_MODEL].astype(BF16)
    zr_ref[...] = z[:, D_MODEL:].astype(BF16)


def _in_proj(x, mods, blocks_per_mod, w_in, b_in):
    n_tok = x.shape[0]
    const = lambda i: (0, 0)
    return pl.pallas_call(
        _in_proj_kernel,
        grid=(n_tok // TOK_BLOCK,),
        in_specs=[
            pl.BlockSpec((TOK_BLOCK, D_MODEL), lambda i: (i, 0)),
            pl.BlockSpec((1, N_MOD, D_MODEL), lambda i: (i // blocks_per_mod, 0, 0)),
            pl.BlockSpec((D_MODEL, D_IN), const),
            pl.BlockSpec((1, D_IN), const),
        ],
        out_specs=[pl.BlockSpec((TOK_BLOCK, D_MODEL), lambda i: (i, 0)),
                   pl.BlockSpec((TOK_BLOCK, D_REST), lambda i: (i, 0))],
        out_shape=[jax.ShapeDtypeStruct((n_tok, D_MODEL), BF16),
                   jax.ShapeDtypeStruct((n_tok, D_REST), BF16)],
        compiler_params=pltpu.CompilerParams(vmem_limit_bytes=VMEM_LIMIT),
        name="in_proj",
    )(x, mods, w_in, b_in)


def _s5_core_kernel(ug_ref, m_ref, wb_ref, wct_ref, lam_ref, s0_ref, y_ref, fin_ref,
                    sloc_ref, sprf_ref, sprb_ref, *, segments):
    u = ug_ref[0]
    sloc_ref[...] = _dot(u, wb_ref[0])
    half = 2 * S5_STATE
    ar = lam_ref[0, 0:1, 0:half]
    ai = lam_ref[0, 1:2, 0:half]
    is_f = lax.broadcasted_iota(jnp.int32, (1, half), 1) < S5_STATE

    for base, n_seq, n_chunk, has_init, fin_row in segments:
        if has_init:
            s_re = s0_ref[0, 0]
            s_im = s0_ref[0, 1]
        else:
            s_re = jnp.zeros((n_seq, half), F32)
            s_im = jnp.zeros((n_seq, half), F32)
        for c in range(n_chunk):
            rf = pl.ds(base + c * n_seq, n_seq)
            rb = pl.ds(base + (n_chunk - 1 - c) * n_seq, n_seq)
            sprf_ref[rf, 0:half] = s_re
            sprf_ref[rf, half:SW] = s_im
            sprb_ref[rb, 0:half] = s_re
            sprb_ref[rb, half:SW] = s_im
            l_re = jnp.where(is_f, sloc_ref[rf, 0:half], sloc_ref[rb, 0:half])
            l_im = jnp.where(is_f, sloc_ref[rf, half:SW], sloc_ref[rb, half:SW])
            s_re, s_im = (ar * s_re - ai * s_im + l_re,
                          ar * s_im + ai * s_re + l_im)
        if fin_row is not None:
            fin_ref[0, 0] = s_re
            fin_ref[0, 1] = s_im

    col = lax.broadcasted_iota(jnp.int32, (1, SW), 1)
    col_is_f = (col & (half - 1)) < S5_STATE
    sprev = jnp.where(col_is_f, sprf_ref[...], sprb_ref[...]).astype(BF16)
    y = _dot(u, m_ref[0]) + _dot_nt(sprev, wct_ref[0])
    y_ref[0] = jax.nn.gelu(y).astype(BF16)


def _s5_core(ug, m, wb, wct, lam, s0, segments, n_fin):
    g, rows, _ = ug.shape
    n_init = s0.shape[2]
    mat = pl.BlockSpec((1, CW, SW), lambda i: (i, 0, 0))
    return pl.pallas_call(
        functools.partial(_s5_core_kernel, segments=segments),
        grid=(g,),
        in_specs=[
            pl.BlockSpec((1, rows, CW), lambda i: (i, 0, 0)),
            mat, mat, mat,
            pl.BlockSpec((1, 8, SW), lambda i: (i, 0, 0)),
            pl.BlockSpec((1, 2, n_init, 2 * S5_STATE), lambda i: (i, 0, 0, 0)),
        ],
        out_specs=[pl.BlockSpec((1, rows, CW), lambda i: (i, 0, 0)),
                   pl.BlockSpec((1, 2, n_fin, 2 * S5_STATE), lambda i: (i, 0, 0, 0))],
        out_shape=[jax.ShapeDtypeStruct((g, rows, CW), BF16),
                   jax.ShapeDtypeStruct((g, 2, n_fin, 2 * S5_STATE), F32)],
        scratch_shapes=[pltpu.VMEM((rows, SW), F32), pltpu.VMEM((rows, SW), F32),
                        pltpu.VMEM((rows, SW), F32)],
        name="s5_core",
    )(ug, m, wb, wct, lam, s0)


def _pool_window_bounds(t, w, n):
    r = t & (n - 1)
    base = t - r
    lo = base + jnp.clip(r - w // 2, 0, n)
    hi = base + jnp.clip(r - w // 2 + w, 0, n)
    return lo, hi


def _mix_out_kernel(x_ref, v_ref, zr_ref, mod_ref, wglu_ref, bglu_ref, wpa_ref, wpool_ref,
                    pscale_ref, wpb_ref, wout_ref, bout_ref, g1_ref, b1_ref, o_ref, *, pool_n):
    v = v_ref[...]
    glu = v.astype(F32) * jax.nn.sigmoid(_dot(v, wglu_ref[...]) + bglu_ref[...])
    ya = _dot(glu.astype(BF16), wpa_ref[...])

    ub = zr_ref[:, 0:D_POOL]
    t_sq = lax.broadcasted_iota(jnp.int32, (POOL_TILE, POOL_TILE), 0)
    j_sq = lax.broadcasted_iota(jnp.int32, (POOL_TILE, POOL_TILE), 1)
    t_ln = lax.broadcasted_iota(jnp.int32, (POOL_TILE, POOL_GROUP), 0)
    pooled = []
    for gi, w in enumerate(POOL_WINDOWS):
        lo, hi = _pool_window_bounds(t_sq, w, pool_n)
        win = jnp.where((j_sq >= lo) & (j_sq < hi), 1.0, 0.0).astype(BF16)
        lo, hi = _pool_window_bounds(t_ln, w, pool_n)
        cnt = (hi - lo).astype(F32)
        tiles = []
        for r in range(TOK_BLOCK // POOL_TILE):
            ug = ub[r * POOL_TILE:(r + 1) * POOL_TILE, gi * POOL_GROUP:(gi + 1) * POOL_GROUP]
            tiles.append(_dot(win, ug) / cnt - ug.astype(F32))
        p = jnp.concatenate(tiles, axis=0).astype(BF16)
        pooled.append(_dot(p, wpool_ref[gi]))
    pm = jnp.concatenate(pooled, axis=1) * pscale_ref[...]
    yb = _dot(pm.astype(BF16), wpb_ref[...])

    ga = zr_ref[:, D_POOL:D_POOL + D_MODEL].astype(F32)
    gb = zr_ref[:, D_POOL + D_MODEL:D_REST].astype(F32)
    merged = jax.nn.sigmoid(ga) * ya + jax.nn.sigmoid(gb) * yb
    tm = _dot(merged.astype(BF16), wout_ref[...]) + bout_ref[...]
    mod = mod_ref[0]
    y = DEEPNORM_ALPHA * x_ref[...] + mod[2:3] * tm
    o_ref[...] = _layer_norm(y) * g1_ref[...] + b1_ref[...]


def _mix_out(x, v, zr, mods, blocks_per_mod, pool_n, wglu, bglu, wpa, wpool, pscale, wpb,
             wout, bout, g1, b1):
    n_tok = x.shape[0]
    const2 = lambda i: (0, 0)
    tok = lambda width: pl.BlockSpec((TOK_BLOCK, width), lambda i: (i, 0))
    vec = pl.BlockSpec((1, D_MODEL), const2)
    sq = pl.BlockSpec((D_MODEL, D_MODEL), const2)
    return pl.pallas_call(
        functools.partial(_mix_out_kernel, pool_n=pool_n),
        grid=(n_tok // TOK_BLOCK,),
        in_specs=[
            tok(D_MODEL), tok(D_MODEL), tok(D_REST),
            pl.BlockSpec((1, N_MOD, D_MODEL), lambda i: (i // blocks_per_mod, 0, 0)),
            sq, vec, sq,
            pl.BlockSpec((len(POOL_WINDOWS), POOL_GROUP, POOL_GROUP), lambda i: (0, 0, 0)),
            pl.BlockSpec((1, D_POOL), const2),
            pl.BlockSpec((D_POOL, D_MODEL), const2),
            sq, vec, vec, vec,
        ],
        out_specs=tok(D_MODEL),
        out_shape=jax.ShapeDtypeStruct((n_tok, D_MODEL), F32),
        compiler_params=pltpu.CompilerParams(vmem_limit_bytes=VMEM_LIMIT),
        name="mix_out",
    )(x, v, zr, mods, wglu, bglu, wpa, wpool, pscale, wpb, wout, bout, g1, b1)


def _mlp_kernel(x_ref, mod_ref, w1_ref, b1_ref, w2_ref, b2_ref, g2_ref, be2_ref, o_ref):
    x = x_ref[...]
    mod = mod_ref[0]
    h = (_layer_norm(x) * (1.0 + mod[4:5]) + mod[3:4]).astype(BF16)
    f = jnp.zeros((TOK_BLOCK, D_MODEL), F32)
    for k in range(D_FF // D_MODEL):
        cols = slice(k * D_MODEL, (k + 1) * D_MODEL)
        a = jnp.square(jax.nn.relu(_dot(h, w1_ref[:, cols]) + b1_ref[:, cols]))
        f = f + _dot(a.astype(BF16), w2_ref[cols, :])
    y = DEEPNORM_ALPHA * x + mod[5:6] * (f + b2_ref[...])
    o_ref[...] = _layer_norm(y) * g2_ref[...] + be2_ref[...]


def _mlp(x, mods, blocks_per_mod, w1, b1, w2, b2, g2, be2):
    n_tok = x.shape[0]
    const2 = lambda i: (0, 0)
    vec = pl.BlockSpec((1, D_MODEL), const2)
    return pl.pallas_call(
        _mlp_kernel,
        grid=(n_tok // TOK_BLOCK,),
        in_specs=[
            pl.BlockSpec((TOK_BLOCK, D_MODEL), lambda i: (i, 0)),
            pl.BlockSpec((1, N_MOD, D_MODEL), lambda i: (i // blocks_per_mod, 0, 0)),
            pl.BlockSpec((D_MODEL, D_FF), const2),
            pl.BlockSpec((1, D_FF), const2),
            pl.BlockSpec((D_FF, D_MODEL), const2),
            vec, vec, vec,
        ],
        out_specs=pl.BlockSpec((TOK_BLOCK, D_MODEL), lambda i: (i, 0)),
        out_shape=jax.ShapeDtypeStruct((n_tok, D_MODEL), F32),
        compiler_params=pltpu.CompilerParams(vmem_limit_bytes=VMEM_LIMIT),
        name="mlp",
    )(x, mods, w1, b1, w2, b2, g2, be2)


def _state_cols(x_f, x_b):
    return jnp.concatenate([x_f, x_b, x_f, x_b], axis=-1)


def _to_chunk_rows(ua, n_seq, seq_len):
    n_chunk = seq_len // CHUNK
    u = ua.reshape(n_seq, n_chunk, CHUNK, N_GROUPS, S5_GROUP)
    return u.transpose(3, 1, 0, 2, 4).reshape(N_GROUPS, n_chunk * n_seq, CW)


def _from_chunk_rows(y, n_seq, seq_len):
    n_chunk = seq_len // CHUNK
    v = y.reshape(N_GROUPS, n_chunk, n_seq, CHUNK, S5_GROUP)
    return v.transpose(2, 1, 3, 0, 4).reshape(n_seq * seq_len, D_MODEL)


def kernel(x_prompt, x_sample, state_s5, c, c_ctx, w_ada, b_ada, w_in, b_in, s5_lam_re, s5_lam_im, s5_log_dt, s5_b_re, s5_b_im, s5_c_re, s5_c_im, s5_d, w_glu, b_glu, w_proj_a, w_pool, pool_scale, w_proj_b, w_out, b_out, ln1_g, ln1_b, w_mlp1, b_mlp1, w_mlp2, b_mlp2, ln2_g, ln2_b):
    assert w_in.shape[0] == 1, "single-layer backbone"
    n_ctx, ctx_len, _ = x_prompt.shape
    n_lat, lat_len, _ = x_sample.shape
    g, p, hh = N_GROUPS, S5_STATE, S5_GROUP

    n_vec = 1 + n_lat
    n_rows = -(-n_vec // 8) * 8
    cvec = jnp.concatenate([c_ctx[None, :], c, jnp.zeros((n_rows - n_vec, D_MODEL), F32)], axis=0)
    mods = _mods(cvec, w_ada[0], b_ada[0][None, :]).reshape(n_rows, N_MOD, D_MODEL)
    mods_ctx, mods_lat = mods[0:1], mods[1:n_vec]

    def per_group_row(x):
        return _state_cols(x[0], x[1])[:, None, :]
    lre = per_group_row(s5_lam_re[0])
    lim = per_group_row(s5_lam_im[0])
    ldt = per_group_row(jnp.broadcast_to(s5_log_dt[0][:, :, None], (2, g, p)))
    b_t = lambda x: _state_cols(x[0].transpose(0, 2, 1), x[1].transpose(0, 2, 1))
    c_t = lambda x: _state_cols(x[0], x[1])
    d_t = lambda x: jnp.tile(x.reshape(g, 1, hh), (1, 1, CHUNK))
    m_mat, wb_mat, wct_mat, lam16 = _s5_prep(
        lre, lim, ldt, b_t(s5_b_re[0]), b_t(s5_b_im[0]), c_t(s5_c_re[0]), c_t(s5_c_im[0]),
        d_t(s5_d[0, 0]), d_t(s5_d[0, 1]))

    w_in_b = w_in[0].astype(BF16)
    b_in_r = b_in[0][None, :]
    xp = x_prompt.reshape(n_ctx * ctx_len, D_MODEL)
    xs = x_sample.reshape(n_lat * lat_len, D_MODEL)
    ua_p, zr_p = _in_proj(xp, mods_ctx, xp.shape[0] // TOK_BLOCK, w_in_b, b_in_r)
    ua_s, zr_s = _in_proj(xs, mods_lat, lat_len // TOK_BLOCK, w_in_b, b_in_r)

    ctx_chunks, lat_chunks = ctx_len // CHUNK, lat_len // CHUNK
    ug = jnp.concatenate([_to_chunk_rows(ua_p, n_ctx, ctx_len),
                          _to_chunk_rows(ua_s, n_lat, lat_len)], axis=1)
    st = state_s5[:, 0].astype(F32)
    s0 = jnp.concatenate([st[:, 0], st[:, 1]], axis=-1).transpose(2, 1, 0, 3)
    segments = ((0, n_ctx, ctx_chunks, False, 0),
                (n_ctx * ctx_chunks, n_lat, lat_chunks, True, None))
    yg, fin = _s5_core(ug, m_mat, wb_mat, wct_mat, lam16, s0, segments, n_ctx)
    n_ctx_rows = n_ctx * ctx_chunks
    v_p = _from_chunk_rows(yg[:, :n_ctx_rows], n_ctx, ctx_len)
    v_s = _from_chunk_rows(yg[:, n_ctx_rows:], n_lat, lat_len)
    new_state = fin.reshape(g, 2, n_ctx, 2, p).transpose(2, 3, 1, 0, 4)[:, None]

    mix_w = (w_glu[0].astype(BF16), b_glu[0][None, :], w_proj_a[0].astype(BF16),
             w_pool[0].astype(BF16), pool_scale[0][None, :], w_proj_b[0].astype(BF16),
             w_out[0].astype(BF16), b_out[0][None, :], ln1_g[0][None, :], ln1_b[0][None, :])
    x1_p = _mix_out(xp, v_p, zr_p, mods_ctx, xp.shape[0] // TOK_BLOCK, ctx_len, *mix_w)
    x1_s = _mix_out(xs, v_s, zr_s, mods_lat, lat_len // TOK_BLOCK, GRID_W, *mix_w)

    mlp_w = (w_mlp1[0].astype(BF16), b_mlp1[0][None, :], w_mlp2[0].astype(BF16),
             b_mlp2[0][None, :], ln2_g[0][None, :], ln2_b[0][None, :])
    y_p = _mlp(x1_p, mods_ctx, xp.shape[0] // TOK_BLOCK, *mlp_w)
    y_s = _mlp(x1_s, mods_lat, lat_len // TOK_BLOCK, *mlp_w)
    return (y_p.reshape(x_prompt.shape), y_s.reshape(x_sample.shape), new_state)
```

```python
import functools

import jax
import jax.numpy as jnp
from jax import lax
from jax.experimental import pallas as pl
from jax.experimental.pallas import tpu as pltpu

F32 = jnp.float32
BF16 = jnp.bfloat16

D_MODEL = 1024
S5_GROUP = 16
N_GROUPS = D_MODEL // S5_GROUP
S5_STATE = 64
D_POOL = D_MODEL // 2
POOL_WINDOWS = (2, 4, 8, 16)
POOL_GROUP = D_POOL // len(POOL_WINDOWS)
D_IN = D_MODEL + D_POOL + 2 * D_MODEL
D_REST = D_IN - D_MODEL
D_FF = 4 * D_MODEL
N_MOD = 6
GRID_W = 64
DEEPNORM_ALPHA = 2.0 ** 0.25
LN_EPS = 1e-6

CHUNK = 16
CW = CHUNK * S5_GROUP
SW = 4 * S5_STATE
LANES = 128
N_STRIPS = D_MODEL // LANES
GROUPS_PER_STRIP = LANES // S5_GROUP
N_POW = 32
TOK_BLOCK = 512
POOL_TILE = 256
VMEM_LIMIT = 56 * 1024 * 1024


def _layer_norm(x):
    mu = jnp.mean(x, axis=-1, keepdims=True)
    xc = x - mu
    var = jnp.mean(xc * xc, axis=-1, keepdims=True)
    return xc * lax.rsqrt(var + LN_EPS)


def _dot(a, b):
    return jnp.dot(a, b, preferred_element_type=F32)


def _dot_nt(a, b, precision=None):
    return lax.dot_general(a, b, (((1,), (1,)), ((), ())), precision=precision,
                           preferred_element_type=F32)


def _mods_kernel(c_ref, w_ref, b_ref, o_ref):
    s = jax.nn.silu(c_ref[...])
    o_ref[...] = jnp.dot(s, w_ref[...], precision=lax.Precision.HIGHEST,
                         preferred_element_type=F32) + b_ref[...]


def _mods(cvec, w_ada, b_ada):
    rows = cvec.shape[0]
    n_out = w_ada.shape[1]
    return pl.pallas_call(
        _mods_kernel,
        grid=(n_out // D_MODEL,),
        in_specs=[
            pl.BlockSpec((rows, D_MODEL), lambda j: (0, 0)),
            pl.BlockSpec((D_MODEL, D_MODEL), lambda j: (0, j)),
            pl.BlockSpec((1, D_MODEL), lambda j: (0, j)),
        ],
        out_specs=pl.BlockSpec((rows, D_MODEL), lambda j: (0, j)),
        out_shape=jax.ShapeDtypeStruct((rows, n_out), F32),
        name="mods",
    )(cvec, w_ada, b_ada)


def _s5_prep_kernel(lre_ref, lim_ref, ldt_ref, bre_ref, bim_ref, cre_ref, cim_ref,
                    df_ref, db_ref, m_ref, wb_ref, wct_ref, lam_ref,
                    pwr_ref, pwi_ref, ge_ref):
    lre = lre_ref[0]
    lim = lim_ref[0]
    dt = jnp.exp(ldt_ref[0])
    a = lre * dt
    b = lim * dt
    col = lax.broadcasted_iota(jnp.int32, (1, SW), 1)
    is_im = col >= 2 * S5_STATE
    is_b = (col & (2 * S5_STATE - 1)) >= S5_STATE

    k = lax.broadcasted_iota(jnp.int32, (N_POW, SW), 0).astype(F32)
    mag = jnp.exp(k * a)
    pwr_ref[...] = mag * jnp.cos(k * b)
    pwi_ref[...] = mag * jnp.sin(k * b)

    def power(kf, kb):
        def row(ref, kk):
            return jnp.zeros((1, SW), F32) if kk is None else ref[kk:kk + 1, :]
        if kf == kb:
            return row(pwr_ref, kf), row(pwi_ref, kf)
        return (jnp.where(is_b, row(pwr_ref, kb), row(pwr_ref, kf)),
                jnp.where(is_b, row(pwi_ref, kb), row(pwi_ref, kf)))

    lbr = pwr_ref[1:2, :]
    lbi = pwi_ref[1:2, :]
    den = lre * lre + lim * lim
    nr = lbr - 1.0
    cr = (nr * lre + lbi * lim) / den
    ci = (lbi * lre - nr * lim) / den
    bre = bre_ref[0]
    bim = bim_ref[0]
    bbr = cr * bre - ci * bim
    bbi = cr * bim + ci * bre
    bx = jnp.where(is_im, bbi, bbr)
    by = jnp.where(is_im, bbr, -bbi)
    cre = cre_ref[0]
    cim = cim_ref[0]
    cx = jnp.where(is_im, -cim, cre)
    cy = jnp.where(is_im, -cre, -cim)

    for t in range(CHUNK):
        rows = slice(t * S5_GROUP, (t + 1) * S5_GROUP)
        pr, pi = power(CHUNK - 1 - t, t)
        wb_ref[0, rows, :] = (pr * bx + pi * by).astype(BF16)
        pr, pi = power(t + 1, CHUNK - t)
        wct_ref[0, rows, :] = (pr * cx + pi * cy).astype(BF16)

    for j in range(2 * CHUNK):
        rows = slice(j * S5_GROUP, (j + 1) * S5_GROUP)
        if j == 2 * CHUNK - 1:
            ge_ref[rows, :] = jnp.zeros((S5_GROUP, SW), F32)
            continue
        pr, pi = power(j - (CHUNK - 1) if j >= CHUNK - 1 else None,
                       (CHUNK - 1) - j if j <= CHUNK - 1 else None)
        ge_ref[rows, :] = pr * cx + pi * cy
    e = _dot_nt(bx, ge_ref[...], precision=lax.Precision.HIGHEST)

    dsum = df_ref[0] + db_ref[0]
    r16 = lax.broadcasted_iota(jnp.int32, (S5_GROUP, CW), 0)
    c16 = lax.broadcasted_iota(jnp.int32, (S5_GROUP, CW), 1)
    for tau in range(CHUNK):
        start = (CHUNK - 1 - tau) * S5_GROUP
        blk = e[:, start:start + CW]
        blk = blk + jnp.where(c16 == r16 + tau * S5_GROUP, dsum, 0.0)
        m_ref[0, tau * S5_GROUP:(tau + 1) * S5_GROUP, :] = blk.astype(BF16)

    lam_ref[0] = jnp.concatenate(
        [pwr_ref[CHUNK:CHUNK + 1, :], pwi_ref[CHUNK:CHUNK + 1, :],
         jnp.zeros((6, SW), F32)], axis=0)


def _s5_prep(lre, lim, ldt, bre, bim, cre, cim, df, db):
    g = N_GROUPS
    row_spec = pl.BlockSpec((1, 1, SW), lambda i: (i, 0, 0))
    mat_in = pl.BlockSpec((1, S5_GROUP, SW), lambda i: (i, 0, 0))
    mat_out = pl.BlockSpec((1, CW, SW), lambda i: (i, 0, 0))
    return pl.pallas_call(
        _s5_prep_kernel,
        grid=(g,),
        in_specs=[row_spec, row_spec, row_spec, mat_in, mat_in, mat_in, mat_in,
                  row_spec, row_spec],
        out_specs=[mat_out, mat_out, mat_out, pl.BlockSpec((1, 8, SW), lambda i: (i, 0, 0))],
        out_shape=[jax.ShapeDtypeStruct((g, CW, CW), BF16),
                   jax.ShapeDtypeStruct((g, CW, SW), BF16),
                   jax.ShapeDtypeStruct((g, CW, SW), BF16),
                   jax.ShapeDtypeStruct((g, 8, SW), F32)],
        scratch_shapes=[pltpu.VMEM((N_POW, SW), F32), pltpu.VMEM((N_POW, SW), F32),
                        pltpu.VMEM((2 * CW, SW), F32)],
        name="s5_prep",
    )(lre, lim, ldt, bre, bim, cre, cim, df, db)


def _in_proj_kernel(x_ref, mod_ref, w_ref, b_ref, ua_ref, zr_ref):
    mod = mod_ref[0]
    h = _layer_norm(x_ref[...]) * (1.0 + mod[1:2]) + mod[0:1]
    z = _dot(h.astype(BF16), w_ref[...]) + b_ref[...]
    for j in range(N_STRIPS):
        ua_ref[j] = z[:, j * LANES:(j + 1) * LANES]
    zr_ref[...] = z[:, D_MODEL:].astype(BF16)


def _in_proj(x, mods, blocks_per_mod, w_in, b_in):
    n_tok = x.shape[0]
    const = lambda i: (0, 0)
    return pl.pallas_call(
        _in_proj_kernel,
        grid=(n_tok // TOK_BLOCK,),
        in_specs=[
            pl.BlockSpec((TOK_BLOCK, D_MODEL), lambda i: (i, 0)),
            pl.BlockSpec((1, N_MOD, D_MODEL), lambda i: (i // blocks_per_mod, 0, 0)),
            pl.BlockSpec((D_MODEL, D_IN), const),
            pl.BlockSpec((1, D_IN), const),
        ],
        out_specs=[pl.BlockSpec((N_STRIPS, TOK_BLOCK, LANES), lambda i: (0, i, 0)),
                   pl.BlockSpec((TOK_BLOCK, D_REST), lambda i: (i, 0))],
        out_shape=[jax.ShapeDtypeStruct((N_STRIPS, n_tok, LANES), F32),
                   jax.ShapeDtypeStruct((n_tok, D_REST), BF16)],
        compiler_params=pltpu.CompilerParams(vmem_limit_bytes=VMEM_LIMIT),
        name="in_proj",
    )(x, mods, w_in, b_in)


def _s5_core_kernel(*refs, n_seq, seq_len, has_init):
    if has_init:
        (ua_ref, m_ref, wb_ref, wct_ref, lam_ref, s0_ref, v_ref,
         xt_ref, yt_ref, sloc_ref, sprf_ref, sprb_ref) = refs
    else:
        (ua_ref, m_ref, wb_ref, wct_ref, lam_ref, v_ref, fin_ref,
         xt_ref, yt_ref, sloc_ref, sprf_ref, sprb_ref) = refs
    n_chunk = seq_len // CHUNK
    half = 2 * S5_STATE

    def chunk_rows(c, t):
        return pl.ds(c * CHUNK + t, n_seq, stride=seq_len)

    for tau in range(CHUNK):
        x_tau = jnp.concatenate([ua_ref[0, chunk_rows(c, tau), :] for c in range(n_chunk)], axis=0)
        xt = x_tau.T
        for gi in range(GROUPS_PER_STRIP):
            xt_ref[gi, tau * S5_GROUP:(tau + 1) * S5_GROUP, :] = xt[gi * S5_GROUP:(gi + 1) * S5_GROUP, :]

    is_f = lax.broadcasted_iota(jnp.int32, (1, half), 1) < S5_STATE
    col = lax.broadcasted_iota(jnp.int32, (1, SW), 1)
    col_is_f = (col & (half - 1)) < S5_STATE

    def group_body(gi, carry):
        u = xt_ref[gi].T.astype(BF16)
        sloc_ref[...] = _dot(u, wb_ref[gi])
        ar = lam_ref[gi, 0:1, 0:half]
        ai = lam_ref[gi, 1:2, 0:half]
        if has_init:
            s_re = s0_ref[gi, 0]
            s_im = s0_ref[gi, 1]
        else:
            s_re = jnp.zeros((n_seq, half), F32)
            s_im = jnp.zeros((n_seq, half), F32)
        for c in range(n_chunk):
            rf = pl.ds(c * n_seq, n_seq)
            rb = pl.ds((n_chunk - 1 - c) * n_seq, n_seq)
            sprf_ref[rf, 0:half] = s_re
            sprf_ref[rf, half:SW] = s_im
            sprb_ref[rb, 0:half] = s_re
            sprb_ref[rb, half:SW] = s_im
            l_re = jnp.where(is_f, sloc_ref[rf, 0:half], sloc_ref[rb, 0:half])
            l_im = jnp.where(is_f, sloc_ref[rf, half:SW], sloc_ref[rb, half:SW])
            s_re, s_im = (ar * s_re - ai * s_im + l_re,
                          ar * s_im + ai * s_re + l_im)
        if not has_init:
            fin_ref[gi, 0] = s_re
            fin_ref[gi, 1] = s_im
        sprev = jnp.where(col_is_f, sprf_ref[...], sprb_ref[...]).astype(BF16)
        y = _dot(u, m_ref[gi]) + _dot_nt(sprev, wct_ref[gi])
        yt_ref[gi] = jax.nn.gelu(y).T
        return carry

    lax.fori_loop(0, GROUPS_PER_STRIP, group_body, 0)

    for t in range(CHUNK):
        vt = yt_ref[:, t * S5_GROUP:(t + 1) * S5_GROUP, :].reshape(LANES, n_seq * n_chunk)
        v_t = vt.T
        for c in range(n_chunk):
            v_ref[0, chunk_rows(c, t), :] = v_t[c * n_seq:(c + 1) * n_seq, :]


def _s5_core(ua, m, wb, wct, lam, s0, n_seq, seq_len):
    n_tok = n_seq * seq_len
    rows = n_tok // CHUNK
    gps = GROUPS_PER_STRIP
    has_init = s0 is not None
    strip = pl.BlockSpec((1, n_tok, LANES), lambda i: (i, 0, 0))
    mat = pl.BlockSpec((gps, CW, SW), lambda i: (i, 0, 0))
    state = pl.BlockSpec((gps, 2, n_seq, 2 * S5_STATE), lambda i: (i, 0, 0, 0))
    in_specs = [strip, mat, mat, mat, pl.BlockSpec((gps, 8, SW), lambda i: (i, 0, 0))]
    args = [ua, m, wb, wct, lam]
    out_specs = [strip]
    out_shape = [jax.ShapeDtypeStruct((N_STRIPS, n_tok, LANES), F32)]
    if has_init:
        in_specs.append(state)
        args.append(s0)
    else:
        out_specs.append(state)
        out_shape.append(jax.ShapeDtypeStruct((N_GROUPS, 2, n_seq, 2 * S5_STATE), F32))
    return pl.pallas_call(
        functools.partial(_s5_core_kernel, n_seq=n_seq, seq_len=seq_len, has_init=has_init),
        grid=(N_STRIPS,),
        in_specs=in_specs,
        out_specs=out_specs,
        out_shape=out_shape,
        scratch_shapes=[pltpu.VMEM((gps, CW, rows), F32), pltpu.VMEM((gps, CW, rows), F32),
                        pltpu.VMEM((rows, SW), F32), pltpu.VMEM((rows, SW), F32),
                        pltpu.VMEM((rows, SW), F32)],
        compiler_params=pltpu.CompilerParams(vmem_limit_bytes=VMEM_LIMIT),
        name="s5_core",
    )(*args)


def _pool_window_bounds(t, w, n):
    r = t & (n - 1)
    base = t - r
    lo = base + jnp.clip(r - w // 2, 0, n)
    hi = base + jnp.clip(r - w // 2 + w, 0, n)
    return lo, hi


def _mix_out_kernel(x_ref, v_ref, zr_ref, mod_ref, wglu_ref, bglu_ref, wpa_ref, wpool_ref,
                    pscale_ref, wpb_ref, wout_ref, bout_ref, g1_ref, b1_ref, o_ref, *, pool_n):
    v = jnp.concatenate([v_ref[j] for j in range(N_STRIPS)], axis=1)
    glu = v * jax.nn.sigmoid(_dot(v.astype(BF16), wglu_ref[...]) + bglu_ref[...])
    ya = _dot(glu.astype(BF16), wpa_ref[...])

    ub = zr_ref[:, 0:D_POOL]
    t_sq = lax.broadcasted_iota(jnp.int32, (POOL_TILE, POOL_TILE), 0)
    j_sq = lax.broadcasted_iota(jnp.int32, (POOL_TILE, POOL_TILE), 1)
    t_ln = lax.broadcasted_iota(jnp.int32, (POOL_TILE, POOL_GROUP), 0)
    pooled = []
    for gi, w in enumerate(POOL_WINDOWS):
        lo, hi = _pool_window_bounds(t_sq, w, pool_n)
        win = jnp.where((j_sq >= lo) & (j_sq < hi), 1.0, 0.0).astype(BF16)
        lo, hi = _pool_window_bounds(t_ln, w, pool_n)
        cnt = (hi - lo).astype(F32)
        tiles = []
        for r in range(TOK_BLOCK // POOL_TILE):
            ug = ub[r * POOL_TILE:(r + 1) * POOL_TILE, gi * POOL_GROUP:(gi + 1) * POOL_GROUP]
            tiles.append(_dot(win, ug) / cnt - ug.astype(F32))
        p = jnp.concatenate(tiles, axis=0).astype(BF16)
        pooled.append(_dot(p, wpool_ref[gi]))
    pm = jnp.concatenate(pooled, axis=1) * pscale_ref[...]
    yb = _dot(pm.astype(BF16), wpb_ref[...])

    ga = zr_ref[:, D_POOL:D_POOL + D_MODEL].astype(F32)
    gb = zr_ref[:, D_POOL + D_MODEL:D_REST].astype(F32)
    merged = jax.nn.sigmoid(ga) * ya + jax.nn.sigmoid(gb) * yb
    tm = _dot(merged.astype(BF16), wout_ref[...]) + bout_ref[...]
    mod = mod_ref[0]
    y = DEEPNORM_ALPHA * x_ref[...] + mod[2:3] * tm
    o_ref[...] = _layer_norm(y) * g1_ref[...] + b1_ref[...]


def _mix_out(x, v, zr, mods, blocks_per_mod, pool_n, wglu, bglu, wpa, wpool, pscale, wpb,
             wout, bout, g1, b1):
    n_tok = x.shape[0]
    const2 = lambda i: (0, 0)
    tok = lambda width: pl.BlockSpec((TOK_BLOCK, width), lambda i: (i, 0))
    vec = pl.BlockSpec((1, D_MODEL), const2)
    sq = pl.BlockSpec((D_MODEL, D_MODEL), const2)
    return pl.pallas_call(
        functools.partial(_mix_out_kernel, pool_n=pool_n),
        grid=(n_tok // TOK_BLOCK,),
        in_specs=[
            tok(D_MODEL),
            pl.BlockSpec((N_STRIPS, TOK_BLOCK, LANES), lambda i: (0, i, 0)),
            tok(D_REST),
            pl.BlockSpec((1, N_MOD, D_MODEL), lambda i: (i // blocks_per_mod, 0, 0)),
            sq, vec, sq,
            pl.BlockSpec((len(POOL_WINDOWS), POOL_GROUP, POOL_GROUP), lambda i: (0, 0, 0)),
            pl.BlockSpec((1, D_POOL), const2),
            pl.BlockSpec((D_POOL, D_MODEL), const2),
            sq, vec, vec, vec,
        ],
        out_specs=tok(D_MODEL),
        out_shape=jax.ShapeDtypeStruct((n_tok, D_MODEL), F32),
        compiler_params=pltpu.CompilerParams(vmem_limit_bytes=VMEM_LIMIT),
        name="mix_out",
    )(x, v, zr, mods, wglu, bglu, wpa, wpool, pscale, wpb, wout, bout, g1, b1)


def _mlp_kernel(x_ref, mod_ref, w1_ref, b1_ref, w2_ref, b2_ref, g2_ref, be2_ref, o_ref):
    x = x_ref[...]
    mod = mod_ref[0]
    h = (_layer_norm(x) * (1.0 + mod[4:5]) + mod[3:4]).astype(BF16)
    f = jnp.zeros((TOK_BLOCK, D_MODEL), F32)
    for k in range(D_FF // D_MODEL):
        cols = slice(k * D_MODEL, (k + 1) * D_MODEL)
        a = jnp.square(jax.nn.relu(_dot(h, w1_ref[:, cols]) + b1_ref[:, cols]))
        f = f + _dot(a.astype(BF16), w2_ref[cols, :])
    y = DEEPNORM_ALPHA * x + mod[5:6] * (f + b2_ref[...])
    o_ref[...] = _layer_norm(y) * g2_ref[...] + be2_ref[...]


def _mlp(x, mods, blocks_per_mod, w1, b1, w2, b2, g2, be2):
    n_tok = x.shape[0]
    const2 = lambda i: (0, 0)
    vec = pl.BlockSpec((1, D_MODEL), const2)
    return pl.pallas_call(
        _mlp_kernel,
        grid=(n_tok // TOK_BLOCK,),
        in_specs=[
            pl.BlockSpec((TOK_BLOCK, D_MODEL), lambda i: (i, 0)),
            pl.BlockSpec((1, N_MOD, D_MODEL), lambda i: (i // blocks_per_mod, 0, 0)),
            pl.BlockSpec((D_MODEL, D_FF), const2),
            pl.BlockSpec((1, D_FF), const2),
            pl.BlockSpec((D_FF, D_MODEL), const2),
            vec, vec, vec,
        ],
        out_specs=pl.BlockSpec((TOK_BLOCK, D_MODEL), lambda i: (i, 0)),
        out_shape=jax.ShapeDtypeStruct((n_tok, D_MODEL), F32),
        compiler_params=pltpu.CompilerParams(vmem_limit_bytes=VMEM_LIMIT),
        name="mlp",
    )(x, mods, w1, b1, w2, b2, g2, be2)


def _state_cols(x_f, x_b):
    return jnp.concatenate([x_f, x_b, x_f, x_b], axis=-1)


def kernel(x_prompt, x_sample, state_s5, c, c_ctx, w_ada, b_ada, w_in, b_in, s5_lam_re, s5_lam_im, s5_log_dt, s5_b_re, s5_b_im, s5_c_re, s5_c_im, s5_d, w_glu, b_glu, w_proj_a, w_pool, pool_scale, w_proj_b, w_out, b_out, ln1_g, ln1_b, w_mlp1, b_mlp1, w_mlp2, b_mlp2, ln2_g, ln2_b):
    assert w_in.shape[0] == 1, "single-layer backbone"
    n_ctx, ctx_len, _ = x_prompt.shape
    n_lat, lat_len, _ = x_sample.shape
    g, p, hh = N_GROUPS, S5_STATE, S5_GROUP

    n_vec = 1 + n_lat
    n_rows = -(-n_vec // 8) * 8
    cvec = jnp.concatenate([c_ctx[None, :], c, jnp.zeros((n_rows - n_vec, D_MODEL), F32)], axis=0)
    mods = _mods(cvec, w_ada[0], b_ada[0][None, :]).reshape(n_rows, N_MOD, D_MODEL)
    mods_ctx, mods_lat = mods[0:1], mods[1:n_vec]

    def per_group_row(x):
        return _state_cols(x[0], x[1])[:, None, :]
    lre = per_group_row(s5_lam_re[0])
    lim = per_group_row(s5_lam_im[0])
    ldt = per_group_row(jnp.broadcast_to(s5_log_dt[0][:, :, None], (2, g, p)))
    b_t = lambda x: _state_cols(x[0].transpose(0, 2, 1), x[1].transpose(0, 2, 1))
    c_t = lambda x: _state_cols(x[0], x[1])
    d_t = lambda x: jnp.tile(x.reshape(g, 1, hh), (1, 1, CHUNK))
    m_mat, wb_mat, wct_mat, lam16 = _s5_prep(
        lre, lim, ldt, b_t(s5_b_re[0]), b_t(s5_b_im[0]), c_t(s5_c_re[0]), c_t(s5_c_im[0]),
        d_t(s5_d[0, 0]), d_t(s5_d[0, 1]))

    w_in_b = w_in[0].astype(BF16)
    b_in_r = b_in[0][None, :]
    xp = x_prompt.reshape(n_ctx * ctx_len, D_MODEL)
    xs = x_sample.reshape(n_lat * lat_len, D_MODEL)
    ua_p, zr_p = _in_proj(xp, mods_ctx, xp.shape[0] // TOK_BLOCK, w_in_b, b_in_r)
    ua_s, zr_s = _in_proj(xs, mods_lat, lat_len // TOK_BLOCK, w_in_b, b_in_r)

    st = state_s5[:, 0].astype(F32)
    s0 = jnp.concatenate([st[:, 0], st[:, 1]], axis=-1).transpose(2, 1, 0, 3)
    v_p, fin = _s5_core(ua_p, m_mat, wb_mat, wct_mat, lam16, None, n_ctx, ctx_len)
    v_s, = _s5_core(ua_s, m_mat, wb_mat, wct_mat, lam16, s0, n_lat, lat_len)
    new_state = fin.reshape(g, 2, n_ctx, 2, p).transpose(2, 3, 1, 0, 4)[:, None]

    mix_w = (w_glu[0].astype(BF16), b_glu[0][None, :], w_proj_a[0].astype(BF16),
             w_pool[0].astype(BF16), pool_scale[0][None, :], w_proj_b[0].astype(BF16),
             w_out[0].astype(BF16), b_out[0][None, :], ln1_g[0][None, :], ln1_b[0][None, :])
    x1_p = _mix_out(xp, v_p, zr_p, mods_ctx, xp.shape[0] // TOK_BLOCK, ctx_len, *mix_w)
    x1_s = _mix_out(xs, v_s, zr_s, mods_lat, lat_len // TOK_BLOCK, GRID_W, *mix_w)

    mlp_w = (w_mlp1[0].astype(BF16), b_mlp1[0][None, :], w_mlp2[0].astype(BF16),
             b_mlp2[0][None, :], ln2_g[0][None, :], ln2_b[0][None, :])
    y_p = _mlp(x1_p, mods_ctx, xp.shape[0] // TOK_BLOCK, *mlp_w)
    y_s = _mlp(x1_s, mods_lat, lat_len // TOK_BLOCK, *mlp_w)
    return (y_p.reshape(x_prompt.shape), y_s.reshape(x_sample.shape), new_state)
```

```python
import functools

import jax
import jax.numpy as jnp
from jax import lax
from jax.experimental import pallas as pl
from jax.experimental.pallas import tpu as pltpu

F32 = jnp.float32
BF16 = jnp.bfloat16

D_MODEL = 1024
S5_GROUP = 16
N_GROUPS = D_MODEL // S5_GROUP
S5_STATE = 64
D_POOL = D_MODEL // 2
POOL_WINDOWS = (2, 4, 8, 16)
POOL_GROUP = D_POOL // len(POOL_WINDOWS)
D_IN = D_MODEL + D_POOL + 2 * D_MODEL
D_REST = D_IN - D_MODEL
D_FF = 4 * D_MODEL
N_MOD = 6
GRID_W = 64
DEEPNORM_ALPHA = 2.0 ** 0.25
LN_EPS = 1e-6

CHUNK = 16
CW = CHUNK * S5_GROUP
SW = 4 * S5_STATE
LANES = 128
N_STRIPS = D_MODEL // LANES
GROUPS_PER_STRIP = LANES // S5_GROUP
GROUP_UNROLL = 2
N_POW = 32
TOK_BLOCK = 512
POOL_TILE = 256
VMEM_LIMIT = 56 * 1024 * 1024


def _layer_norm(x):
    mu = jnp.mean(x, axis=-1, keepdims=True)
    xc = x - mu
    var = jnp.mean(xc * xc, axis=-1, keepdims=True)
    return xc * lax.rsqrt(var + LN_EPS)


def _dot(a, b):
    return jnp.dot(a, b, preferred_element_type=F32)


def _dot_nt(a, b, precision=None):
    return lax.dot_general(a, b, (((1,), (1,)), ((), ())), precision=precision,
                           preferred_element_type=F32)


def _mods_kernel(c_ref, w_ref, b_ref, o_ref):
    s = jax.nn.silu(c_ref[...])
    o_ref[...] = jnp.dot(s, w_ref[...], precision=lax.Precision.HIGHEST,
                         preferred_element_type=F32) + b_ref[...]


def _mods(cvec, w_ada, b_ada):
    rows = cvec.shape[0]
    n_out = w_ada.shape[1]
    return pl.pallas_call(
        _mods_kernel,
        grid=(n_out // D_MODEL,),
        in_specs=[
            pl.BlockSpec((rows, D_MODEL), lambda j: (0, 0)),
            pl.BlockSpec((D_MODEL, D_MODEL), lambda j: (0, j)),
            pl.BlockSpec((1, D_MODEL), lambda j: (0, j)),
        ],
        out_specs=pl.BlockSpec((rows, D_MODEL), lambda j: (0, j)),
        out_shape=jax.ShapeDtypeStruct((rows, n_out), F32),
        name="mods",
    )(cvec, w_ada, b_ada)


def _s5_prep_kernel(lre_ref, lim_ref, ldt_ref, bre_ref, bim_ref, cre_ref, cim_ref,
                    df_ref, db_ref, mt_ref, wb_ref, wct_ref, lam_ref,
                    pwr_ref, pwi_ref, ge_ref):
    pg = GROUPS_PER_STRIP
    lre = lre_ref[...]
    lim = lim_ref[...]
    dt = jnp.exp(ldt_ref[...])
    a = lre * dt
    b = lim * dt
    col = lax.broadcasted_iota(jnp.int32, (1, 1, SW), 2)
    is_im = col >= 2 * S5_STATE
    is_b = (col & (2 * S5_STATE - 1)) >= S5_STATE

    k = lax.broadcasted_iota(jnp.int32, (pg, N_POW, SW), 1).astype(F32)
    mag = jnp.exp(k * a)
    pwr_ref[...] = mag * jnp.cos(k * b)
    pwi_ref[...] = mag * jnp.sin(k * b)

    def power(kf, kb):
        def row(ref, kk):
            return jnp.zeros((pg, 1, SW), F32) if kk is None else ref[:, kk:kk + 1, :]
        if kf == kb:
            return row(pwr_ref, kf), row(pwi_ref, kf)
        return (jnp.where(is_b, row(pwr_ref, kb), row(pwr_ref, kf)),
                jnp.where(is_b, row(pwi_ref, kb), row(pwi_ref, kf)))

    lbr = pwr_ref[:, 1:2, :]
    lbi = pwi_ref[:, 1:2, :]
    den = lre * lre + lim * lim
    nr = lbr - 1.0
    cr = (nr * lre + lbi * lim) / den
    ci = (lbi * lre - nr * lim) / den
    bre = bre_ref[...]
    bim = bim_ref[...]
    bbr = cr * bre - ci * bim
    bbi = cr * bim + ci * bre
    bx = jnp.where(is_im, bbi, bbr)
    by = jnp.where(is_im, bbr, -bbi)
    cre = cre_ref[...]
    cim = cim_ref[...]
    cx = jnp.where(is_im, -cim, cre)
    cy = jnp.where(is_im, -cre, -cim)

    for t in range(CHUNK):
        rows = slice(t * S5_GROUP, (t + 1) * S5_GROUP)
        pr, pi = power(CHUNK - 1 - t, t)
        wb_ref[:, rows, :] = (pr * bx + pi * by).astype(BF16)
        pr, pi = power(t + 1, CHUNK - t)
        wct_ref[:, rows, :] = (pr * cx + pi * cy).astype(BF16)

    for j in range(2 * CHUNK):
        rows = slice(j * S5_GROUP, (j + 1) * S5_GROUP)
        if j == 2 * CHUNK - 1:
            ge_ref[:, rows, :] = jnp.zeros((pg, S5_GROUP, SW), F32)
            continue
        pr, pi = power(j - (CHUNK - 1) if j >= CHUNK - 1 else None,
                       (CHUNK - 1) - j if j <= CHUNK - 1 else None)
        ge_ref[:, rows, :] = pr * cx + pi * cy

    dsum = df_ref[...] + db_ref[...]
    r16 = lax.broadcasted_iota(jnp.int32, (S5_GROUP, CW), 0)
    c16 = lax.broadcasted_iota(jnp.int32, (S5_GROUP, CW), 1)
    for gi in range(pg):
        e = _dot_nt(bx[gi], ge_ref[gi], precision=lax.Precision.HIGHEST)
        blocks = []
        for tau in range(CHUNK):
            start = (CHUNK - 1 - tau) * S5_GROUP
            blocks.append(e[:, start:start + CW]
                          + jnp.where(c16 == r16 + tau * S5_GROUP, dsum[gi], 0.0))
        mt_ref[gi] = jnp.concatenate(blocks, axis=0).T.astype(BF16)

    lam_ref[...] = jnp.concatenate(
        [pwr_ref[:, CHUNK:CHUNK + 1, :], pwi_ref[:, CHUNK:CHUNK + 1, :],
         jnp.zeros((pg, 6, SW), F32)], axis=1)


def _s5_prep(lre, lim, ldt, bre, bim, cre, cim, df, db):
    g = N_GROUPS
    pg = GROUPS_PER_STRIP
    row_spec = pl.BlockSpec((pg, 1, SW), lambda i: (i, 0, 0))
    mat_in = pl.BlockSpec((pg, S5_GROUP, SW), lambda i: (i, 0, 0))
    mat_out = pl.BlockSpec((pg, CW, SW), lambda i: (i, 0, 0))
    return pl.pallas_call(
        _s5_prep_kernel,
        grid=(g // pg,),
        in_specs=[row_spec, row_spec, row_spec, mat_in, mat_in, mat_in, mat_in,
                  row_spec, row_spec],
        out_specs=[mat_out, mat_out, mat_out, pl.BlockSpec((pg, 8, SW), lambda i: (i, 0, 0))],
        out_shape=[jax.ShapeDtypeStruct((g, CW, CW), BF16),
                   jax.ShapeDtypeStruct((g, CW, SW), BF16),
                   jax.ShapeDtypeStruct((g, CW, SW), BF16),
                   jax.ShapeDtypeStruct((g, 8, SW), F32)],
        scratch_shapes=[pltpu.VMEM((pg, N_POW, SW), F32), pltpu.VMEM((pg, N_POW, SW), F32),
                        pltpu.VMEM((pg, 2 * CW, SW), F32)],
        name="s5_prep",
    )(lre, lim, ldt, bre, bim, cre, cim, df, db)


def _in_proj_kernel(x_ref, mod_ref, w_ref, b_ref, ua_ref, zr_ref):
    mod = mod_ref[0]
    h = _layer_norm(x_ref[...]) * (1.0 + mod[1:2]) + mod[0:1]
    z = _dot(h.astype(BF16), w_ref[...]) + b_ref[...]
    for j in range(N_STRIPS):
        ua_ref[j] = z[:, j * LANES:(j + 1) * LANES]
    zr_ref[...] = z[:, D_MODEL:].astype(BF16)


def _in_proj(x, mods, blocks_per_mod, w_in, b_in):
    n_tok = x.shape[0]
    const = lambda i: (0, 0)
    return pl.pallas_call(
        _in_proj_kernel,
        grid=(n_tok // TOK_BLOCK,),
        in_specs=[
            pl.BlockSpec((TOK_BLOCK, D_MODEL), lambda i: (i, 0)),
            pl.BlockSpec((1, N_MOD, D_MODEL), lambda i: (i // blocks_per_mod, 0, 0)),
            pl.BlockSpec((D_MODEL, D_IN), const),
            pl.BlockSpec((1, D_IN), const),
        ],
        out_specs=[pl.BlockSpec((N_STRIPS, TOK_BLOCK, LANES), lambda i: (0, i, 0)),
                   pl.BlockSpec((TOK_BLOCK, D_REST), lambda i: (i, 0))],
        out_shape=[jax.ShapeDtypeStruct((N_STRIPS, n_tok, LANES), F32),
                   jax.ShapeDtypeStruct((n_tok, D_REST), BF16)],
        compiler_params=pltpu.CompilerParams(vmem_limit_bytes=VMEM_LIMIT),
        name="in_proj",
    )(x, mods, w_in, b_in)


def _s5_core_kernel(*refs, n_seq, seq_len, has_init):
    if has_init:
        (ua_ref, mt_ref, wb_ref, wct_ref, lam_ref, s0_ref, v_ref,
         xt_ref, yt_ref, sloc_ref, sprf_ref, sprb_ref) = refs
    else:
        (ua_ref, mt_ref, wb_ref, wct_ref, lam_ref, v_ref, fin_ref,
         xt_ref, yt_ref, sloc_ref, sprf_ref, sprb_ref) = refs
    n_chunk = seq_len // CHUNK
    half = 2 * S5_STATE

    def chunk_rows(c, t):
        return pl.ds(c * CHUNK + t, n_seq, stride=seq_len)

    for tau in range(CHUNK):
        x_tau = jnp.concatenate([ua_ref[0, chunk_rows(c, tau), :] for c in range(n_chunk)], axis=0)
        xt = x_tau.T
        for gi in range(GROUPS_PER_STRIP):
            xt_ref[gi, tau * S5_GROUP:(tau + 1) * S5_GROUP, :] = xt[gi * S5_GROUP:(gi + 1) * S5_GROUP, :]

    is_f = lax.broadcasted_iota(jnp.int32, (1, half), 1) < S5_STATE
    col = lax.broadcasted_iota(jnp.int32, (1, SW), 1)
    col_is_f = (col & (half - 1)) < S5_STATE

    def one_group(gi, slot):
        xg = xt_ref[gi]
        u = xg.T.astype(BF16)
        sloc_ref[slot] = _dot(u, wb_ref[gi])
        ar = lam_ref[gi, 0:1, 0:half]
        ai = lam_ref[gi, 1:2, 0:half]
        if has_init:
            s_re = s0_ref[gi, 0]
            s_im = s0_ref[gi, 1]
        else:
            s_re = jnp.zeros((n_seq, half), F32)
            s_im = jnp.zeros((n_seq, half), F32)
        for c in range(n_chunk):
            rf = pl.ds(c * n_seq, n_seq)
            rb = pl.ds((n_chunk - 1 - c) * n_seq, n_seq)
            sprf_ref[slot, rf, 0:half] = s_re
            sprf_ref[slot, rf, half:SW] = s_im
            sprb_ref[slot, rb, 0:half] = s_re
            sprb_ref[slot, rb, half:SW] = s_im
            l_re = jnp.where(is_f, sloc_ref[slot, rf, 0:half], sloc_ref[slot, rb, 0:half])
            l_im = jnp.where(is_f, sloc_ref[slot, rf, half:SW], sloc_ref[slot, rb, half:SW])
            s_re, s_im = (ar * s_re - ai * s_im + l_re,
                          ar * s_im + ai * s_re + l_im)
        if not has_init:
            fin_ref[gi, 0] = s_re
            fin_ref[gi, 1] = s_im
        sprev = jnp.where(col_is_f, sprf_ref[slot], sprb_ref[slot]).astype(BF16)
        yt = _dot(mt_ref[gi], xg.astype(BF16)) + _dot_nt(wct_ref[gi], sprev)
        yt_ref[gi] = jax.nn.gelu(yt)

    def group_body(i, carry):
        for slot in range(GROUP_UNROLL):
            one_group(i * GROUP_UNROLL + slot, slot)
        return carry

    lax.fori_loop(0, GROUPS_PER_STRIP // GROUP_UNROLL, group_body, 0)

    for t in range(CHUNK):
        vt = yt_ref[:, t * S5_GROUP:(t + 1) * S5_GROUP, :].reshape(LANES, n_seq * n_chunk)
        v_t = vt.T
        for c in range(n_chunk):
            v_ref[0, chunk_rows(c, t), :] = v_t[c * n_seq:(c + 1) * n_seq, :]


def _s5_core(ua, m, wb, wct, lam, s0, n_seq, seq_len):
    n_tok = n_seq * seq_len
    rows = n_tok // CHUNK
    gps = GROUPS_PER_STRIP
    has_init = s0 is not None
    strip = pl.BlockSpec((1, n_tok, LANES), lambda i: (i, 0, 0))
    mat = pl.BlockSpec((gps, CW, SW), lambda i: (i, 0, 0))
    state = pl.BlockSpec((gps, 2, n_seq, 2 * S5_STATE), lambda i: (i, 0, 0, 0))
    in_specs = [strip, mat, mat, mat, pl.BlockSpec((gps, 8, SW), lambda i: (i, 0, 0))]
    args = [ua, m, wb, wct, lam]
    out_specs = [strip]
    out_shape = [jax.ShapeDtypeStruct((N_STRIPS, n_tok, LANES), F32)]
    if has_init:
        in_specs.append(state)
        args.append(s0)
    else:
        out_specs.append(state)
        out_shape.append(jax.ShapeDtypeStruct((N_GROUPS, 2, n_seq, 2 * S5_STATE), F32))
    return pl.pallas_call(
        functools.partial(_s5_core_kernel, n_seq=n_seq, seq_len=seq_len, has_init=has_init),
        grid=(N_STRIPS,),
        in_specs=in_specs,
        out_specs=out_specs,
        out_shape=out_shape,
        scratch_shapes=[pltpu.VMEM((gps, CW, rows), F32), pltpu.VMEM((gps, CW, rows), F32),
                        pltpu.VMEM((GROUP_UNROLL, rows, SW), F32),
                        pltpu.VMEM((GROUP_UNROLL, rows, SW), F32),
                        pltpu.VMEM((GROUP_UNROLL, rows, SW), F32)],
        compiler_params=pltpu.CompilerParams(vmem_limit_bytes=VMEM_LIMIT),
        name="s5_core",
    )(*args)


def _pool_window_bounds(t, w, n):
    r = t & (n - 1)
    base = t - r
    lo = base + jnp.clip(r - w // 2, 0, n)
    hi = base + jnp.clip(r - w // 2 + w, 0, n)
    return lo, hi


def _mix_out_kernel(x_ref, v_ref, zr_ref, mod_ref, wglu_ref, bglu_ref, wpa_ref, wpool_ref,
                    pscale_ref, wpb_ref, wout_ref, bout_ref, g1_ref, b1_ref, o_ref, *, pool_n):
    v = jnp.concatenate([v_ref[j] for j in range(N_STRIPS)], axis=1)
    glu = v * jax.nn.sigmoid(_dot(v.astype(BF16), wglu_ref[...]) + bglu_ref[...])
    ya = _dot(glu.astype(BF16), wpa_ref[...])

    ub = zr_ref[:, 0:D_POOL]
    t_sq = lax.broadcasted_iota(jnp.int32, (POOL_TILE, POOL_TILE), 0)
    j_sq = lax.broadcasted_iota(jnp.int32, (POOL_TILE, POOL_TILE), 1)
    t_ln = lax.broadcasted_iota(jnp.int32, (POOL_TILE, POOL_GROUP), 0)
    pooled = []
    for gi, w in enumerate(POOL_WINDOWS):
        lo, hi = _pool_window_bounds(t_sq, w, pool_n)
        win = jnp.where((j_sq >= lo) & (j_sq < hi), 1.0, 0.0).astype(BF16)
        lo, hi = _pool_window_bounds(t_ln, w, pool_n)
        cnt = (hi - lo).astype(F32)
        tiles = []
        for r in range(TOK_BLOCK // POOL_TILE):
            ug = ub[r * POOL_TILE:(r + 1) * POOL_TILE, gi * POOL_GROUP:(gi + 1) * POOL_GROUP]
            tiles.append(_dot(win, ug) / cnt - ug.astype(F32))
        p = jnp.concatenate(tiles, axis=0).astype(BF16)
        pooled.append(_dot(p, wpool_ref[gi]))
    pm = jnp.concatenate(pooled, axis=1) * pscale_ref[...]
    yb = _dot(pm.astype(BF16), wpb_ref[...])

    ga = zr_ref[:, D_POOL:D_POOL + D_MODEL].astype(F32)
    gb = zr_ref[:, D_POOL + D_MODEL:D_REST].astype(F32)
    merged = jax.nn.sigmoid(ga) * ya + jax.nn.sigmoid(gb) * yb
    tm = _dot(merged.astype(BF16), wout_ref[...]) + bout_ref[...]
    mod = mod_ref[0]
    y = DEEPNORM_ALPHA * x_ref[...] + mod[2:3] * tm
    o_ref[...] = _layer_norm(y) * g1_ref[...] + b1_ref[...]


def _mix_out(x, v, zr, mods, blocks_per_mod, pool_n, wglu, bglu, wpa, wpool, pscale, wpb,
             wout, bout, g1, b1):
    n_tok = x.shape[0]
    const2 = lambda i: (0, 0)
    tok = lambda width: pl.BlockSpec((TOK_BLOCK, width), lambda i: (i, 0))
    vec = pl.BlockSpec((1, D_MODEL), const2)
    sq = pl.BlockSpec((D_MODEL, D_MODEL), const2)
    return pl.pallas_call(
        functools.partial(_mix_out_kernel, pool_n=pool_n),
        grid=(n_tok // TOK_BLOCK,),
        in_specs=[
            tok(D_MODEL),
            pl.BlockSpec((N_STRIPS, TOK_BLOCK, LANES), lambda i: (0, i, 0)),
            tok(D_REST),
            pl.BlockSpec((1, N_MOD, D_MODEL), lambda i: (i // blocks_per_mod, 0, 0)),
            sq, vec, sq,
            pl.BlockSpec((len(POOL_WINDOWS), POOL_GROUP, POOL_GROUP), lambda i: (0, 0, 0)),
            pl.BlockSpec((1, D_POOL), const2),
            pl.BlockSpec((D_POOL, D_MODEL), const2),
            sq, vec, vec, vec,
        ],
        out_specs=tok(D_MODEL),
        out_shape=jax.ShapeDtypeStruct((n_tok, D_MODEL), F32),
        compiler_params=pltpu.CompilerParams(vmem_limit_bytes=VMEM_LIMIT),
        name="mix_out",
    )(x, v, zr, mods, wglu, bglu, wpa, wpool, pscale, wpb, wout, bout, g1, b1)


def _mlp_kernel(x_ref, mod_ref, w1_ref, b1_ref, w2_ref, b2_ref, g2_ref, be2_ref, o_ref):
    x = x_ref[...]
    mod = mod_ref[0]
    h = (_layer_norm(x) * (1.0 + mod[4:5]) + mod[3:4]).astype(BF16)
    f = jnp.zeros((TOK_BLOCK, D_MODEL), F32)
    for k in range(D_FF // D_MODEL):
        cols = slice(k * D_MODEL, (k + 1) * D_MODEL)
        a = jnp.square(jax.nn.relu(_dot(h, w1_ref[:, cols]) + b1_ref[:, cols]))
        f = f + _dot(a.astype(BF16), w2_ref[cols, :])
    y = DEEPNORM_ALPHA * x + mod[5:6] * (f + b2_ref[...])
    o_ref[...] = _layer_norm(y) * g2_ref[...] + be2_ref[...]


def _mlp(x, mods, blocks_per_mod, w1, b1, w2, b2, g2, be2):
    n_tok = x.shape[0]
    const2 = lambda i: (0, 0)
    vec = pl.BlockSpec((1, D_MODEL), const2)
    return pl.pallas_call(
        _mlp_kernel,
        grid=(n_tok // TOK_BLOCK,),
        in_specs=[
            pl.BlockSpec((TOK_BLOCK, D_MODEL), lambda i: (i, 0)),
            pl.BlockSpec((1, N_MOD, D_MODEL), lambda i: (i // blocks_per_mod, 0, 0)),
            pl.BlockSpec((D_MODEL, D_FF), const2),
            pl.BlockSpec((1, D_FF), const2),
            pl.BlockSpec((D_FF, D_MODEL), const2),
            vec, vec, vec,
        ],
        out_specs=pl.BlockSpec((TOK_BLOCK, D_MODEL), lambda i: (i, 0)),
        out_shape=jax.ShapeDtypeStruct((n_tok, D_MODEL), F32),
        compiler_params=pltpu.CompilerParams(vmem_limit_bytes=VMEM_LIMIT),
        name="mlp",
    )(x, mods, w1, b1, w2, b2, g2, be2)


def _state_cols(x_f, x_b):
    return jnp.concatenate([x_f, x_b, x_f, x_b], axis=-1)


def kernel(x_prompt, x_sample, state_s5, c, c_ctx, w_ada, b_ada, w_in, b_in, s5_lam_re, s5_lam_im, s5_log_dt, s5_b_re, s5_b_im, s5_c_re, s5_c_im, s5_d, w_glu, b_glu, w_proj_a, w_pool, pool_scale, w_proj_b, w_out, b_out, ln1_g, ln1_b, w_mlp1, b_mlp1, w_mlp2, b_mlp2, ln2_g, ln2_b):
    assert w_in.shape[0] == 1, "single-layer backbone"
    n_ctx, ctx_len, _ = x_prompt.shape
    n_lat, lat_len, _ = x_sample.shape
    g, p, hh = N_GROUPS, S5_STATE, S5_GROUP

    n_vec = 1 + n_lat
    n_rows = -(-n_vec // 8) * 8
    cvec = jnp.concatenate([c_ctx[None, :], c, jnp.zeros((n_rows - n_vec, D_MODEL), F32)], axis=0)
    mods = _mods(cvec, w_ada[0], b_ada[0][None, :]).reshape(n_rows, N_MOD, D_MODEL)
    mods_ctx, mods_lat = mods[0:1], mods[1:n_vec]

    def per_group_row(x):
        return _state_cols(x[0], x[1])[:, None, :]
    lre = per_group_row(s5_lam_re[0])
    lim = per_group_row(s5_lam_im[0])
    ldt = per_group_row(jnp.broadcast_to(s5_log_dt[0][:, :, None], (2, g, p)))
    b_t = lambda x: _state_cols(x[0].transpose(0, 2, 1), x[1].transpose(0, 2, 1))
    c_t = lambda x: _state_cols(x[0], x[1])
    d_t = lambda x: jnp.tile(x.reshape(g, 1, hh), (1, 1, CHUNK))
    m_mat, wb_mat, wct_mat, lam16 = _s5_prep(
        lre, lim, ldt, b_t(s5_b_re[0]), b_t(s5_b_im[0]), c_t(s5_c_re[0]), c_t(s5_c_im[0]),
        d_t(s5_d[0, 0]), d_t(s5_d[0, 1]))

    w_in_b = w_in[0].astype(BF16)
    b_in_r = b_in[0][None, :]
    xp = x_prompt.reshape(n_ctx * ctx_len, D_MODEL)
    xs = x_sample.reshape(n_lat * lat_len, D_MODEL)
    ua_p, zr_p = _in_proj(xp, mods_ctx, xp.shape[0] // TOK_BLOCK, w_in_b, b_in_r)
    ua_s, zr_s = _in_proj(xs, mods_lat, lat_len // TOK_BLOCK, w_in_b, b_in_r)

    st = state_s5[:, 0].astype(F32)
    s0 = jnp.concatenate([st[:, 0], st[:, 1]], axis=-1).transpose(2, 1, 0, 3)
    v_p, fin = _s5_core(ua_p, m_mat, wb_mat, wct_mat, lam16, None, n_ctx, ctx_len)
    v_s, = _s5_core(ua_s, m_mat, wb_mat, wct_mat, lam16, s0, n_lat, lat_len)
    new_state = fin.reshape(g, 2, n_ctx, 2, p).transpose(2, 3, 1, 0, 4)[:, None]

    mix_w = (w_glu[0].astype(BF16), b_glu[0][None, :], w_proj_a[0].astype(BF16),
             w_pool[0].astype(BF16), pool_scale[0][None, :], w_proj_b[0].astype(BF16),
             w_out[0].astype(BF16), b_out[0][None, :], ln1_g[0][None, :], ln1_b[0][None, :])
    x1_p = _mix_out(xp, v_p, zr_p, mods_ctx, xp.shape[0] // TOK_BLOCK, ctx_len, *mix_w)
    x1_s = _mix_out(xs, v_s, zr_s, mods_lat, lat_len // TOK_BLOCK, GRID_W, *mix_w)

    mlp_w = (w_mlp1[0].astype(BF16), b_mlp1[0][None, :], w_mlp2[0].astype(BF16),
             b_mlp2[0][None, :], ln2_g[0][None, :], ln2_b[0][None, :])
    y_p = _mlp(x1_p, mods_ctx, xp.shape[0] // TOK_BLOCK, *mlp_w)
    y_s = _mlp(x1_s, mods_lat, lat_len // TOK_BLOCK, *mlp_w)
    return (y_p.reshape(x_prompt.shape), y_s.reshape(x_sample.shape), new_state)
```

```python
import functools

import jax
import jax.numpy as jnp
from jax import lax
from jax.experimental import pallas as pl
from jax.experimental.pallas import tpu as pltpu

F32 = jnp.float32
BF16 = jnp.bfloat16

D_MODEL = 1024
S5_GROUP = 16
N_GROUPS = D_MODEL // S5_GROUP
S5_STATE = 64
D_POOL = D_MODEL // 2
POOL_WINDOWS = (2, 4, 8, 16)
POOL_GROUP = D_POOL // len(POOL_WINDOWS)
D_IN = D_MODEL + D_POOL + 2 * D_MODEL
D_REST = D_IN - D_MODEL
D_FF = 4 * D_MODEL
N_MOD = 6
GRID_W = 64
DEEPNORM_ALPHA = 2.0 ** 0.25
LN_EPS = 1e-6

CHUNK = 16
CW = CHUNK * S5_GROUP
SW = 4 * S5_STATE
LANES = 128
N_STRIPS = D_MODEL // LANES
GROUPS_PER_STRIP = LANES // S5_GROUP
GROUP_UNROLL = 2
N_POW = 32
SEQ_TILE = 8
TIME_BLOCK = 64
TOK_BLOCK = SEQ_TILE * TIME_BLOCK
HALO = max(POOL_WINDOWS) // 2
VMEM_LIMIT = 56 * 1024 * 1024


def _time_major(x3):
    return jnp.swapaxes(x3, 0, 1).reshape(TOK_BLOCK, x3.shape[-1])


def _seq_major(x2):
    return jnp.swapaxes(x2.reshape(TIME_BLOCK, SEQ_TILE, x2.shape[-1]), 0, 1)


def _modulate(a, mul, add=None):
    a3 = a.reshape(TIME_BLOCK, SEQ_TILE, a.shape[-1]) * mul[None]
    if add is not None:
        a3 = a3 + add[None]
    return a3.reshape(a.shape)


def _layer_norm(x):
    mu = jnp.mean(x, axis=-1, keepdims=True)
    xc = x - mu
    var = jnp.mean(xc * xc, axis=-1, keepdims=True)
    return xc * lax.rsqrt(var + LN_EPS)


def _dot(a, b):
    return jnp.dot(a, b, preferred_element_type=F32)


def _dot_nt(a, b, precision=None):
    return lax.dot_general(a, b, (((1,), (1,)), ((), ())), precision=precision,
                           preferred_element_type=F32)


def _mods_kernel(c_ref, w_ref, b_ref, o_ref):
    s = jax.nn.silu(c_ref[...])
    o_ref[...] = jnp.dot(s, w_ref[...], precision=lax.Precision.HIGHEST,
                         preferred_element_type=F32) + b_ref[...]


def _mods(cvec, w_ada, b_ada):
    rows = cvec.shape[0]
    n_out = w_ada.shape[1]
    return pl.pallas_call(
        _mods_kernel,
        grid=(n_out // D_MODEL,),
        in_specs=[
            pl.BlockSpec((rows, D_MODEL), lambda j: (0, 0)),
            pl.BlockSpec((D_MODEL, D_MODEL), lambda j: (0, j)),
            pl.BlockSpec((1, D_MODEL), lambda j: (0, j)),
        ],
        out_specs=pl.BlockSpec((rows, D_MODEL), lambda j: (0, j)),
        out_shape=jax.ShapeDtypeStruct((rows, n_out), F32),
        name="mods",
    )(cvec, w_ada, b_ada)


def _s5_prep_kernel(lre_ref, lim_ref, ldt_ref, bre_ref, bim_ref, cre_ref, cim_ref,
                    df_ref, db_ref, mt_ref, wb_ref, wct_ref, lam_ref,
                    pwr_ref, pwi_ref, ge_ref):
    pg = GROUPS_PER_STRIP
    lre = lre_ref[...]
    lim = lim_ref[...]
    dt = jnp.exp(ldt_ref[...])
    a = lre * dt
    b = lim * dt
    col = lax.broadcasted_iota(jnp.int32, (1, 1, SW), 2)
    is_im = col >= 2 * S5_STATE
    is_b = (col & (2 * S5_STATE - 1)) >= S5_STATE

    k = lax.broadcasted_iota(jnp.int32, (pg, N_POW, SW), 1).astype(F32)
    mag = jnp.exp(k * a)
    pwr_ref[...] = mag * jnp.cos(k * b)
    pwi_ref[...] = mag * jnp.sin(k * b)

    def power(kf, kb):
        def row(ref, kk):
            return jnp.zeros((pg, 1, SW), F32) if kk is None else ref[:, kk:kk + 1, :]
        if kf == kb:
            return row(pwr_ref, kf), row(pwi_ref, kf)
        return (jnp.where(is_b, row(pwr_ref, kb), row(pwr_ref, kf)),
                jnp.where(is_b, row(pwi_ref, kb), row(pwi_ref, kf)))

    lbr = pwr_ref[:, 1:2, :]
    lbi = pwi_ref[:, 1:2, :]
    den = lre * lre + lim * lim
    nr = lbr - 1.0
    cr = (nr * lre + lbi * lim) / den
    ci = (lbi * lre - nr * lim) / den
    bre = bre_ref[...]
    bim = bim_ref[...]
    bbr = cr * bre - ci * bim
    bbi = cr * bim + ci * bre
    bx = jnp.where(is_im, bbi, bbr)
    by = jnp.where(is_im, bbr, -bbi)
    cre = cre_ref[...]
    cim = cim_ref[...]
    cx = jnp.where(is_im, -cim, cre)
    cy = jnp.where(is_im, -cre, -cim)

    for t in range(CHUNK):
        rows = slice(t * S5_GROUP, (t + 1) * S5_GROUP)
        pr, pi = power(CHUNK - 1 - t, t)
        wb_ref[:, rows, :] = (pr * bx + pi * by).astype(BF16)
        pr, pi = power(t + 1, CHUNK - t)
        wct_ref[:, rows, :] = (pr * cx + pi * cy).astype(BF16)

    for j in range(2 * CHUNK):
        rows = slice(j * S5_GROUP, (j + 1) * S5_GROUP)
        if j == 2 * CHUNK - 1:
            ge_ref[:, rows, :] = jnp.zeros((pg, S5_GROUP, SW), F32)
            continue
        pr, pi = power(j - (CHUNK - 1) if j >= CHUNK - 1 else None,
                       (CHUNK - 1) - j if j <= CHUNK - 1 else None)
        ge_ref[:, rows, :] = pr * cx + pi * cy

    dsum = df_ref[...] + db_ref[...]
    r16 = lax.broadcasted_iota(jnp.int32, (S5_GROUP, CW), 0)
    c16 = lax.broadcasted_iota(jnp.int32, (S5_GROUP, CW), 1)
    for gi in range(pg):
        e = _dot_nt(bx[gi], ge_ref[gi], precision=lax.Precision.HIGHEST)
        blocks = []
        for tau in range(CHUNK):
            start = (CHUNK - 1 - tau) * S5_GROUP
            blocks.append(e[:, start:start + CW]
                          + jnp.where(c16 == r16 + tau * S5_GROUP, dsum[gi], 0.0))
        mt_ref[gi] = jnp.concatenate(blocks, axis=0).T.astype(BF16)

    lam_ref[...] = jnp.concatenate(
        [pwr_ref[:, CHUNK:CHUNK + 1, :], pwi_ref[:, CHUNK:CHUNK + 1, :],
         jnp.zeros((pg, 6, SW), F32)], axis=1)


def _s5_prep(lre, lim, ldt, bre, bim, cre, cim, df, db):
    g = N_GROUPS
    pg = GROUPS_PER_STRIP
    row_spec = pl.BlockSpec((pg, 1, SW), lambda i: (i, 0, 0))
    mat_in = pl.BlockSpec((pg, S5_GROUP, SW), lambda i: (i, 0, 0))
    mat_out = pl.BlockSpec((pg, CW, SW), lambda i: (i, 0, 0))
    return pl.pallas_call(
        _s5_prep_kernel,
        grid=(g // pg,),
        in_specs=[row_spec, row_spec, row_spec, mat_in, mat_in, mat_in, mat_in,
                  row_spec, row_spec],
        out_specs=[mat_out, mat_out, mat_out, pl.BlockSpec((pg, 8, SW), lambda i: (i, 0, 0))],
        out_shape=[jax.ShapeDtypeStruct((g, CW, CW), BF16),
                   jax.ShapeDtypeStruct((g, CW, SW), BF16),
                   jax.ShapeDtypeStruct((g, CW, SW), BF16),
                   jax.ShapeDtypeStruct((g, 8, SW), F32)],
        scratch_shapes=[pltpu.VMEM((pg, N_POW, SW), F32), pltpu.VMEM((pg, N_POW, SW), F32),
                        pltpu.VMEM((pg, 2 * CW, SW), F32)],
        name="s5_prep",
    )(lre, lim, ldt, bre, bim, cre, cim, df, db)


def _token_grid(x):
    n_seq, seq_len, _ = x.shape
    assert n_seq % SEQ_TILE == 0 and seq_len % TIME_BLOCK == 0
    n_tb = seq_len // TIME_BLOCK
    grid = (n_seq // SEQ_TILE, n_tb)
    seq3 = pl.BlockSpec((SEQ_TILE, TIME_BLOCK, D_MODEL), lambda sb, tb: (sb, tb, 0))
    rows = lambda width, shift=0: pl.BlockSpec(
        (TOK_BLOCK, width),
        lambda sb, tb: (sb * n_tb + jnp.clip(tb + shift, 0, n_tb - 1), 0))
    tiles = pl.BlockSpec((1, TIME_BLOCK, SEQ_TILE, D_MODEL), lambda sb, tb: (sb, tb, 0, 0))
    return grid, seq3, rows, tiles


def _mod_spec(mods):
    if mods.shape[1] == 1:
        return pl.BlockSpec((N_MOD, 1, D_MODEL), lambda sb, tb: (0, 0, 0))
    return pl.BlockSpec((N_MOD, SEQ_TILE, D_MODEL), lambda sb, tb: (0, sb, 0))


def _in_proj_kernel(x_ref, mod_ref, w_ref, b_ref, ua_ref, ub_ref, zg_ref):
    x = _time_major(x_ref[...])
    h = _modulate(_layer_norm(x), 1.0 + mod_ref[1], mod_ref[0])
    z = _dot(h.astype(BF16), w_ref[...]) + b_ref[...]
    ua_ref[0] = z[:, :D_MODEL].reshape(TIME_BLOCK, SEQ_TILE, D_MODEL)
    ub_ref[...] = z[:, D_MODEL:D_MODEL + D_POOL].astype(BF16)
    zg_ref[...] = z[:, D_MODEL + D_POOL:].astype(BF16)


def _in_proj(x, mods, w_in, b_in):
    n_seq, seq_len, _ = x.shape
    n_tok = n_seq * seq_len
    grid, seq3, rows, tiles = _token_grid(x)
    const = lambda sb, tb: (0, 0)
    return pl.pallas_call(
        _in_proj_kernel,
        grid=grid,
        in_specs=[
            seq3,
            _mod_spec(mods),
            pl.BlockSpec((D_MODEL, D_IN), const),
            pl.BlockSpec((1, D_IN), const),
        ],
        out_specs=[tiles, rows(D_POOL), rows(2 * D_MODEL)],
        out_shape=[jax.ShapeDtypeStruct((n_seq // SEQ_TILE, seq_len, SEQ_TILE, D_MODEL), F32),
                   jax.ShapeDtypeStruct((n_tok, D_POOL), BF16),
                   jax.ShapeDtypeStruct((n_tok, 2 * D_MODEL), BF16)],
        compiler_params=pltpu.CompilerParams(vmem_limit_bytes=VMEM_LIMIT),
        name="in_proj",
    )(x, mods, w_in, b_in)


def _s5_core_kernel(*refs, n_seq, seq_len, has_init):
    if has_init:
        (ua_ref, mt_ref, wb_ref, wct_ref, lam_ref, s0_ref, v_ref,
         xt_ref, yt_ref, sloc_ref, sprf_ref, sprb_ref) = refs
    else:
        (ua_ref, mt_ref, wb_ref, wct_ref, lam_ref, v_ref, fin_ref,
         xt_ref, yt_ref, sloc_ref, sprf_ref, sprb_ref) = refs
    n_chunk = seq_len // CHUNK
    n_sb = n_seq // SEQ_TILE
    half = 2 * S5_STATE

    for tau in range(CHUNK):
        x_tau = jnp.concatenate([ua_ref[sb, c * CHUNK + tau]
                                 for c in range(n_chunk) for sb in range(n_sb)], axis=0)
        xt = x_tau.T
        for gi in range(GROUPS_PER_STRIP):
            xt_ref[gi, tau * S5_GROUP:(tau + 1) * S5_GROUP, :] = xt[gi * S5_GROUP:(gi + 1) * S5_GROUP, :]

    is_f = lax.broadcasted_iota(jnp.int32, (1, half), 1) < S5_STATE
    col = lax.broadcasted_iota(jnp.int32, (1, SW), 1)
    col_is_f = (col & (half - 1)) < S5_STATE

    def one_group(gi, slot):
        xg = xt_ref[gi]
        u = xg.T.astype(BF16)
        sloc_ref[slot] = _dot(u, wb_ref[gi])
        ar = lam_ref[gi, 0:1, 0:half]
        ai = lam_ref[gi, 1:2, 0:half]
        if has_init:
            s_re = s0_ref[gi, 0]
            s_im = s0_ref[gi, 1]
        else:
            s_re = jnp.zeros((n_seq, half), F32)
            s_im = jnp.zeros((n_seq, half), F32)
        for c in range(n_chunk):
            rf = pl.ds(c * n_seq, n_seq)
            rb = pl.ds((n_chunk - 1 - c) * n_seq, n_seq)
            sprf_ref[slot, rf, 0:half] = s_re
            sprf_ref[slot, rf, half:SW] = s_im
            sprb_ref[slot, rb, 0:half] = s_re
            sprb_ref[slot, rb, half:SW] = s_im
            l_re = jnp.where(is_f, sloc_ref[slot, rf, 0:half], sloc_ref[slot, rb, 0:half])
            l_im = jnp.where(is_f, sloc_ref[slot, rf, half:SW], sloc_ref[slot, rb, half:SW])
            s_re, s_im = (ar * s_re - ai * s_im + l_re,
                          ar * s_im + ai * s_re + l_im)
        if not has_init:
            fin_ref[gi, 0] = s_re
            fin_ref[gi, 1] = s_im
        sprev = jnp.where(col_is_f, sprf_ref[slot], sprb_ref[slot]).astype(BF16)
        yt = _dot(mt_ref[gi], xg.astype(BF16)) + _dot_nt(wct_ref[gi], sprev)
        yt_ref[gi] = jax.nn.gelu(yt)

    def group_body(i, carry):
        for slot in range(GROUP_UNROLL):
            one_group(i * GROUP_UNROLL + slot, slot)
        return carry

    lax.fori_loop(0, GROUPS_PER_STRIP // GROUP_UNROLL, group_body, 0)

    for t in range(CHUNK):
        vt = yt_ref[:, t * S5_GROUP:(t + 1) * S5_GROUP, :].reshape(LANES, n_seq * n_chunk)
        v_t = vt.T
        for c in range(n_chunk):
            for sb in range(n_sb):
                r0 = (c * n_sb + sb) * SEQ_TILE
                v_ref[sb, c * CHUNK + t] = v_t[r0:r0 + SEQ_TILE, :]


def _s5_core(ua, m, wb, wct, lam, s0):
    n_sb, seq_len, _, _ = ua.shape
    n_seq = n_sb * SEQ_TILE
    rows = n_seq * seq_len // CHUNK
    gps = GROUPS_PER_STRIP
    has_init = s0 is not None
    strip = pl.BlockSpec((n_sb, seq_len, SEQ_TILE, LANES), lambda i: (0, 0, 0, i))
    mat = pl.BlockSpec((gps, CW, SW), lambda i: (i, 0, 0))
    state = pl.BlockSpec((gps, 2, n_seq, 2 * S5_STATE), lambda i: (i, 0, 0, 0))
    in_specs = [strip, mat, mat, mat, pl.BlockSpec((gps, 8, SW), lambda i: (i, 0, 0))]
    args = [ua, m, wb, wct, lam]
    out_specs = [strip]
    out_shape = [jax.ShapeDtypeStruct(ua.shape, F32)]
    if has_init:
        in_specs.append(state)
        args.append(s0)
    else:
        out_specs.append(state)
        out_shape.append(jax.ShapeDtypeStruct((N_GROUPS, 2, n_seq, 2 * S5_STATE), F32))
    return pl.pallas_call(
        functools.partial(_s5_core_kernel, n_seq=n_seq, seq_len=seq_len, has_init=has_init),
        grid=(N_STRIPS,),
        in_specs=in_specs,
        out_specs=out_specs,
        out_shape=out_shape,
        scratch_shapes=[pltpu.VMEM((gps, CW, rows), F32), pltpu.VMEM((gps, CW, rows), F32),
                        pltpu.VMEM((GROUP_UNROLL, rows, SW), F32),
                        pltpu.VMEM((GROUP_UNROLL, rows, SW), F32),
                        pltpu.VMEM((GROUP_UNROLL, rows, SW), F32)],
        compiler_params=pltpu.CompilerParams(vmem_limit_bytes=VMEM_LIMIT),
        name="s5_core",
    )(*args)


def _mix_out_kernel(x_ref, v_ref, ubp_ref, ub_ref, ubn_ref, zg_ref, mod_ref, wglu_ref, bglu_ref,
                    wpa_ref, wpool_ref, pscale_ref, wpb_ref, wout_ref, bout_ref, g1_ref, b1_ref,
                    o_ref, *, pool_n):
    v = v_ref[0].reshape(TOK_BLOCK, D_MODEL)
    glu = v * jax.nn.sigmoid(_dot(v.astype(BF16), wglu_ref[...]) + bglu_ref[...])
    ya = _dot(glu.astype(BF16), wpa_ref[...])

    run_blocks = pool_n // TIME_BLOCK
    pos = pl.program_id(1) % run_blocks
    first = pos == 0
    last = pos == run_blocks - 1
    halo_rows = HALO * SEQ_TILE
    u = ub_ref[...].astype(F32)
    u_prev = jnp.where(first, 0.0, ubp_ref[TOK_BLOCK - halo_rows:, :].astype(F32))
    u_next = jnp.where(last, 0.0, ubn_ref[:halo_rows, :].astype(F32))
    ext = jnp.concatenate([u_prev, u, u_next], axis=0)
    t_idx = lax.broadcasted_iota(jnp.int32, (TOK_BLOCK, POOL_GROUP), 0) // SEQ_TILE
    pooled = []
    for gi, w in enumerate(POOL_WINDOWS):
        cols = slice(gi * POOL_GROUP, (gi + 1) * POOL_GROUP)
        acc = jnp.zeros((TOK_BLOCK, POOL_GROUP), F32)
        for k in range(-(w // 2), w - w // 2):
            r0 = (HALO + k) * SEQ_TILE
            acc = acc + ext[r0:r0 + TOK_BLOCK, cols]
        lo = jnp.where(first, jnp.maximum(t_idx - w // 2, 0), t_idx - w // 2)
        hi = jnp.where(last, jnp.minimum(t_idx - w // 2 + w, TIME_BLOCK), t_idx - w // 2 + w)
        p = acc / (hi - lo).astype(F32) - u[:, cols]
        pooled.append(_dot(p.astype(BF16), wpool_ref[gi]))
    pm = jnp.concatenate(pooled, axis=1) * pscale_ref[...]
    yb = _dot(pm.astype(BF16), wpb_ref[...])

    ga = zg_ref[:, 0:D_MODEL].astype(F32)
    gb = zg_ref[:, D_MODEL:2 * D_MODEL].astype(F32)
    merged = jax.nn.sigmoid(ga) * ya + jax.nn.sigmoid(gb) * yb
    tm = _dot(merged.astype(BF16), wout_ref[...]) + bout_ref[...]
    y = DEEPNORM_ALPHA * _time_major(x_ref[...]) + _modulate(tm, mod_ref[2])
    o_ref[...] = _layer_norm(y) * g1_ref[...] + b1_ref[...]


def _mix_out(x, v, ub, zg, mods, pool_n, wglu, bglu, wpa, wpool, pscale, wpb, wout, bout, g1, b1):
    n_seq, seq_len, _ = x.shape
    assert pool_n % TIME_BLOCK == 0 and seq_len % pool_n == 0 and HALO <= TIME_BLOCK
    grid, seq3, rows, tiles = _token_grid(x)
    const2 = lambda sb, tb: (0, 0)
    vec = pl.BlockSpec((1, D_MODEL), const2)
    sq = pl.BlockSpec((D_MODEL, D_MODEL), const2)
    return pl.pallas_call(
        functools.partial(_mix_out_kernel, pool_n=pool_n),
        grid=grid,
        in_specs=[
            seq3, tiles, rows(D_POOL, -1), rows(D_POOL), rows(D_POOL, 1), rows(2 * D_MODEL),
            _mod_spec(mods),
            sq, vec, sq,
            pl.BlockSpec((len(POOL_WINDOWS), POOL_GROUP, POOL_GROUP), lambda sb, tb: (0, 0, 0)),
            pl.BlockSpec((1, D_POOL), const2),
            pl.BlockSpec((D_POOL, D_MODEL), const2),
            sq, vec, vec, vec,
        ],
        out_specs=rows(D_MODEL),
        out_shape=jax.ShapeDtypeStruct((n_seq * seq_len, D_MODEL), F32),
        compiler_params=pltpu.CompilerParams(vmem_limit_bytes=VMEM_LIMIT),
        name="mix_out",
    )(x, v, ub, ub, ub, zg, mods, wglu, bglu, wpa, wpool, pscale, wpb, wout, bout, g1, b1)


def _mlp_kernel(x_ref, mod_ref, w1_ref, b1_ref, w2_ref, b2_ref, g2_ref, be2_ref, o_ref):
    x = x_ref[...]
    h = _modulate(_layer_norm(x), 1.0 + mod_ref[4], mod_ref[3]).astype(BF16)
    f = jnp.zeros((TOK_BLOCK, D_MODEL), F32)
    for k in range(D_FF // D_MODEL):
        cols = slice(k * D_MODEL, (k + 1) * D_MODEL)
        a = jnp.square(jax.nn.relu(_dot(h, w1_ref[:, cols]) + b1_ref[:, cols]))
        f = f + _dot(a.astype(BF16), w2_ref[cols, :])
    y = DEEPNORM_ALPHA * x + _modulate(f + b2_ref[...], mod_ref[5])
    o_ref[...] = _seq_major(_layer_norm(y) * g2_ref[...] + be2_ref[...])


def _mlp(x1, out_like, mods, w1, b1, w2, b2, g2, be2):
    grid, seq3, rows, _ = _token_grid(out_like)
    const2 = lambda sb, tb: (0, 0)
    vec = pl.BlockSpec((1, D_MODEL), const2)
    return pl.pallas_call(
        _mlp_kernel,
        grid=grid,
        in_specs=[
            rows(D_MODEL),
            _mod_spec(mods),
            pl.BlockSpec((D_MODEL, D_FF), const2),
            pl.BlockSpec((1, D_FF), const2),
            pl.BlockSpec((D_FF, D_MODEL), const2),
            vec, vec, vec,
        ],
        out_specs=seq3,
        out_shape=jax.ShapeDtypeStruct(out_like.shape, F32),
        compiler_params=pltpu.CompilerParams(vmem_limit_bytes=VMEM_LIMIT),
        name="mlp",
    )(x1, mods, w1, b1, w2, b2, g2, be2)


def _state_cols(x_f, x_b):
    return jnp.concatenate([x_f, x_b, x_f, x_b], axis=-1)


def kernel(x_prompt, x_sample, state_s5, c, c_ctx, w_ada, b_ada, w_in, b_in, s5_lam_re, s5_lam_im, s5_log_dt, s5_b_re, s5_b_im, s5_c_re, s5_c_im, s5_d, w_glu, b_glu, w_proj_a, w_pool, pool_scale, w_proj_b, w_out, b_out, ln1_g, ln1_b, w_mlp1, b_mlp1, w_mlp2, b_mlp2, ln2_g, ln2_b):
    assert w_in.shape[0] == 1, "single-layer backbone"
    n_ctx, ctx_len, _ = x_prompt.shape
    n_lat, lat_len, _ = x_sample.shape
    g, p, hh = N_GROUPS, S5_STATE, S5_GROUP

    n_vec = 1 + n_lat
    n_rows = -(-n_vec // 8) * 8
    cvec = jnp.concatenate([c_ctx[None, :], c, jnp.zeros((n_rows - n_vec, D_MODEL), F32)], axis=0)
    mods = _mods(cvec, w_ada[0], b_ada[0][None, :]).reshape(n_rows, N_MOD, D_MODEL)
    mods = mods.transpose(1, 0, 2)
    mods_ctx, mods_lat = mods[:, 0:1], mods[:, 1:n_vec]

    def per_group_row(x):
        return _state_cols(x[0], x[1])[:, None, :]
    lre = per_group_row(s5_lam_re[0])
    lim = per_group_row(s5_lam_im[0])
    ldt = per_group_row(jnp.broadcast_to(s5_log_dt[0][:, :, None], (2, g, p)))
    b_t = lambda x: _state_cols(x[0].transpose(0, 2, 1), x[1].transpose(0, 2, 1))
    c_t = lambda x: _state_cols(x[0], x[1])
    d_t = lambda x: jnp.tile(x.reshape(g, 1, hh), (1, 1, CHUNK))
    mt_mat, wb_mat, wct_mat, lam16 = _s5_prep(
        lre, lim, ldt, b_t(s5_b_re[0]), b_t(s5_b_im[0]), c_t(s5_c_re[0]), c_t(s5_c_im[0]),
        d_t(s5_d[0, 0]), d_t(s5_d[0, 1]))

    w_in_b = w_in[0].astype(BF16)
    b_in_r = b_in[0][None, :]
    ua_p, ub_p, zg_p = _in_proj(x_prompt, mods_ctx, w_in_b, b_in_r)
    ua_s, ub_s, zg_s = _in_proj(x_sample, mods_lat, w_in_b, b_in_r)

    st = state_s5[:, 0].astype(F32)
    s0 = jnp.concatenate([st[:, 0], st[:, 1]], axis=-1).transpose(2, 1, 0, 3)
    v_p, fin = _s5_core(ua_p, mt_mat, wb_mat, wct_mat, lam16, None)
    v_s, = _s5_core(ua_s, mt_mat, wb_mat, wct_mat, lam16, s0)
    new_state = fin.reshape(g, 2, n_ctx, 2, p).transpose(2, 3, 1, 0, 4)[:, None]

    mix_w = (w_glu[0].astype(BF16), b_glu[0][None, :], w_proj_a[0].astype(BF16),
             w_pool[0].astype(BF16), pool_scale[0][None, :], w_proj_b[0].astype(BF16),
             w_out[0].astype(BF16), b_out[0][None, :], ln1_g[0][None, :], ln1_b[0][None, :])
    x1_p = _mix_out(x_prompt, v_p, ub_p, zg_p, mods_ctx, ctx_len, *mix_w)
    x1_s = _mix_out(x_sample, v_s, ub_s, zg_s, mods_lat, GRID_W, *mix_w)

    mlp_w = (w_mlp1[0].astype(BF16), b_mlp1[0][None, :], w_mlp2[0].astype(BF16),
             b_mlp2[0][None, :], ln2_g[0][None, :], ln2_b[0][None, :])
    y_p = _mlp(x1_p, x_prompt, mods_ctx, *mlp_w)
    y_s = _mlp(x1_s, x_sample, mods_lat, *mlp_w)
    return (y_p, y_s, new_state)
```

```python
import functools

import jax
import jax.numpy as jnp
from jax import lax
from jax.experimental import pallas as pl
from jax.experimental.pallas import tpu as pltpu

F32 = jnp.float32
BF16 = jnp.bfloat16

D_MODEL = 1024
S5_GROUP = 16
N_GROUPS = D_MODEL // S5_GROUP
S5_STATE = 64
D_POOL = D_MODEL // 2
POOL_WINDOWS = (2, 4, 8, 16)
POOL_GROUP = D_POOL // len(POOL_WINDOWS)
D_IN = D_MODEL + D_POOL + 2 * D_MODEL
D_REST = D_IN - D_MODEL
D_FF = 4 * D_MODEL
N_MOD = 6
GRID_W = 64
DEEPNORM_ALPHA = 2.0 ** 0.25
LN_EPS = 1e-6

CHUNK = 16
CW = CHUNK * S5_GROUP
SW = 4 * S5_STATE
LANES = 128
N_STRIPS = D_MODEL // LANES
GROUPS_PER_STRIP = LANES // S5_GROUP
GROUP_UNROLL = 4
N_POW = 32
SEQ_TILE = 8
TIME_BLOCK = 64
TOK_BLOCK = SEQ_TILE * TIME_BLOCK
HALO = max(POOL_WINDOWS) // 2
MLP_SLICES = 2
MIX_SLICES = 2
VMEM_LIMIT = 56 * 1024 * 1024


def _time_major(x3):
    return jnp.swapaxes(x3, 0, 1).reshape(x3.shape[1] * SEQ_TILE, x3.shape[-1])


def _seq_major(x2):
    return jnp.swapaxes(x2.reshape(x2.shape[0] // SEQ_TILE, SEQ_TILE, x2.shape[-1]), 0, 1)


def _modulate(a, mul, add=None):
    a3 = a.reshape(a.shape[0] // SEQ_TILE, SEQ_TILE, a.shape[-1]) * mul[None]
    if add is not None:
        a3 = a3 + add[None]
    return a3.reshape(a.shape)


def _layer_norm(x):
    mu = jnp.mean(x, axis=-1, keepdims=True)
    xc = x - mu
    var = jnp.mean(xc * xc, axis=-1, keepdims=True)
    return xc * lax.rsqrt(var + LN_EPS)


def _sigmoid(x):
    return 0.5 * jnp.tanh(0.5 * x) + 0.5


def _dot(a, b):
    return jnp.dot(a, b, preferred_element_type=F32)


def _dot_nt(a, b, precision=None):
    return lax.dot_general(a, b, (((1,), (1,)), ((), ())), precision=precision,
                           preferred_element_type=F32)


def _mods_kernel(c_ref, w_ref, b_ref, o_ref):
    s = jax.nn.silu(c_ref[...])
    o_ref[...] = jnp.dot(s, w_ref[...], precision=lax.Precision.HIGHEST,
                         preferred_element_type=F32) + b_ref[...]


def _mods(cvec, w_ada, b_ada):
    rows = cvec.shape[0]
    n_out = w_ada.shape[1]
    return pl.pallas_call(
        _mods_kernel,
        grid=(n_out // D_MODEL,),
        in_specs=[
            pl.BlockSpec((rows, D_MODEL), lambda j: (0, 0)),
            pl.BlockSpec((D_MODEL, D_MODEL), lambda j: (0, j)),
            pl.BlockSpec((1, D_MODEL), lambda j: (0, j)),
        ],
        out_specs=pl.BlockSpec((rows, D_MODEL), lambda j: (0, j)),
        out_shape=jax.ShapeDtypeStruct((rows, n_out), F32),
        name="mods",
    )(cvec, w_ada, b_ada)


def _s5_prep_kernel(lre_ref, lim_ref, ldt_ref, bre_ref, bim_ref, cre_ref, cim_ref,
                    df_ref, db_ref, mt_ref, wb_ref, wct_ref, lam_ref,
                    pwr_ref, pwi_ref, ge_ref):
    pg = GROUPS_PER_STRIP
    lre = lre_ref[...]
    lim = lim_ref[...]
    dt = jnp.exp(ldt_ref[...])
    a = lre * dt
    b = lim * dt
    col = lax.broadcasted_iota(jnp.int32, (1, 1, SW), 2)
    is_im = col >= 2 * S5_STATE
    is_b = (col & (2 * S5_STATE - 1)) >= S5_STATE

    mag = jnp.exp(a)
    sq_r = mag * jnp.cos(b)
    sq_i = mag * jnp.sin(b)
    pwr_ref[:, 0:1, :] = jnp.ones((pg, 1, SW), F32)
    pwi_ref[:, 0:1, :] = jnp.zeros((pg, 1, SW), F32)
    m = 1
    while m < N_POW:
        lo_r = pwr_ref[:, 0:m, :]
        lo_i = pwi_ref[:, 0:m, :]
        pwr_ref[:, m:2 * m, :] = lo_r * sq_r - lo_i * sq_i
        pwi_ref[:, m:2 * m, :] = lo_r * sq_i + lo_i * sq_r
        sq_r, sq_i = sq_r * sq_r - sq_i * sq_i, 2.0 * sq_r * sq_i
        m *= 2

    def power(kf, kb):
        def row(ref, kk):
            return jnp.zeros((pg, 1, SW), F32) if kk is None else ref[:, kk:kk + 1, :]
        if kf == kb:
            return row(pwr_ref, kf), row(pwi_ref, kf)
        return (jnp.where(is_b, row(pwr_ref, kb), row(pwr_ref, kf)),
                jnp.where(is_b, row(pwi_ref, kb), row(pwi_ref, kf)))

    lbr = pwr_ref[:, 1:2, :]
    lbi = pwi_ref[:, 1:2, :]
    den = lre * lre + lim * lim
    nr = lbr - 1.0
    cr = (nr * lre + lbi * lim) / den
    ci = (lbi * lre - nr * lim) / den
    bre = bre_ref[...]
    bim = bim_ref[...]
    bbr = cr * bre - ci * bim
    bbi = cr * bim + ci * bre
    bx = jnp.where(is_im, bbi, bbr)
    by = jnp.where(is_im, bbr, -bbi)
    cre = cre_ref[...]
    cim = cim_ref[...]
    cx = jnp.where(is_im, -cim, cre)
    cy = jnp.where(is_im, -cre, -cim)

    for t in range(CHUNK):
        rows = slice(t * S5_GROUP, (t + 1) * S5_GROUP)
        pr, pi = power(CHUNK - 1 - t, t)
        wb_ref[:, rows, :] = (pr * bx + pi * by).astype(BF16)
        pr, pi = power(t + 1, CHUNK - t)
        wct_ref[:, rows, :] = (pr * cx + pi * cy).astype(BF16)

    for j in range(2 * CHUNK):
        rows = slice(j * S5_GROUP, (j + 1) * S5_GROUP)
        if j == 2 * CHUNK - 1:
            ge_ref[:, rows, :] = jnp.zeros((pg, S5_GROUP, SW), F32)
            continue
        pr, pi = power(j - (CHUNK - 1) if j >= CHUNK - 1 else None,
                       (CHUNK - 1) - j if j <= CHUNK - 1 else None)
        ge_ref[:, rows, :] = pr * cx + pi * cy

    dsum = df_ref[...] + db_ref[...]
    r16 = lax.broadcasted_iota(jnp.int32, (S5_GROUP, CW), 0)
    c16 = lax.broadcasted_iota(jnp.int32, (S5_GROUP, CW), 1)
    for gi in range(pg):
        e = _dot_nt(bx[gi], ge_ref[gi], precision=lax.Precision.HIGHEST)
        blocks = []
        for tau in range(CHUNK):
            start = (CHUNK - 1 - tau) * S5_GROUP
            blocks.append(e[:, start:start + CW]
                          + jnp.where(c16 == r16 + tau * S5_GROUP, dsum[gi], 0.0))
        mt_ref[gi] = jnp.concatenate(blocks, axis=0).T.astype(BF16)

    lam_ref[...] = jnp.concatenate(
        [pwr_ref[:, CHUNK:CHUNK + 1, :], pwi_ref[:, CHUNK:CHUNK + 1, :],
         jnp.zeros((pg, 6, SW), F32)], axis=1)


def _s5_prep(lre, lim, ldt, bre, bim, cre, cim, df, db):
    g = N_GROUPS
    pg = GROUPS_PER_STRIP
    row_spec = pl.BlockSpec((pg, 1, SW), lambda i: (i, 0, 0))
    mat_in = pl.BlockSpec((pg, S5_GROUP, SW), lambda i: (i, 0, 0))
    mat_out = pl.BlockSpec((pg, CW, SW), lambda i: (i, 0, 0))
    return pl.pallas_call(
        _s5_prep_kernel,
        grid=(g // pg,),
        in_specs=[row_spec, row_spec, row_spec, mat_in, mat_in, mat_in, mat_in,
                  row_spec, row_spec],
        out_specs=[mat_out, mat_out, mat_out, pl.BlockSpec((pg, 8, SW), lambda i: (i, 0, 0))],
        out_shape=[jax.ShapeDtypeStruct((g, CW, CW), BF16),
                   jax.ShapeDtypeStruct((g, CW, SW), BF16),
                   jax.ShapeDtypeStruct((g, CW, SW), BF16),
                   jax.ShapeDtypeStruct((g, 8, SW), F32)],
        scratch_shapes=[pltpu.VMEM((pg, N_POW, SW), F32), pltpu.VMEM((pg, N_POW, SW), F32),
                        pltpu.VMEM((pg, 2 * CW, SW), F32)],
        name="s5_prep",
    )(lre, lim, ldt, bre, bim, cre, cim, df, db)


def _token_grid(x):
    n_seq, seq_len, _ = x.shape
    assert n_seq % SEQ_TILE == 0 and seq_len % TIME_BLOCK == 0
    n_tb = seq_len // TIME_BLOCK
    grid = (n_seq // SEQ_TILE, n_tb)
    seq3 = pl.BlockSpec((SEQ_TILE, TIME_BLOCK, D_MODEL), lambda sb, tb: (sb, tb, 0))
    rows = lambda width, shift=0: pl.BlockSpec(
        (TOK_BLOCK, width),
        lambda sb, tb: (sb * n_tb + jnp.clip(tb + shift, 0, n_tb - 1), 0))
    tiles = pl.BlockSpec((1, TIME_BLOCK, SEQ_TILE, D_MODEL), lambda sb, tb: (sb, tb, 0, 0))
    return grid, seq3, rows, tiles


def _mod_spec(mods):
    if mods.shape[1] == 1:
        return pl.BlockSpec((N_MOD, 1, D_MODEL), lambda sb, tb: (0, 0, 0))
    return pl.BlockSpec((N_MOD, SEQ_TILE, D_MODEL), lambda sb, tb: (0, sb, 0))


def _in_proj_kernel(x_ref, mod_ref, w_ref, b_ref, xt_ref, ua_ref, ub_ref, sg_ref):
    x = _time_major(x_ref[...])
    xt_ref[...] = x
    h = _modulate(_layer_norm(x), 1.0 + mod_ref[1], mod_ref[0])
    z = _dot(h.astype(BF16), w_ref[...]) + b_ref[...]
    ua_ref[0] = z[:, :D_MODEL].reshape(TIME_BLOCK, SEQ_TILE, D_MODEL)
    ub_ref[...] = z[:, D_MODEL:D_MODEL + D_POOL].astype(BF16)
    sg_ref[...] = _sigmoid(z[:, D_MODEL + D_POOL:]).astype(BF16)


def _in_proj(x, mods, w_in, b_in):
    n_seq, seq_len, _ = x.shape
    n_tok = n_seq * seq_len
    grid, seq3, rows, tiles = _token_grid(x)
    const = lambda sb, tb: (0, 0)
    return pl.pallas_call(
        _in_proj_kernel,
        grid=grid,
        in_specs=[
            seq3,
            _mod_spec(mods),
            pl.BlockSpec((D_MODEL, D_IN), const),
            pl.BlockSpec((1, D_IN), const),
        ],
        out_specs=[rows(D_MODEL), tiles, rows(D_POOL), rows(2 * D_MODEL)],
        out_shape=[jax.ShapeDtypeStruct((n_tok, D_MODEL), F32),
                   jax.ShapeDtypeStruct((n_seq // SEQ_TILE, seq_len, SEQ_TILE, D_MODEL), F32),
                   jax.ShapeDtypeStruct((n_tok, D_POOL), BF16),
                   jax.ShapeDtypeStruct((n_tok, 2 * D_MODEL), BF16)],
        compiler_params=pltpu.CompilerParams(vmem_limit_bytes=VMEM_LIMIT),
        name="in_proj",
    )(x, mods, w_in, b_in)


def _s5_core_kernel(*refs, n_seq, seq_len, has_init):
    if has_init:
        (ua_ref, mt_ref, wb_ref, wct_ref, lam_ref, s0_ref, v_ref,
         xt_ref, yt_ref, sloc_ref, sprf_ref, sprb_ref) = refs
    else:
        (ua_ref, mt_ref, wb_ref, wct_ref, lam_ref, v_ref, fin_ref,
         xt_ref, yt_ref, sloc_ref, sprf_ref, sprb_ref) = refs
    n_chunk = seq_len // CHUNK
    n_sb = n_seq // SEQ_TILE
    half = 2 * S5_STATE

    for tau in range(CHUNK):
        x_tau = jnp.concatenate([ua_ref[sb, c * CHUNK + tau]
                                 for c in range(n_chunk) for sb in range(n_sb)], axis=0)
        xt = x_tau.T
        for gi in range(GROUPS_PER_STRIP):
            xt_ref[gi, tau * S5_GROUP:(tau + 1) * S5_GROUP, :] = xt[gi * S5_GROUP:(gi + 1) * S5_GROUP, :]

    is_f = lax.broadcasted_iota(jnp.int32, (1, half), 1) < S5_STATE
    col = lax.broadcasted_iota(jnp.int32, (1, SW), 1)
    col_is_f = (col & (half - 1)) < S5_STATE

    def one_group(gi, slot):
        xg = xt_ref[gi]
        u = xg.T.astype(BF16)
        sloc_ref[slot] = _dot(u, wb_ref[gi])
        ar = lam_ref[gi, 0:1, 0:half]
        ai = lam_ref[gi, 1:2, 0:half]
        if has_init:
            s_re = s0_ref[gi, 0]
            s_im = s0_ref[gi, 1]
        else:
            s_re = jnp.zeros((n_seq, half), F32)
            s_im = jnp.zeros((n_seq, half), F32)
        for c in range(n_chunk):
            rf = pl.ds(c * n_seq, n_seq)
            rb = pl.ds((n_chunk - 1 - c) * n_seq, n_seq)
            sprf_ref[slot, rf, 0:half] = s_re
            sprf_ref[slot, rf, half:SW] = s_im
            sprb_ref[slot, rb, 0:half] = s_re
            sprb_ref[slot, rb, half:SW] = s_im
            l_re = jnp.where(is_f, sloc_ref[slot, rf, 0:half], sloc_ref[slot, rb, 0:half])
            l_im = jnp.where(is_f, sloc_ref[slot, rf, half:SW], sloc_ref[slot, rb, half:SW])
            s_re, s_im = (ar * s_re - ai * s_im + l_re,
                          ar * s_im + ai * s_re + l_im)
        if not has_init:
            fin_ref[gi, 0] = s_re
            fin_ref[gi, 1] = s_im
        sprev = jnp.where(col_is_f, sprf_ref[slot], sprb_ref[slot]).astype(BF16)
        yt = _dot(mt_ref[gi], xg.astype(BF16)) + _dot_nt(wct_ref[gi], sprev)
        yt_ref[gi] = jax.nn.gelu(yt)

    def group_body(i, carry):
        for slot in range(GROUP_UNROLL):
            one_group(i * GROUP_UNROLL + slot, slot)
        return carry

    lax.fori_loop(0, GROUPS_PER_STRIP // GROUP_UNROLL, group_body, 0)

    for t in range(CHUNK):
        vt = yt_ref[:, t * S5_GROUP:(t + 1) * S5_GROUP, :].reshape(LANES, n_seq * n_chunk)
        v_t = vt.T
        for c in range(n_chunk):
            for sb in range(n_sb):
                r0 = (c * n_sb + sb) * SEQ_TILE
                v_ref[sb, c * CHUNK + t] = v_t[r0:r0 + SEQ_TILE, :]


def _s5_core(ua, m, wb, wct, lam, s0):
    n_sb, seq_len, _, _ = ua.shape
    n_seq = n_sb * SEQ_TILE
    rows = n_seq * seq_len // CHUNK
    gps = GROUPS_PER_STRIP
    has_init = s0 is not None
    strip = pl.BlockSpec((n_sb, seq_len, SEQ_TILE, LANES), lambda i: (0, 0, 0, i))
    mat = pl.BlockSpec((gps, CW, SW), lambda i: (i, 0, 0))
    state = pl.BlockSpec((gps, 2, n_seq, 2 * S5_STATE), lambda i: (i, 0, 0, 0))
    in_specs = [strip, mat, mat, mat, pl.BlockSpec((gps, 8, SW), lambda i: (i, 0, 0))]
    args = [ua, m, wb, wct, lam]
    out_specs = [strip]
    out_shape = [jax.ShapeDtypeStruct(ua.shape, F32)]
    if has_init:
        in_specs.append(state)
        args.append(s0)
    else:
        out_specs.append(state)
        out_shape.append(jax.ShapeDtypeStruct((N_GROUPS, 2, n_seq, 2 * S5_STATE), F32))
    return pl.pallas_call(
        functools.partial(_s5_core_kernel, n_seq=n_seq, seq_len=seq_len, has_init=has_init),
        grid=(N_STRIPS,),
        in_specs=in_specs,
        out_specs=out_specs,
        out_shape=out_shape,
        scratch_shapes=[pltpu.VMEM((gps, CW, rows), F32), pltpu.VMEM((gps, CW, rows), F32),
                        pltpu.VMEM((GROUP_UNROLL, rows, SW), F32),
                        pltpu.VMEM((GROUP_UNROLL, rows, SW), F32),
                        pltpu.VMEM((GROUP_UNROLL, rows, SW), F32)],
        compiler_params=pltpu.CompilerParams(vmem_limit_bytes=VMEM_LIMIT),
        name="s5_core",
    )(*args)


def _mix_out_kernel(xt_ref, v_ref, ubp_ref, ub_ref, ubn_ref, sg_ref, mod_ref, wglu_ref, bglu_ref,
                    wpa_ref, wpool_ref, pscale_ref, wpb_ref, wout_ref, bout_ref, g1_ref, b1_ref,
                    o_ref, *, pool_n):
    run_blocks = pool_n // TIME_BLOCK
    pos = pl.program_id(1) % run_blocks
    first = pos == 0
    last = pos == run_blocks - 1
    halo_rows = HALO * SEQ_TILE
    u_prev = jnp.where(first, 0.0, ubp_ref[TOK_BLOCK - halo_rows:, :].astype(F32))
    u_next = jnp.where(last, 0.0, ubn_ref[:halo_rows, :].astype(F32))
    ext = jnp.concatenate([u_prev, ub_ref[...].astype(F32), u_next], axis=0)

    sub_rows = TOK_BLOCK // MIX_SLICES
    slices = [slice(sub * sub_rows, (sub + 1) * sub_rows) for sub in range(MIX_SLICES)]

    def s5_branch(sub):
        tiles = sub_rows // SEQ_TILE
        v = v_ref[0, sub * tiles:(sub + 1) * tiles].reshape(sub_rows, D_MODEL)
        glu = v * _sigmoid(_dot(v.astype(BF16), wglu_ref[...]) + bglu_ref[...])
        return _dot(glu.astype(BF16), wpa_ref[...])

    def pool_branch(sub):
        base = halo_rows + sub * sub_rows
        t_idx = (lax.broadcasted_iota(jnp.int32, (sub_rows, POOL_GROUP), 0)
                 + sub * sub_rows) // SEQ_TILE
        pooled = []
        for gi, w in enumerate(POOL_WINDOWS):
            cols = slice(gi * POOL_GROUP, (gi + 1) * POOL_GROUP)
            acc = jnp.zeros((sub_rows, POOL_GROUP), F32)
            for k in range(-(w // 2), w - w // 2):
                r0 = base + k * SEQ_TILE
                acc = acc + ext[r0:r0 + sub_rows, cols]
            lo = jnp.where(first, jnp.maximum(t_idx - w // 2, 0), t_idx - w // 2)
            hi = jnp.where(last, jnp.minimum(t_idx - w // 2 + w, TIME_BLOCK), t_idx - w // 2 + w)
            p = acc / (hi - lo).astype(F32) - ext[base:base + sub_rows, cols]
            pooled.append(_dot(p.astype(BF16), wpool_ref[gi]))
        pm = jnp.concatenate(pooled, axis=1) * pscale_ref[...]
        return _dot(pm.astype(BF16), wpb_ref[...])

    ya = [s5_branch(sub) for sub in range(MIX_SLICES)]
    yb = [pool_branch(sub) for sub in range(MIX_SLICES)]
    tm = []
    for sub, rows in enumerate(slices):
        merged = (sg_ref[rows, 0:D_MODEL].astype(F32) * ya[sub]
                  + sg_ref[rows, D_MODEL:2 * D_MODEL].astype(F32) * yb[sub])
        tm.append(_dot(merged.astype(BF16), wout_ref[...]) + bout_ref[...])
    for sub, rows in enumerate(slices):
        y = DEEPNORM_ALPHA * xt_ref[rows, :] + _modulate(tm[sub], mod_ref[2])
        o_ref[rows, :] = _layer_norm(y) * g1_ref[...] + b1_ref[...]


def _mix_out(like, xt, v, ub, sg, mods, pool_n, wglu, bglu, wpa, wpool, pscale, wpb, wout, bout,
             g1, b1):
    n_seq, seq_len, _ = like.shape
    assert pool_n % TIME_BLOCK == 0 and seq_len % pool_n == 0 and HALO <= TIME_BLOCK
    grid, _, rows, tiles = _token_grid(like)
    const2 = lambda sb, tb: (0, 0)
    vec = pl.BlockSpec((1, D_MODEL), const2)
    sq = pl.BlockSpec((D_MODEL, D_MODEL), const2)
    return pl.pallas_call(
        functools.partial(_mix_out_kernel, pool_n=pool_n),
        grid=grid,
        in_specs=[
            rows(D_MODEL), tiles, rows(D_POOL, -1), rows(D_POOL), rows(D_POOL, 1),
            rows(2 * D_MODEL),
            _mod_spec(mods),
            sq, vec, sq,
            pl.BlockSpec((len(POOL_WINDOWS), POOL_GROUP, POOL_GROUP), lambda sb, tb: (0, 0, 0)),
            pl.BlockSpec((1, D_POOL), const2),
            pl.BlockSpec((D_POOL, D_MODEL), const2),
            sq, vec, vec, vec,
        ],
        out_specs=rows(D_MODEL),
        out_shape=jax.ShapeDtypeStruct((n_seq * seq_len, D_MODEL), F32),
        compiler_params=pltpu.CompilerParams(vmem_limit_bytes=VMEM_LIMIT),
        name="mix_out",
    )(xt, v, ub, ub, ub, sg, mods, wglu, bglu, wpa, wpool, pscale, wpb, wout, bout, g1, b1)


def _mlp_kernel(x_ref, mod_ref, w1_ref, b1_ref, w2_ref, b2_ref, g2_ref, be2_ref, o_ref):
    sub_time = TIME_BLOCK // MLP_SLICES
    for sub in range(MLP_SLICES):
        x = x_ref[sub * sub_time * SEQ_TILE:(sub + 1) * sub_time * SEQ_TILE, :]
        h = _modulate(_layer_norm(x), 1.0 + mod_ref[4], mod_ref[3]).astype(BF16)
        f = jnp.zeros(x.shape, F32)
        for k in range(D_FF // D_MODEL):
            cols = slice(k * D_MODEL, (k + 1) * D_MODEL)
            a = jnp.square(jax.nn.relu(_dot(h, w1_ref[:, cols]) + b1_ref[:, cols]))
            f = f + _dot(a.astype(BF16), w2_ref[cols, :])
        y = DEEPNORM_ALPHA * x + _modulate(f + b2_ref[...], mod_ref[5])
        o_ref[:, sub * sub_time:(sub + 1) * sub_time, :] = _seq_major(
            _layer_norm(y) * g2_ref[...] + be2_ref[...])


def _mlp(x1, out_like, mods, w1, b1, w2, b2, g2, be2):
    grid, seq3, rows, _ = _token_grid(out_like)
    const2 = lambda sb, tb: (0, 0)
    vec = pl.BlockSpec((1, D_MODEL), const2)
    return pl.pallas_call(
        _mlp_kernel,
        grid=grid,
        in_specs=[
            rows(D_MODEL),
            _mod_spec(mods),
            pl.BlockSpec((D_MODEL, D_FF), const2),
            pl.BlockSpec((1, D_FF), const2),
            pl.BlockSpec((D_FF, D_MODEL), const2),
            vec, vec, vec,
        ],
        out_specs=seq3,
        out_shape=jax.ShapeDtypeStruct(out_like.shape, F32),
        compiler_params=pltpu.CompilerParams(vmem_limit_bytes=VMEM_LIMIT),
        name="mlp",
    )(x1, mods, w1, b1, w2, b2, g2, be2)


def _state_cols(x_f, x_b):
    return jnp.concatenate([x_f, x_b, x_f, x_b], axis=-1)


def kernel(x_prompt, x_sample, state_s5, c, c_ctx, w_ada, b_ada, w_in, b_in, s5_lam_re, s5_lam_im, s5_log_dt, s5_b_re, s5_b_im, s5_c_re, s5_c_im, s5_d, w_glu, b_glu, w_proj_a, w_pool, pool_scale, w_proj_b, w_out, b_out, ln1_g, ln1_b, w_mlp1, b_mlp1, w_mlp2, b_mlp2, ln2_g, ln2_b):
    assert w_in.shape[0] == 1, "single-layer backbone"
    n_ctx, ctx_len, _ = x_prompt.shape
    n_lat, lat_len, _ = x_sample.shape
    g, p, hh = N_GROUPS, S5_STATE, S5_GROUP

    n_vec = 1 + n_lat
    n_rows = -(-n_vec // 8) * 8
    cvec = jnp.concatenate([c_ctx[None, :], c, jnp.zeros((n_rows - n_vec, D_MODEL), F32)], axis=0)
    mods = _mods(cvec, w_ada[0], b_ada[0][None, :]).reshape(n_rows, N_MOD, D_MODEL)
    mods = mods.transpose(1, 0, 2)
    mods_ctx, mods_lat = mods[:, 0:1], mods[:, 1:n_vec]

    def per_group_row(x):
        return _state_cols(x[0], x[1])[:, None, :]
    lre = per_group_row(s5_lam_re[0])
    lim = per_group_row(s5_lam_im[0])
    ldt = per_group_row(jnp.broadcast_to(s5_log_dt[0][:, :, None], (2, g, p)))
    b_t = lambda x: _state_cols(x[0].transpose(0, 2, 1), x[1].transpose(0, 2, 1))
    c_t = lambda x: _state_cols(x[0], x[1])
    d_t = lambda x: jnp.tile(x.reshape(g, 1, hh), (1, 1, CHUNK))
    mt_mat, wb_mat, wct_mat, lam16 = _s5_prep(
        lre, lim, ldt, b_t(s5_b_re[0]), b_t(s5_b_im[0]), c_t(s5_c_re[0]), c_t(s5_c_im[0]),
        d_t(s5_d[0, 0]), d_t(s5_d[0, 1]))

    w_in_b = w_in[0].astype(BF16)
    b_in_r = b_in[0][None, :]
    xt_p, ua_p, ub_p, sg_p = _in_proj(x_prompt, mods_ctx, w_in_b, b_in_r)
    xt_s, ua_s, ub_s, sg_s = _in_proj(x_sample, mods_lat, w_in_b, b_in_r)

    st = state_s5[:, 0].astype(F32)
    s0 = jnp.concatenate([st[:, 0], st[:, 1]], axis=-1).transpose(2, 1, 0, 3)
    v_p, fin = _s5_core(ua_p, mt_mat, wb_mat, wct_mat, lam16, None)
    v_s, = _s5_core(ua_s, mt_mat, wb_mat, wct_mat, lam16, s0)
    new_state = fin.reshape(g, 2, n_ctx, 2, p).transpose(2, 3, 1, 0, 4)[:, None]

    mix_w = (w_glu[0].astype(BF16), b_glu[0][None, :], w_proj_a[0].astype(BF16),
             w_pool[0].astype(BF16), pool_scale[0][None, :], w_proj_b[0].astype(BF16),
             w_out[0].astype(BF16), b_out[0][None, :], ln1_g[0][None, :], ln1_b[0][None, :])
    x1_p = _mix_out(x_prompt, xt_p, v_p, ub_p, sg_p, mods_ctx, ctx_len, *mix_w)
    x1_s = _mix_out(x_sample, xt_s, v_s, ub_s, sg_s, mods_lat, GRID_W, *mix_w)

    mlp_w = (w_mlp1[0].astype(BF16), b_mlp1[0][None, :], w_mlp2[0].astype(BF16),
             b_mlp2[0][None, :], ln2_g[0][None, :], ln2_b[0][None, :])
    y_p = _mlp(x1_p, x_prompt, mods_ctx, *mlp_w)
    y_s = _mlp(x1_s, x_sample, mods_lat, *mlp_w)
    return (y_p, y_s, new_state)
```

```python
import functools
from typing import NamedTuple

import jax
import jax.numpy as jnp
from jax import lax
from jax.experimental import pallas as pl
from jax.experimental.pallas import tpu as pltpu

F32 = jnp.float32
BF16 = jnp.bfloat16

D_MODEL = 1024
S5_GROUP = 16
N_GROUPS = D_MODEL // S5_GROUP
S5_STATE = 64
D_POOL = D_MODEL // 2
POOL_WINDOWS = (2, 4, 8, 16)
POOL_GROUP = D_POOL // len(POOL_WINDOWS)
D_IN = D_MODEL + D_POOL + 2 * D_MODEL
D_REST = D_IN - D_MODEL
D_FF = 4 * D_MODEL
N_MOD = 6
GRID_W = 64
DEEPNORM_ALPHA = 2.0 ** 0.25
LN_EPS = 1e-6

CHUNK = 16
CW = CHUNK * S5_GROUP
SW = 4 * S5_STATE
LANES = 128
N_STRIPS = D_MODEL // LANES
GROUPS_PER_STRIP = LANES // S5_GROUP
GROUP_UNROLL = 4
N_POW = 32
SEQ_TILE = 8
TIME_BLOCK = 64
TOK_BLOCK = SEQ_TILE * TIME_BLOCK
HALO = max(POOL_WINDOWS) // 2
MLP_SLICES = 2
MIX_SLICES = 2
VMEM_LIMIT = 56 * 1024 * 1024


def _time_major(x3):
    return jnp.swapaxes(x3, 0, 1).reshape(x3.shape[1] * SEQ_TILE, x3.shape[-1])


def _seq_major(x2):
    return jnp.swapaxes(x2.reshape(x2.shape[0] // SEQ_TILE, SEQ_TILE, x2.shape[-1]), 0, 1)


def _modulate(a, mul, add=None):
    a3 = a.reshape(a.shape[0] // SEQ_TILE, SEQ_TILE, a.shape[-1]) * mul[None]
    if add is not None:
        a3 = a3 + add[None]
    return a3.reshape(a.shape)


def _layer_norm(x):
    mu = jnp.mean(x, axis=-1, keepdims=True)
    xc = x - mu
    var = jnp.mean(xc * xc, axis=-1, keepdims=True)
    return xc * lax.rsqrt(var + LN_EPS)


def _sigmoid(x):
    return 0.5 * jnp.tanh(0.5 * x) + 0.5


def _dot(a, b):
    return jnp.dot(a, b, preferred_element_type=F32)


def _dot_nt(a, b, precision=None):
    return lax.dot_general(a, b, (((1,), (1,)), ((), ())), precision=precision,
                           preferred_element_type=F32)


def _mods_kernel(c_ref, w_ref, b_ref, o_ref):
    s = jax.nn.silu(c_ref[...])
    o_ref[...] = jnp.dot(s, w_ref[...], precision=lax.Precision.HIGHEST,
                         preferred_element_type=F32) + b_ref[...]


def _mods(cvec, w_ada, b_ada):
    rows = cvec.shape[0]
    n_out = w_ada.shape[1]
    return pl.pallas_call(
        _mods_kernel,
        grid=(n_out // D_MODEL,),
        in_specs=[
            pl.BlockSpec((rows, D_MODEL), lambda j: (0, 0)),
            pl.BlockSpec((D_MODEL, D_MODEL), lambda j: (0, j)),
            pl.BlockSpec((1, D_MODEL), lambda j: (0, j)),
        ],
        out_specs=pl.BlockSpec((rows, D_MODEL), lambda j: (0, j)),
        out_shape=jax.ShapeDtypeStruct((rows, n_out), F32),
        name="mods",
    )(cvec, w_ada, b_ada)


def _s5_prep_kernel(lre_ref, lim_ref, ldt_ref, bre_ref, bim_ref, cre_ref, cim_ref,
                    df_ref, db_ref, mt_ref, wb_ref, wct_ref, lam_ref,
                    pwr_ref, pwi_ref, ge_ref):
    pg = GROUPS_PER_STRIP
    lre = lre_ref[...]
    lim = lim_ref[...]
    dt = jnp.exp(ldt_ref[...])
    a = lre * dt
    b = lim * dt
    col = lax.broadcasted_iota(jnp.int32, (1, 1, SW), 2)
    is_im = col >= 2 * S5_STATE
    is_b = (col & (2 * S5_STATE - 1)) >= S5_STATE

    mag = jnp.exp(a)
    sq_r = mag * jnp.cos(b)
    sq_i = mag * jnp.sin(b)
    pwr_ref[:, 0:1, :] = jnp.ones((pg, 1, SW), F32)
    pwi_ref[:, 0:1, :] = jnp.zeros((pg, 1, SW), F32)
    m = 1
    while m < N_POW:
        lo_r = pwr_ref[:, 0:m, :]
        lo_i = pwi_ref[:, 0:m, :]
        pwr_ref[:, m:2 * m, :] = lo_r * sq_r - lo_i * sq_i
        pwi_ref[:, m:2 * m, :] = lo_r * sq_i + lo_i * sq_r
        sq_r, sq_i = sq_r * sq_r - sq_i * sq_i, 2.0 * sq_r * sq_i
        m *= 2

    def power(kf, kb):
        def row(ref, kk):
            return jnp.zeros((pg, 1, SW), F32) if kk is None else ref[:, kk:kk + 1, :]
        if kf == kb:
            return row(pwr_ref, kf), row(pwi_ref, kf)
        return (jnp.where(is_b, row(pwr_ref, kb), row(pwr_ref, kf)),
                jnp.where(is_b, row(pwi_ref, kb), row(pwi_ref, kf)))

    lbr = pwr_ref[:, 1:2, :]
    lbi = pwi_ref[:, 1:2, :]
    den = lre * lre + lim * lim
    nr = lbr - 1.0
    cr = (nr * lre + lbi * lim) / den
    ci = (lbi * lre - nr * lim) / den
    bre = bre_ref[...]
    bim = bim_ref[...]
    bbr = cr * bre - ci * bim
    bbi = cr * bim + ci * bre
    bx = jnp.where(is_im, bbi, bbr)
    by = jnp.where(is_im, bbr, -bbi)
    cre = cre_ref[...]
    cim = cim_ref[...]
    cx = jnp.where(is_im, -cim, cre)
    cy = jnp.where(is_im, -cre, -cim)

    for t in range(CHUNK):
        rows = slice(t * S5_GROUP, (t + 1) * S5_GROUP)
        pr, pi = power(CHUNK - 1 - t, t)
        wb_ref[:, rows, :] = (pr * bx + pi * by).astype(BF16)
        pr, pi = power(t + 1, CHUNK - t)
        wct_ref[:, rows, :] = (pr * cx + pi * cy).astype(BF16)

    for j in range(2 * CHUNK):
        rows = slice(j * S5_GROUP, (j + 1) * S5_GROUP)
        if j == 2 * CHUNK - 1:
            ge_ref[:, rows, :] = jnp.zeros((pg, S5_GROUP, SW), F32)
            continue
        pr, pi = power(j - (CHUNK - 1) if j >= CHUNK - 1 else None,
                       (CHUNK - 1) - j if j <= CHUNK - 1 else None)
        ge_ref[:, rows, :] = pr * cx + pi * cy

    dsum = df_ref[...] + db_ref[...]
    r16 = lax.broadcasted_iota(jnp.int32, (S5_GROUP, CW), 0)
    c16 = lax.broadcasted_iota(jnp.int32, (S5_GROUP, CW), 1)
    for gi in range(pg):
        e = _dot_nt(bx[gi], ge_ref[gi], precision=lax.Precision.HIGHEST)
        blocks = []
        for tau in range(CHUNK):
            start = (CHUNK - 1 - tau) * S5_GROUP
            blocks.append(e[:, start:start + CW]
                          + jnp.where(c16 == r16 + tau * S5_GROUP, dsum[gi], 0.0))
        mt_ref[gi] = jnp.concatenate(blocks, axis=0).T.astype(BF16)

    lam_ref[...] = jnp.concatenate(
        [pwr_ref[:, CHUNK:CHUNK + 1, :], pwi_ref[:, CHUNK:CHUNK + 1, :],
         jnp.zeros((pg, 6, SW), F32)], axis=1)


def _s5_prep(lre, lim, ldt, bre, bim, cre, cim, df, db):
    g = N_GROUPS
    pg = GROUPS_PER_STRIP
    row_spec = pl.BlockSpec((pg, 1, SW), lambda i: (i, 0, 0))
    mat_in = pl.BlockSpec((pg, S5_GROUP, SW), lambda i: (i, 0, 0))
    mat_out = pl.BlockSpec((pg, CW, SW), lambda i: (i, 0, 0))
    return pl.pallas_call(
        _s5_prep_kernel,
        grid=(g // pg,),
        in_specs=[row_spec, row_spec, row_spec, mat_in, mat_in, mat_in, mat_in,
                  row_spec, row_spec],
        out_specs=[mat_out, mat_out, mat_out, pl.BlockSpec((pg, 8, SW), lambda i: (i, 0, 0))],
        out_shape=[jax.ShapeDtypeStruct((g, CW, CW), BF16),
                   jax.ShapeDtypeStruct((g, CW, SW), BF16),
                   jax.ShapeDtypeStruct((g, CW, SW), BF16),
                   jax.ShapeDtypeStruct((g, 8, SW), F32)],
        scratch_shapes=[pltpu.VMEM((pg, N_POW, SW), F32), pltpu.VMEM((pg, N_POW, SW), F32),
                        pltpu.VMEM((pg, 2 * CW, SW), F32)],
        name="s5_prep",
    )(lre, lim, ldt, bre, bim, cre, cim, df, db)


class _Path(NamedTuple):
    n_seq: int
    seq_len: int
    pool_n: int
    first_block: int

    @property
    def time_blocks(self):
        return self.seq_len // TIME_BLOCK

    @property
    def n_blocks(self):
        return (self.n_seq // SEQ_TILE) * self.time_blocks


def _paths(shapes_and_pool):
    paths, first = [], 0
    for (n_seq, seq_len, _), pool_n in shapes_and_pool:
        assert n_seq % SEQ_TILE == 0 and seq_len % TIME_BLOCK == 0
        assert pool_n % TIME_BLOCK == 0 and seq_len % pool_n == 0 and HALO <= TIME_BLOCK
        run_blocks = pool_n // TIME_BLOCK
        assert run_blocks & (run_blocks - 1) == 0 and first % run_blocks == 0
        paths.append(_Path(n_seq, seq_len, pool_n, first))
        first += paths[-1].n_blocks
    return tuple(paths), first


def _block_coords(i, paths):
    coords = []
    for k, p in enumerate(paths):
        end = p.first_block + p.n_blocks
        inside = (i >= p.first_block) & (i < end)
        j = jnp.clip(i - p.first_block, 0, p.n_blocks - 1)
        coords.append((inside, (j // p.time_blocks) * SEQ_TILE, (j % p.time_blocks) * TIME_BLOCK))
    return coords


def _mod_block(i, paths):
    blk, first_mod = 0, 0
    for p in paths:
        j = jnp.clip(i - p.first_block, 0, p.n_blocks - 1)
        blk = jnp.where(i >= p.first_block, first_mod + j // p.time_blocks, blk)
        first_mod += p.n_seq // SEQ_TILE
    return blk


def _row_spec(width, n_blocks, shift=0):
    return pl.BlockSpec((TOK_BLOCK, width),
                        lambda i: (jnp.clip(i + shift, 0, n_blocks - 1), 0))


def _block_copies(hbm_refs, buf, sem, i, slot, paths, to_hbm):
    for (inside, seq0, t0), hbm in zip(_block_coords(i, paths), hbm_refs):
        copies = []
        for s in range(SEQ_TILE):
            rows = hbm.at[seq0 + s, pl.ds(t0, TIME_BLOCK), :]
            tile_rows = buf.at[slot, :, s, :]
            src, dst = (tile_rows, rows) if to_hbm else (rows, tile_rows)
            copies.append(pltpu.make_async_copy(src, dst, sem.at[slot]))
        yield inside, copies


def _start_block(hbm_refs, buf, sem, i, slot, paths, to_hbm):
    for inside, copies in _block_copies(hbm_refs, buf, sem, i, slot, paths, to_hbm):
        @pl.when(inside)
        def _():
            for c in copies:
                c.start()


def _wait_block(hbm_refs, buf, sem, slot, paths, to_hbm):
    _, copies = next(_block_copies(hbm_refs, buf, sem, 0, slot, paths, to_hbm))
    for c in copies:
        c.wait()


def _in_proj_kernel(*refs, paths):
    n_paths = len(paths)
    x_hbm = refs[:n_paths]
    mod_ref, w_ref, b_ref, xt_ref, ua_ref, ub_ref, sg_ref, xbuf, sem = refs[n_paths:]
    i = pl.program_id(0)
    n = pl.num_programs(0)
    slot = i % 2

    @pl.when(i == 0)
    def _():
        _start_block(x_hbm, xbuf, sem, i, slot, paths, to_hbm=False)

    @pl.when(i + 1 < n)
    def _():
        _start_block(x_hbm, xbuf, sem, i + 1, 1 - slot, paths, to_hbm=False)

    _wait_block(x_hbm, xbuf, sem, slot, paths, to_hbm=False)
    x = xbuf[slot].reshape(TOK_BLOCK, D_MODEL)
    xt_ref[...] = x
    h = _modulate(_layer_norm(x), 1.0 + mod_ref[1], mod_ref[0])
    z = _dot(h.astype(BF16), w_ref[...]) + b_ref[...]
    ua_ref[...] = z[:, :D_MODEL].reshape(TIME_BLOCK, SEQ_TILE, D_MODEL)
    ub_ref[...] = z[:, D_MODEL:D_MODEL + D_POOL].astype(BF16)
    sg_ref[...] = _sigmoid(z[:, D_MODEL + D_POOL:]).astype(BF16)


def _in_proj(xs, paths, n_blocks, mods, w_in, b_in):
    n_tok = n_blocks * TOK_BLOCK
    const = lambda i: (0, 0)
    once = pl.Buffered(1)
    return pl.pallas_call(
        functools.partial(_in_proj_kernel, paths=paths),
        grid=(n_blocks,),
        in_specs=[pl.BlockSpec(memory_space=pl.ANY)] * len(xs) + [
            pl.BlockSpec((N_MOD, SEQ_TILE, D_MODEL), lambda i: (0, _mod_block(i, paths), 0)),
            pl.BlockSpec((D_MODEL, D_IN), const, pipeline_mode=once),
            pl.BlockSpec((1, D_IN), const),
        ],
        out_specs=[_row_spec(D_MODEL, n_blocks),
                   pl.BlockSpec((TIME_BLOCK, SEQ_TILE, D_MODEL), lambda i: (i, 0, 0)),
                   _row_spec(D_POOL, n_blocks), _row_spec(2 * D_MODEL, n_blocks)],
        out_shape=[jax.ShapeDtypeStruct((n_tok, D_MODEL), F32),
                   jax.ShapeDtypeStruct((n_blocks * TIME_BLOCK, SEQ_TILE, D_MODEL), F32),
                   jax.ShapeDtypeStruct((n_tok, D_POOL), BF16),
                   jax.ShapeDtypeStruct((n_tok, 2 * D_MODEL), BF16)],
        scratch_shapes=[pltpu.VMEM((2, TIME_BLOCK, SEQ_TILE, D_MODEL), F32),
                        pltpu.SemaphoreType.DMA((2,))],
        compiler_params=pltpu.CompilerParams(dimension_semantics=("arbitrary",),
                                             vmem_limit_bytes=VMEM_LIMIT),
        name="in_proj",
    )(*xs, mods, w_in, b_in)


def _s5_core_kernel(ua_ref, mt_ref, wb_ref, wct_ref, lam_ref, s0_ref, v_ref, fin_ref,
                    xt_ref, yt_ref, sloc_ref, sprf_ref, sprb_ref, *, paths):
    for k, p in enumerate(paths):
        @pl.when(pl.program_id(1) == k)
        def _():
            _s5_strip(ua_ref, mt_ref, wb_ref, wct_ref, lam_ref, s0_ref, v_ref, fin_ref,
                      xt_ref, yt_ref, sloc_ref, sprf_ref, sprb_ref,
                      n_seq=p.n_seq, seq_len=p.seq_len, has_init=k > 0)


def _s5_strip(ua_ref, mt_ref, wb_ref, wct_ref, lam_ref, s0_ref, v_ref, fin_ref,
              xt_ref, yt_ref, sloc_ref, sprf_ref, sprb_ref, *, n_seq, seq_len, has_init):
    n_chunk = seq_len // CHUNK
    n_sb = n_seq // SEQ_TILE
    half = 2 * S5_STATE

    for tau in range(CHUNK):
        x_tau = jnp.concatenate([ua_ref[sb * seq_len + c * CHUNK + tau]
                                 for c in range(n_chunk) for sb in range(n_sb)], axis=0)
        xt = x_tau.T
        for gi in range(GROUPS_PER_STRIP):
            xt_ref[gi, tau * S5_GROUP:(tau + 1) * S5_GROUP, :] = xt[gi * S5_GROUP:(gi + 1) * S5_GROUP, :]

    is_f = lax.broadcasted_iota(jnp.int32, (1, half), 1) < S5_STATE
    col = lax.broadcasted_iota(jnp.int32, (1, SW), 1)
    col_is_f = (col & (half - 1)) < S5_STATE

    def one_group(gi, slot):
        xg = xt_ref[gi]
        u = xg.T.astype(BF16)
        sloc_ref[slot] = _dot(u, wb_ref[gi])
        ar = lam_ref[gi, 0:1, 0:half]
        ai = lam_ref[gi, 1:2, 0:half]
        if has_init:
            s_re = s0_ref[gi, 0]
            s_im = s0_ref[gi, 1]
        else:
            s_re = jnp.zeros((n_seq, half), F32)
            s_im = jnp.zeros((n_seq, half), F32)
        for c in range(n_chunk):
            rf = pl.ds(c * n_seq, n_seq)
            rb = pl.ds((n_chunk - 1 - c) * n_seq, n_seq)
            sprf_ref[slot, rf, 0:half] = s_re
            sprf_ref[slot, rf, half:SW] = s_im
            sprb_ref[slot, rb, 0:half] = s_re
            sprb_ref[slot, rb, half:SW] = s_im
            l_re = jnp.where(is_f, sloc_ref[slot, rf, 0:half], sloc_ref[slot, rb, 0:half])
            l_im = jnp.where(is_f, sloc_ref[slot, rf, half:SW], sloc_ref[slot, rb, half:SW])
            s_re, s_im = (ar * s_re - ai * s_im + l_re,
                          ar * s_im + ai * s_re + l_im)
        if not has_init:
            fin_ref[gi, 0] = s_re
            fin_ref[gi, 1] = s_im
        sprev = jnp.where(col_is_f, sprf_ref[slot], sprb_ref[slot]).astype(BF16)
        yt = _dot(mt_ref[gi], xg.astype(BF16)) + _dot_nt(wct_ref[gi], sprev)
        yt_ref[gi] = jax.nn.gelu(yt)

    def group_body(i, carry):
        for slot in range(GROUP_UNROLL):
            one_group(i * GROUP_UNROLL + slot, slot)
        return carry

    lax.fori_loop(0, GROUPS_PER_STRIP // GROUP_UNROLL, group_body, 0)

    for t in range(CHUNK):
        vt = yt_ref[:, t * S5_GROUP:(t + 1) * S5_GROUP, :].reshape(LANES, n_seq * n_chunk)
        v_t = vt.T
        for c in range(n_chunk):
            for sb in range(n_sb):
                r0 = (c * n_sb + sb) * SEQ_TILE
                v_ref[sb * seq_len + c * CHUNK + t] = v_t[r0:r0 + SEQ_TILE, :]


def _s5_core(ua, paths, m, wb, wct, lam, s0):
    ctx, lat = paths
    tiles = ctx.n_blocks * TIME_BLOCK
    rows = tiles * SEQ_TILE // CHUNK
    assert lat.n_blocks * TIME_BLOCK == tiles and s0.shape[2] == lat.n_seq
    gps = GROUPS_PER_STRIP
    strip = pl.BlockSpec((tiles, SEQ_TILE, LANES), lambda j, k: (k, 0, j))
    mat = pl.BlockSpec((gps, CW, SW), lambda j, k: (j, 0, 0))
    state = lambda n_seq: pl.BlockSpec((gps, 2, n_seq, 2 * S5_STATE), lambda j, k: (j, 0, 0, 0))
    return pl.pallas_call(
        functools.partial(_s5_core_kernel, paths=paths),
        grid=(N_STRIPS, len(paths)),
        in_specs=[strip, mat, mat, mat, pl.BlockSpec((gps, 8, SW), lambda j, k: (j, 0, 0)),
                  state(lat.n_seq)],
        out_specs=[strip, state(ctx.n_seq)],
        out_shape=[jax.ShapeDtypeStruct(ua.shape, F32),
                   jax.ShapeDtypeStruct((N_GROUPS, 2, ctx.n_seq, 2 * S5_STATE), F32)],
        scratch_shapes=[pltpu.VMEM((gps, CW, rows), F32), pltpu.VMEM((gps, CW, rows), F32),
                        pltpu.VMEM((GROUP_UNROLL, rows, SW), F32),
                        pltpu.VMEM((GROUP_UNROLL, rows, SW), F32),
                        pltpu.VMEM((GROUP_UNROLL, rows, SW), F32)],
        compiler_params=pltpu.CompilerParams(dimension_semantics=("arbitrary", "arbitrary"),
                                             vmem_limit_bytes=VMEM_LIMIT),
        name="s5_core",
    )(ua, m, wb, wct, lam, s0)


def _mix_out_kernel(xt_ref, v_ref, ubp_ref, ub_ref, ubn_ref, sg_ref, mod_ref, wglu_ref, bglu_ref,
                    wpa_ref, wpool_ref, pscale_ref, wpb_ref, wout_ref, bout_ref, g1_ref, b1_ref,
                    o_ref, *, paths):
    i = pl.program_id(0)
    run_mask = 0
    for p in paths:
        run_mask = jnp.where(i >= p.first_block, p.pool_n // TIME_BLOCK - 1, run_mask)
    pos = i & run_mask
    first = pos == 0
    last = pos == run_mask
    halo_rows = HALO * SEQ_TILE
    u_prev = jnp.where(first, 0.0, ubp_ref[TOK_BLOCK - halo_rows:, :].astype(F32))
    u_next = jnp.where(last, 0.0, ubn_ref[:halo_rows, :].astype(F32))
    ext = jnp.concatenate([u_prev, ub_ref[...].astype(F32), u_next], axis=0)

    sub_rows = TOK_BLOCK // MIX_SLICES
    slices = [slice(sub * sub_rows, (sub + 1) * sub_rows) for sub in range(MIX_SLICES)]

    def s5_branch(sub):
        tiles = sub_rows // SEQ_TILE
        v = v_ref[sub * tiles:(sub + 1) * tiles].reshape(sub_rows, D_MODEL)
        glu = v * _sigmoid(_dot(v.astype(BF16), wglu_ref[...]) + bglu_ref[...])
        return _dot(glu.astype(BF16), wpa_ref[...])

    def pool_branch(sub):
        base = halo_rows + sub * sub_rows
        t_idx = (lax.broadcasted_iota(jnp.int32, (sub_rows, POOL_GROUP), 0)
                 + sub * sub_rows) // SEQ_TILE
        pooled = []
        for gi, w in enumerate(POOL_WINDOWS):
            cols = slice(gi * POOL_GROUP, (gi + 1) * POOL_GROUP)
            acc = jnp.zeros((sub_rows, POOL_GROUP), F32)
            for k in range(-(w // 2), w - w // 2):
                r0 = base + k * SEQ_TILE
                acc = acc + ext[r0:r0 + sub_rows, cols]
            lo = jnp.where(first, jnp.maximum(t_idx - w // 2, 0), t_idx - w // 2)
            hi = jnp.where(last, jnp.minimum(t_idx - w // 2 + w, TIME_BLOCK), t_idx - w // 2 + w)
            p = acc / (hi - lo).astype(F32) - ext[base:base + sub_rows, cols]
            pooled.append(_dot(p.astype(BF16), wpool_ref[gi]))
        pm = jnp.concatenate(pooled, axis=1) * pscale_ref[...]
        return _dot(pm.astype(BF16), wpb_ref[...])

    ya = [s5_branch(sub) for sub in range(MIX_SLICES)]
    yb = [pool_branch(sub) for sub in range(MIX_SLICES)]
    tm = []
    for sub, rows in enumerate(slices):
        merged = (sg_ref[rows, 0:D_MODEL].astype(F32) * ya[sub]
                  + sg_ref[rows, D_MODEL:2 * D_MODEL].astype(F32) * yb[sub])
        tm.append(_dot(merged.astype(BF16), wout_ref[...]) + bout_ref[...])
    for sub, rows in enumerate(slices):
        y = DEEPNORM_ALPHA * xt_ref[rows, :] + _modulate(tm[sub], mod_ref[2])
        o_ref[rows, :] = _layer_norm(y) * g1_ref[...] + b1_ref[...]


def _mix_out(xt, v, ub, sg, paths, n_blocks, mods, wglu, bglu, wpa, wpool, pscale, wpb, wout,
             bout, g1, b1):
    const2 = lambda i: (0, 0)
    once = pl.Buffered(1)
    vec = pl.BlockSpec((1, D_MODEL), const2)
    sq = pl.BlockSpec((D_MODEL, D_MODEL), const2, pipeline_mode=once)
    rows = functools.partial(_row_spec, n_blocks=n_blocks)
    return pl.pallas_call(
        functools.partial(_mix_out_kernel, paths=paths),
        grid=(n_blocks,),
        in_specs=[
            rows(D_MODEL),
            pl.BlockSpec((TIME_BLOCK, SEQ_TILE, D_MODEL), lambda i: (i, 0, 0)),
            rows(D_POOL, shift=-1), rows(D_POOL), rows(D_POOL, shift=1),
            rows(2 * D_MODEL),
            pl.BlockSpec((N_MOD, SEQ_TILE, D_MODEL), lambda i: (0, _mod_block(i, paths), 0)),
            sq, vec, sq,
            pl.BlockSpec((len(POOL_WINDOWS), POOL_GROUP, POOL_GROUP), lambda i: (0, 0, 0)),
            pl.BlockSpec((1, D_POOL), const2),
            pl.BlockSpec((D_POOL, D_MODEL), const2, pipeline_mode=once),
            sq, vec, vec, vec,
        ],
        out_specs=rows(D_MODEL),
        out_shape=jax.ShapeDtypeStruct((n_blocks * TOK_BLOCK, D_MODEL), F32),
        compiler_params=pltpu.CompilerParams(vmem_limit_bytes=VMEM_LIMIT),
        name="mix_out",
    )(xt, v, ub, ub, ub, sg, mods, wglu, bglu, wpa, wpool, pscale, wpb, wout, bout, g1, b1)


def _mlp_kernel(*refs, paths):
    n_paths = len(paths)
    x_ref, mod_ref, w1_ref, b1_ref, w2_ref, b2_ref, g2_ref, be2_ref = refs[:8]
    y_hbm = refs[8:8 + n_paths]
    obuf, sem = refs[8 + n_paths:]
    i = pl.program_id(0)
    n = pl.num_programs(0)
    slot = i % 2

    @pl.when(i >= 2)
    def _():
        _wait_block(y_hbm, obuf, sem, slot, paths, to_hbm=True)

    sub_time = TIME_BLOCK // MLP_SLICES
    for sub in range(MLP_SLICES):
        x = x_ref[sub * sub_time * SEQ_TILE:(sub + 1) * sub_time * SEQ_TILE, :]
        h = _modulate(_layer_norm(x), 1.0 + mod_ref[4], mod_ref[3]).astype(BF16)
        f = jnp.zeros(x.shape, F32)
        for k in range(D_FF // D_MODEL):
            cols = slice(k * D_MODEL, (k + 1) * D_MODEL)
            a = jnp.square(jax.nn.relu(_dot(h, w1_ref[:, cols]) + b1_ref[:, cols]))
            f = f + _dot(a.astype(BF16), w2_ref[cols, :])
        y = DEEPNORM_ALPHA * x + _modulate(f + b2_ref[...], mod_ref[5])
        obuf[slot, sub * sub_time:(sub + 1) * sub_time] = (
            _layer_norm(y) * g2_ref[...] + be2_ref[...]).reshape(sub_time, SEQ_TILE, D_MODEL)

    _start_block(y_hbm, obuf, sem, i, slot, paths, to_hbm=True)

    @pl.when(i == n - 1)
    def _():
        @pl.when(n >= 2)
        def _():
            _wait_block(y_hbm, obuf, sem, 1 - slot, paths, to_hbm=True)
        _wait_block(y_hbm, obuf, sem, slot, paths, to_hbm=True)


def _mlp(x1, paths, n_blocks, mods, w1, b1, w2, b2, g2, be2):
    const2 = lambda i: (0, 0)
    once = pl.Buffered(1)
    vec = pl.BlockSpec((1, D_MODEL), const2)
    return pl.pallas_call(
        functools.partial(_mlp_kernel, paths=paths),
        grid=(n_blocks,),
        in_specs=[
            _row_spec(D_MODEL, n_blocks),
            pl.BlockSpec((N_MOD, SEQ_TILE, D_MODEL), lambda i: (0, _mod_block(i, paths), 0)),
            pl.BlockSpec((D_MODEL, D_FF), const2, pipeline_mode=once),
            pl.BlockSpec((1, D_FF), const2),
            pl.BlockSpec((D_FF, D_MODEL), const2, pipeline_mode=once),
            vec, vec, vec,
        ],
        out_specs=[pl.BlockSpec(memory_space=pl.ANY)] * len(paths),
        out_shape=[jax.ShapeDtypeStruct((p.n_seq, p.seq_len, D_MODEL), F32) for p in paths],
        scratch_shapes=[pltpu.VMEM((2, TIME_BLOCK, SEQ_TILE, D_MODEL), F32),
                        pltpu.SemaphoreType.DMA((2,))],
        compiler_params=pltpu.CompilerParams(dimension_semantics=("arbitrary",),
                                             vmem_limit_bytes=VMEM_LIMIT),
        name="mlp",
    )(x1, mods, w1, b1, w2, b2, g2, be2)


def _state_cols(x_f, x_b):
    return jnp.concatenate([x_f, x_b, x_f, x_b], axis=-1)


def kernel(x_prompt, x_sample, state_s5, c, c_ctx, w_ada, b_ada, w_in, b_in, s5_lam_re, s5_lam_im, s5_log_dt, s5_b_re, s5_b_im, s5_c_re, s5_c_im, s5_d, w_glu, b_glu, w_proj_a, w_pool, pool_scale, w_proj_b, w_out, b_out, ln1_g, ln1_b, w_mlp1, b_mlp1, w_mlp2, b_mlp2, ln2_g, ln2_b):
    assert w_in.shape[0] == 1, "single-layer backbone"
    n_ctx = x_prompt.shape[0]
    n_lat = x_sample.shape[0]
    g, p, hh = N_GROUPS, S5_STATE, S5_GROUP
    paths, n_blocks = _paths([(x_prompt.shape, x_prompt.shape[1]), (x_sample.shape, GRID_W)])

    n_vec = 1 + n_lat
    n_rows = -(-n_vec // 8) * 8
    cvec = jnp.concatenate([c_ctx[None, :], c, jnp.zeros((n_rows - n_vec, D_MODEL), F32)], axis=0)
    mods = _mods(cvec, w_ada[0], b_ada[0][None, :]).reshape(n_rows, N_MOD, D_MODEL)
    mods = mods.transpose(1, 0, 2)
    mods = jnp.concatenate([jnp.broadcast_to(mods[:, 0:1], (N_MOD, n_ctx, D_MODEL)),
                            mods[:, 1:n_vec]], axis=1)

    def per_group_row(x):
        return _state_cols(x[0], x[1])[:, None, :]
    lre = per_group_row(s5_lam_re[0])
    lim = per_group_row(s5_lam_im[0])
    ldt = per_group_row(jnp.broadcast_to(s5_log_dt[0][:, :, None], (2, g, p)))
    b_t = lambda x: _state_cols(x[0].transpose(0, 2, 1), x[1].transpose(0, 2, 1))
    c_t = lambda x: _state_cols(x[0], x[1])
    d_t = lambda x: jnp.tile(x.reshape(g, 1, hh), (1, 1, CHUNK))
    mt_mat, wb_mat, wct_mat, lam16 = _s5_prep(
        lre, lim, ldt, b_t(s5_b_re[0]), b_t(s5_b_im[0]), c_t(s5_c_re[0]), c_t(s5_c_im[0]),
        d_t(s5_d[0, 0]), d_t(s5_d[0, 1]))

    w_in_b = w_in[0].astype(BF16)
    b_in_r = b_in[0][None, :]
    xt, ua, ub, sg = _in_proj((x_prompt, x_sample), paths, n_blocks, mods, w_in_b, b_in_r)

    st = state_s5[:, 0].astype(F32)
    s0 = jnp.concatenate([st[:, 0], st[:, 1]], axis=-1).transpose(2, 1, 0, 3)
    v, fin = _s5_core(ua, paths, mt_mat, wb_mat, wct_mat, lam16, s0)
    new_state = fin.reshape(g, 2, n_ctx, 2, p).transpose(2, 3, 1, 0, 4)[:, None]

    x1 = _mix_out(xt, v, ub, sg, paths, n_blocks, mods,
                  w_glu[0].astype(BF16), b_glu[0][None, :], w_proj_a[0].astype(BF16),
                  w_pool[0].astype(BF16), pool_scale[0][None, :], w_proj_b[0].astype(BF16),
                  w_out[0].astype(BF16), b_out[0][None, :], ln1_g[0][None, :], ln1_b[0][None, :])
    y_p, y_s = _mlp(x1, paths, n_blocks, mods,
                    w_mlp1[0].astype(BF16), b_mlp1[0][None, :], w_mlp2[0].astype(BF16),
                    b_mlp2[0][None, :], ln2_g[0][None, :], ln2_b[0][None, :])
    return (y_p, y_s, new_state)
```

```python
import functools
from typing import NamedTuple

import jax
import jax.numpy as jnp
from jax import lax
from jax.experimental import pallas as pl
from jax.experimental.pallas import tpu as pltpu

F32 = jnp.float32
BF16 = jnp.bfloat16

D_MODEL = 1024
S5_GROUP = 16
N_GROUPS = D_MODEL // S5_GROUP
S5_STATE = 64
D_POOL = D_MODEL // 2
POOL_WINDOWS = (2, 4, 8, 16)
POOL_GROUP = D_POOL // len(POOL_WINDOWS)
D_IN = D_MODEL + D_POOL + 2 * D_MODEL
D_REST = D_IN - D_MODEL
D_FF = 4 * D_MODEL
N_MOD = 6
GRID_W = 64
DEEPNORM_ALPHA = 2.0 ** 0.25
LN_EPS = 1e-6

CHUNK = 16
CW = CHUNK * S5_GROUP
SW = 4 * S5_STATE
LANES = 128
N_STRIPS = D_MODEL // LANES
GROUPS_PER_STRIP = LANES // S5_GROUP
GROUP_UNROLL = 4
N_POW = 32
SEQ_TILE = 8
TIME_BLOCK = 64
TOK_BLOCK = SEQ_TILE * TIME_BLOCK
HALO = max(POOL_WINDOWS) // 2
IN_SLICES = 2
MLP_SLICES = 2
MIX_SLICES = 2
VMEM_LIMIT = 56 * 1024 * 1024


def _time_major(x3):
    return jnp.swapaxes(x3, 0, 1).reshape(x3.shape[1] * SEQ_TILE, x3.shape[-1])


def _seq_major(x2):
    return jnp.swapaxes(x2.reshape(x2.shape[0] // SEQ_TILE, SEQ_TILE, x2.shape[-1]), 0, 1)


def _modulate(a, mul, add=None):
    a3 = a.reshape(a.shape[0] // SEQ_TILE, SEQ_TILE, a.shape[-1]) * mul[None]
    if add is not None:
        a3 = a3 + add[None]
    return a3.reshape(a.shape)


def _layer_norm(x):
    mu = jnp.mean(x, axis=-1, keepdims=True)
    xc = x - mu
    var = jnp.mean(xc * xc, axis=-1, keepdims=True)
    return xc * lax.rsqrt(var + LN_EPS)


def _sigmoid(x):
    return 0.5 * jnp.tanh(0.5 * x) + 0.5


def _dot(a, b):
    return jnp.dot(a, b, preferred_element_type=F32)


def _dot_nt(a, b, precision=None):
    return lax.dot_general(a, b, (((1,), (1,)), ((), ())), precision=precision,
                           preferred_element_type=F32)


def _mods_kernel(c_ref, w_ref, b_ref, o_ref):
    s = jax.nn.silu(c_ref[...])
    o_ref[0] = jnp.dot(s, w_ref[...], precision=lax.Precision.HIGHEST,
                       preferred_element_type=F32) + b_ref[...]


def _mods(cvec, w_ada, b_ada):
    rows = cvec.shape[0]
    n_out = w_ada.shape[1]
    return pl.pallas_call(
        _mods_kernel,
        grid=(n_out // D_MODEL,),
        in_specs=[
            pl.BlockSpec((rows, D_MODEL), lambda j: (0, 0)),
            pl.BlockSpec((D_MODEL, D_MODEL), lambda j: (0, j)),
            pl.BlockSpec((1, D_MODEL), lambda j: (0, j)),
        ],
        out_specs=pl.BlockSpec((1, rows, D_MODEL), lambda j: (j, 0, 0)),
        out_shape=jax.ShapeDtypeStruct((n_out // D_MODEL, rows, D_MODEL), F32),
        name="mods",
    )(cvec, w_ada, b_ada)


def _s5_prep_kernel(rows_ref, mats_ref, mt_ref, wb_ref, wct_ref, lam_ref,
                    pwr_ref, pwi_ref, ge_ref):
    pg = GROUPS_PER_STRIP
    lre = rows_ref[:, 0:1, :]
    lim = rows_ref[:, 1:2, :]
    dt = jnp.exp(rows_ref[:, 2:3, :])
    a = lre * dt
    b = lim * dt
    col = lax.broadcasted_iota(jnp.int32, (1, 1, SW), 2)
    is_im = col >= 2 * S5_STATE
    is_b = (col & (2 * S5_STATE - 1)) >= S5_STATE

    mag = jnp.exp(a)
    sq_r = mag * jnp.cos(b)
    sq_i = mag * jnp.sin(b)
    pwr_ref[:, 0:1, :] = jnp.ones((pg, 1, SW), F32)
    pwi_ref[:, 0:1, :] = jnp.zeros((pg, 1, SW), F32)
    m = 1
    while m < N_POW:
        lo_r = pwr_ref[:, 0:m, :]
        lo_i = pwi_ref[:, 0:m, :]
        pwr_ref[:, m:2 * m, :] = lo_r * sq_r - lo_i * sq_i
        pwi_ref[:, m:2 * m, :] = lo_r * sq_i + lo_i * sq_r
        sq_r, sq_i = sq_r * sq_r - sq_i * sq_i, 2.0 * sq_r * sq_i
        m *= 2

    def power(kf, kb):
        def row(ref, kk):
            return jnp.zeros((pg, 1, SW), F32) if kk is None else ref[:, kk:kk + 1, :]
        if kf == kb:
            return row(pwr_ref, kf), row(pwi_ref, kf)
        return (jnp.where(is_b, row(pwr_ref, kb), row(pwr_ref, kf)),
                jnp.where(is_b, row(pwi_ref, kb), row(pwi_ref, kf)))

    lbr = pwr_ref[:, 1:2, :]
    lbi = pwi_ref[:, 1:2, :]
    den = lre * lre + lim * lim
    nr = lbr - 1.0
    cr = (nr * lre + lbi * lim) / den
    ci = (lbi * lre - nr * lim) / den
    bre = mats_ref[:, 0]
    bim = mats_ref[:, 1]
    bbr = cr * bre - ci * bim
    bbi = cr * bim + ci * bre
    bx = jnp.where(is_im, bbi, bbr)
    by = jnp.where(is_im, bbr, -bbi)
    cre = mats_ref[:, 2]
    cim = mats_ref[:, 3]
    cx = jnp.where(is_im, -cim, cre)
    cy = jnp.where(is_im, -cre, -cim)

    for t in range(CHUNK):
        rows = slice(t * S5_GROUP, (t + 1) * S5_GROUP)
        pr, pi = power(CHUNK - 1 - t, t)
        wb_ref[:, rows, :] = (pr * bx + pi * by).astype(BF16)
        pr, pi = power(t + 1, CHUNK - t)
        wct_ref[:, rows, :] = (pr * cx + pi * cy).astype(BF16)

    for j in range(2 * CHUNK):
        rows = slice(j * S5_GROUP, (j + 1) * S5_GROUP)
        if j == 2 * CHUNK - 1:
            ge_ref[:, rows, :] = jnp.zeros((pg, S5_GROUP, SW), F32)
            continue
        pr, pi = power(j - (CHUNK - 1) if j >= CHUNK - 1 else None,
                       (CHUNK - 1) - j if j <= CHUNK - 1 else None)
        ge_ref[:, rows, :] = pr * cx + pi * cy

    dsum = rows_ref[:, 3:4, :] + rows_ref[:, 4:5, :]
    r16 = lax.broadcasted_iota(jnp.int32, (S5_GROUP, CW), 0)
    c16 = lax.broadcasted_iota(jnp.int32, (S5_GROUP, CW), 1)
    for gi in range(pg):
        e = _dot_nt(bx[gi], ge_ref[gi], precision=lax.Precision.HIGHEST)
        blocks = []
        for tau in range(CHUNK):
            start = (CHUNK - 1 - tau) * S5_GROUP
            blocks.append(e[:, start:start + CW]
                          + jnp.where(c16 == r16 + tau * S5_GROUP, dsum[gi], 0.0))
        mt_ref[gi] = jnp.concatenate(blocks, axis=0).T.astype(BF16)

    lam_ref[...] = jnp.concatenate(
        [pwr_ref[:, CHUNK:CHUNK + 1, :], pwi_ref[:, CHUNK:CHUNK + 1, :],
         jnp.zeros((pg, 6, SW), F32)], axis=1)


def _s5_prep(rows, mats):
    g = N_GROUPS
    pg = GROUPS_PER_STRIP
    mat_out = pl.BlockSpec((pg, CW, SW), lambda i: (i, 0, 0))
    return pl.pallas_call(
        _s5_prep_kernel,
        grid=(g // pg,),
        in_specs=[pl.BlockSpec((pg,) + rows.shape[1:], lambda i: (i, 0, 0)),
                  pl.BlockSpec((pg,) + mats.shape[1:], lambda i: (i, 0, 0, 0))],
        out_specs=[mat_out, mat_out, mat_out, pl.BlockSpec((pg, 8, SW), lambda i: (i, 0, 0))],
        out_shape=[jax.ShapeDtypeStruct((g, CW, CW), BF16),
                   jax.ShapeDtypeStruct((g, CW, SW), BF16),
                   jax.ShapeDtypeStruct((g, CW, SW), BF16),
                   jax.ShapeDtypeStruct((g, 8, SW), F32)],
        scratch_shapes=[pltpu.VMEM((pg, N_POW, SW), F32), pltpu.VMEM((pg, N_POW, SW), F32),
                        pltpu.VMEM((pg, 2 * CW, SW), F32)],
        name="s5_prep",
    )(rows, mats)


class _Path(NamedTuple):
    n_seq: int
    seq_len: int
    pool_n: int
    first_block: int

    @property
    def time_blocks(self):
        return self.seq_len // TIME_BLOCK

    @property
    def n_blocks(self):
        return (self.n_seq // SEQ_TILE) * self.time_blocks


def _paths(shapes_and_pool):
    paths, first = [], 0
    for (n_seq, seq_len, _), pool_n in shapes_and_pool:
        assert n_seq % SEQ_TILE == 0 and seq_len % TIME_BLOCK == 0
        assert pool_n % TIME_BLOCK == 0 and seq_len % pool_n == 0 and HALO <= TIME_BLOCK
        run_blocks = pool_n // TIME_BLOCK
        assert run_blocks & (run_blocks - 1) == 0 and first % run_blocks == 0
        paths.append(_Path(n_seq, seq_len, pool_n, first))
        first += paths[-1].n_blocks
    return tuple(paths), first


def _block_coords(i, paths):
    coords = []
    for k, p in enumerate(paths):
        end = p.first_block + p.n_blocks
        inside = (i >= p.first_block) & (i < end)
        j = jnp.clip(i - p.first_block, 0, p.n_blocks - 1)
        coords.append((inside, (j // p.time_blocks) * SEQ_TILE, (j % p.time_blocks) * TIME_BLOCK))
    return coords


def _mod_block(i, paths):
    blk, first_mod = 0, 0
    for p in paths:
        j = jnp.clip(i - p.first_block, 0, p.n_blocks - 1)
        blk = jnp.where(i >= p.first_block, first_mod + j // p.time_blocks, blk)
        first_mod += p.n_seq // SEQ_TILE
    return blk


def _row_spec(width, n_blocks, shift=0):
    return pl.BlockSpec((TOK_BLOCK, width),
                        lambda i: (jnp.clip(i + shift, 0, n_blocks - 1), 0))


def _block_copies(hbm_refs, buf, sem, i, slot, paths, to_hbm):
    for (inside, seq0, t0), hbm in zip(_block_coords(i, paths), hbm_refs):
        copies = []
        for s in range(SEQ_TILE):
            rows = hbm.at[seq0 + s, pl.ds(t0, TIME_BLOCK), :]
            tile_rows = buf.at[slot, :, s, :]
            src, dst = (tile_rows, rows) if to_hbm else (rows, tile_rows)
            copies.append(pltpu.make_async_copy(src, dst, sem.at[slot]))
        yield inside, copies


def _start_block(hbm_refs, buf, sem, i, slot, paths, to_hbm):
    for inside, copies in _block_copies(hbm_refs, buf, sem, i, slot, paths, to_hbm):
        @pl.when(inside)
        def _():
            for c in copies:
                c.start()


def _wait_block(hbm_refs, buf, sem, slot, paths, to_hbm):
    _, copies = next(_block_copies(hbm_refs, buf, sem, 0, slot, paths, to_hbm))
    for c in copies:
        c.wait()


def _in_proj_kernel(*refs, paths):
    n_paths = len(paths)
    x_hbm = refs[:n_paths]
    mod_ref, w_ref, b_ref, xt_ref, ua_ref, ub_ref, sg_ref, xbuf, sem = refs[n_paths:]
    i = pl.program_id(0)
    n = pl.num_programs(0)
    slot = i % 2

    @pl.when(i == 0)
    def _():
        _start_block(x_hbm, xbuf, sem, i, slot, paths, to_hbm=False)

    @pl.when(i + 1 < n)
    def _():
        _start_block(x_hbm, xbuf, sem, i + 1, 1 - slot, paths, to_hbm=False)

    _wait_block(x_hbm, xbuf, sem, slot, paths, to_hbm=False)
    sub_time = TIME_BLOCK // IN_SLICES
    for sub in range(IN_SLICES):
        times = slice(sub * sub_time, (sub + 1) * sub_time)
        rows = slice(sub * sub_time * SEQ_TILE, (sub + 1) * sub_time * SEQ_TILE)
        x = xbuf[slot, times].reshape(sub_time * SEQ_TILE, D_MODEL)
        xt_ref[rows, :] = x
        h = _modulate(_layer_norm(x), 1.0 + mod_ref[1], mod_ref[0])
        z = _dot(h.astype(BF16), w_ref[...]) + b_ref[...]
        ua_ref[times] = z[:, :D_MODEL].reshape(sub_time, SEQ_TILE, D_MODEL)
        ub_ref[rows, :] = z[:, D_MODEL:D_MODEL + D_POOL].astype(BF16)
        sg_ref[rows, :] = _sigmoid(z[:, D_MODEL + D_POOL:]).astype(BF16)


def _in_proj(xs, paths, n_blocks, mods, w_in, b_in):
    n_tok = n_blocks * TOK_BLOCK
    const = lambda i: (0, 0)
    once = pl.Buffered(1)
    return pl.pallas_call(
        functools.partial(_in_proj_kernel, paths=paths),
        grid=(n_blocks,),
        in_specs=[pl.BlockSpec(memory_space=pl.ANY)] * len(xs) + [
            pl.BlockSpec((N_MOD, SEQ_TILE, D_MODEL), lambda i: (0, _mod_block(i, paths), 0)),
            pl.BlockSpec((D_MODEL, D_IN), const, pipeline_mode=once),
            pl.BlockSpec((1, D_IN), const),
        ],
        out_specs=[_row_spec(D_MODEL, n_blocks),
                   pl.BlockSpec((TIME_BLOCK, SEQ_TILE, D_MODEL), lambda i: (i, 0, 0)),
                   _row_spec(D_POOL, n_blocks), _row_spec(2 * D_MODEL, n_blocks)],
        out_shape=[jax.ShapeDtypeStruct((n_tok, D_MODEL), F32),
                   jax.ShapeDtypeStruct((n_blocks * TIME_BLOCK, SEQ_TILE, D_MODEL), F32),
                   jax.ShapeDtypeStruct((n_tok, D_POOL), BF16),
                   jax.ShapeDtypeStruct((n_tok, 2 * D_MODEL), BF16)],
        scratch_shapes=[pltpu.VMEM((2, TIME_BLOCK, SEQ_TILE, D_MODEL), F32),
                        pltpu.SemaphoreType.DMA((2,))],
        compiler_params=pltpu.CompilerParams(dimension_semantics=("arbitrary",),
                                             vmem_limit_bytes=VMEM_LIMIT),
        name="in_proj",
    )(*xs, mods, w_in, b_in)


def _s5_core_kernel(ua_ref, mt_ref, wb_ref, wct_ref, lam_ref, s0_ref, v_ref, fin_ref,
                    xt_ref, yt_ref, sloc_ref, sprf_ref, sprb_ref, fsc_ref, *, paths):
    for k, p in enumerate(paths):
        @pl.when(pl.program_id(1) == k)
        def _():
            _s5_strip(ua_ref, mt_ref, wb_ref, wct_ref, lam_ref, s0_ref, v_ref, fin_ref,
                      xt_ref, yt_ref, sloc_ref, sprf_ref, sprb_ref, fsc_ref,
                      n_seq=p.n_seq, seq_len=p.seq_len, has_init=k > 0)


def _s5_strip(ua_ref, mt_ref, wb_ref, wct_ref, lam_ref, s0_ref, v_ref, fin_ref,
              xt_ref, yt_ref, sloc_ref, sprf_ref, sprb_ref, fsc_ref, *, n_seq, seq_len, has_init):
    n_chunk = seq_len // CHUNK
    n_sb = n_seq // SEQ_TILE
    half = 2 * S5_STATE

    for tau in range(CHUNK):
        x_tau = jnp.concatenate([ua_ref[sb * seq_len + c * CHUNK + tau]
                                 for c in range(n_chunk) for sb in range(n_sb)], axis=0)
        xt = x_tau.T
        for gi in range(GROUPS_PER_STRIP):
            xt_ref[gi, tau * S5_GROUP:(tau + 1) * S5_GROUP, :] = xt[gi * S5_GROUP:(gi + 1) * S5_GROUP, :]

    is_f = lax.broadcasted_iota(jnp.int32, (1, half), 1) < S5_STATE
    col = lax.broadcasted_iota(jnp.int32, (1, SW), 1)
    col_is_f = (col & (half - 1)) < S5_STATE

    def one_group(gi, slot):
        xg = xt_ref[gi]
        u = xg.T.astype(BF16)
        sloc_ref[slot] = _dot(u, wb_ref[gi])
        ar = lam_ref[gi, 0:1, 0:half]
        ai = lam_ref[gi, 1:2, 0:half]
        if has_init:
            s_re = s0_ref[gi, 0]
            s_im = s0_ref[gi, 1]
        else:
            s_re = jnp.zeros((n_seq, half), F32)
            s_im = jnp.zeros((n_seq, half), F32)
        for c in range(n_chunk):
            rf = pl.ds(c * n_seq, n_seq)
            rb = pl.ds((n_chunk - 1 - c) * n_seq, n_seq)
            sprf_ref[slot, rf, 0:half] = s_re
            sprf_ref[slot, rf, half:SW] = s_im
            sprb_ref[slot, rb, 0:half] = s_re
            sprb_ref[slot, rb, half:SW] = s_im
            l_re = jnp.where(is_f, sloc_ref[slot, rf, 0:half], sloc_ref[slot, rb, 0:half])
            l_im = jnp.where(is_f, sloc_ref[slot, rf, half:SW], sloc_ref[slot, rb, half:SW])
            s_re, s_im = (ar * s_re - ai * s_im + l_re,
                          ar * s_im + ai * s_re + l_im)
        if not has_init:
            fsc_ref[gi, 0] = s_re
            fsc_ref[gi, 1] = s_im
        sprev = jnp.where(col_is_f, sprf_ref[slot], sprb_ref[slot]).astype(BF16)
        yt = _dot(mt_ref[gi], xg.astype(BF16)) + _dot_nt(wct_ref[gi], sprev)
        yt_ref[gi] = jax.nn.gelu(yt)

    def group_body(i, carry):
        for slot in range(GROUP_UNROLL):
            one_group(i * GROUP_UNROLL + slot, slot)
        return carry

    lax.fori_loop(0, GROUPS_PER_STRIP // GROUP_UNROLL, group_body, 0)

    if not has_init:
        for part in range(2):
            fs = jnp.swapaxes(fsc_ref[:, part], 0, 1)
            fin_ref[:, part] = fs[:, :, 0:S5_STATE]
            fin_ref[:, 2 + part] = fs[:, :, S5_STATE:half]

    for t in range(CHUNK):
        vt = yt_ref[:, t * S5_GROUP:(t + 1) * S5_GROUP, :].reshape(LANES, n_seq * n_chunk)
        v_t = vt.T
        for c in range(n_chunk):
            for sb in range(n_sb):
                r0 = (c * n_sb + sb) * SEQ_TILE
                v_ref[sb * seq_len + c * CHUNK + t] = v_t[r0:r0 + SEQ_TILE, :]


def _s5_core(ua, paths, m, wb, wct, lam, s0):
    ctx, lat = paths
    tiles = ctx.n_blocks * TIME_BLOCK
    rows = tiles * SEQ_TILE // CHUNK
    assert lat.n_blocks * TIME_BLOCK == tiles and s0.shape[2] == lat.n_seq
    gps = GROUPS_PER_STRIP
    strip = pl.BlockSpec((tiles, SEQ_TILE, LANES), lambda j, k: (k, 0, j))
    mat = pl.BlockSpec((gps, CW, SW), lambda j, k: (j, 0, 0))
    return pl.pallas_call(
        functools.partial(_s5_core_kernel, paths=paths),
        grid=(N_STRIPS, len(paths)),
        in_specs=[strip, mat, mat, mat, pl.BlockSpec((gps, 8, SW), lambda j, k: (j, 0, 0)),
                  pl.BlockSpec((gps, 2, lat.n_seq, 2 * S5_STATE), lambda j, k: (j, 0, 0, 0))],
        out_specs=[strip,
                   pl.BlockSpec((ctx.n_seq, 4, gps, S5_STATE), lambda j, k: (0, 0, j, 0))],
        out_shape=[jax.ShapeDtypeStruct(ua.shape, F32),
                   jax.ShapeDtypeStruct((ctx.n_seq, 4, N_GROUPS, S5_STATE), F32)],
        scratch_shapes=[pltpu.VMEM((gps, CW, rows), F32), pltpu.VMEM((gps, CW, rows), F32),
                        pltpu.VMEM((GROUP_UNROLL, rows, SW), F32),
                        pltpu.VMEM((GROUP_UNROLL, rows, SW), F32),
                        pltpu.VMEM((GROUP_UNROLL, rows, SW), F32),
                        pltpu.VMEM((gps, 2, ctx.n_seq, 2 * S5_STATE), F32)],
        compiler_params=pltpu.CompilerParams(dimension_semantics=("arbitrary", "arbitrary"),
                                             vmem_limit_bytes=VMEM_LIMIT),
        name="s5_core",
    )(ua, m, wb, wct, lam, s0)


def _mix_out_kernel(xt_ref, v_ref, ubp_ref, ub_ref, ubn_ref, sg_ref, mod_ref, wglu_ref, bglu_ref,
                    wpa_ref, wpool_ref, pscale_ref, wpb_ref, wout_ref, bout_ref, g1_ref, b1_ref,
                    o_ref, *, paths):
    i = pl.program_id(0)
    run_mask = 0
    for p in paths:
        run_mask = jnp.where(i >= p.first_block, p.pool_n // TIME_BLOCK - 1, run_mask)
    pos = i & run_mask
    first = pos == 0
    last = pos == run_mask
    halo_rows = HALO * SEQ_TILE
    u_prev = jnp.where(first, 0.0, ubp_ref[TOK_BLOCK - halo_rows:, :].astype(F32))
    u_next = jnp.where(last, 0.0, ubn_ref[:halo_rows, :].astype(F32))
    ext = jnp.concatenate([u_prev, ub_ref[...].astype(F32), u_next], axis=0)

    sub_rows = TOK_BLOCK // MIX_SLICES
    slices = [slice(sub * sub_rows, (sub + 1) * sub_rows) for sub in range(MIX_SLICES)]

    def s5_branch(sub):
        tiles = sub_rows // SEQ_TILE
        v = v_ref[sub * tiles:(sub + 1) * tiles].reshape(sub_rows, D_MODEL)
        glu = v * _sigmoid(_dot(v.astype(BF16), wglu_ref[...]) + bglu_ref[...])
        return _dot(glu.astype(BF16), wpa_ref[...])

    def pool_branch(sub):
        base = halo_rows + sub * sub_rows
        t_idx = (lax.broadcasted_iota(jnp.int32, (sub_rows, POOL_GROUP), 0)
                 + sub * sub_rows) // SEQ_TILE
        pooled = []
        for gi, w in enumerate(POOL_WINDOWS):
            cols = slice(gi * POOL_GROUP, (gi + 1) * POOL_GROUP)
            acc = jnp.zeros((sub_rows, POOL_GROUP), F32)
            for k in range(-(w // 2), w - w // 2):
                r0 = base + k * SEQ_TILE
                acc = acc + ext[r0:r0 + sub_rows, cols]
            lo = jnp.where(first, jnp.maximum(t_idx - w // 2, 0), t_idx - w // 2)
            hi = jnp.where(last, jnp.minimum(t_idx - w // 2 + w, TIME_BLOCK), t_idx - w // 2 + w)
            p = acc / (hi - lo).astype(F32) - ext[base:base + sub_rows, cols]
            pooled.append(_dot(p.astype(BF16), wpool_ref[gi]))
        pm = jnp.concatenate(pooled, axis=1) * pscale_ref[...]
        return _dot(pm.astype(BF16), wpb_ref[...])

    ya = [s5_branch(sub) for sub in range(MIX_SLICES)]
    yb = [pool_branch(sub) for sub in range(MIX_SLICES)]
    tm = []
    for sub, rows in enumerate(slices):
        merged = (sg_ref[rows, 0:D_MODEL].astype(F32) * ya[sub]
                  + sg_ref[rows, D_MODEL:2 * D_MODEL].astype(F32) * yb[sub])
        tm.append(_dot(merged.astype(BF16), wout_ref[...]) + bout_ref[...])
    for sub, rows in enumerate(slices):
        y = DEEPNORM_ALPHA * xt_ref[rows, :] + _modulate(tm[sub], mod_ref[2])
        o_ref[rows, :] = _layer_norm(y) * g1_ref[...] + b1_ref[...]


def _mix_out(xt, v, ub, sg, paths, n_blocks, mods, wglu, bglu, wpa, wpool, pscale, wpb, wout,
             bout, g1, b1):
    const2 = lambda i: (0, 0)
    once = pl.Buffered(1)
    vec = pl.BlockSpec((1, D_MODEL), const2)
    sq = pl.BlockSpec((D_MODEL, D_MODEL), const2, pipeline_mode=once)
    rows = functools.partial(_row_spec, n_blocks=n_blocks)
    return pl.pallas_call(
        functools.partial(_mix_out_kernel, paths=paths),
        grid=(n_blocks,),
        in_specs=[
            rows(D_MODEL),
            pl.BlockSpec((TIME_BLOCK, SEQ_TILE, D_MODEL), lambda i: (i, 0, 0)),
            rows(D_POOL, shift=-1), rows(D_POOL), rows(D_POOL, shift=1),
            rows(2 * D_MODEL),
            pl.BlockSpec((N_MOD, SEQ_TILE, D_MODEL), lambda i: (0, _mod_block(i, paths), 0)),
            sq, vec, sq,
            pl.BlockSpec((len(POOL_WINDOWS), POOL_GROUP, POOL_GROUP), lambda i: (0, 0, 0)),
            pl.BlockSpec((1, D_POOL), const2),
            pl.BlockSpec((D_POOL, D_MODEL), const2, pipeline_mode=once),
            sq, vec, vec, vec,
        ],
        out_specs=rows(D_MODEL),
        out_shape=jax.ShapeDtypeStruct((n_blocks * TOK_BLOCK, D_MODEL), F32),
        compiler_params=pltpu.CompilerParams(vmem_limit_bytes=VMEM_LIMIT),
        name="mix_out",
    )(xt, v, ub, ub, ub, sg, mods, wglu, bglu, wpa, wpool, pscale, wpb, wout, bout, g1, b1)


def _mlp_kernel(*refs, paths):
    n_paths = len(paths)
    x_ref, mod_ref, w1_ref, b1_ref, w2_ref, b2_ref, g2_ref, be2_ref = refs[:8]
    y_hbm = refs[8:8 + n_paths]
    obuf, sem = refs[8 + n_paths:]
    i = pl.program_id(0)
    n = pl.num_programs(0)
    slot = i % 2

    @pl.when(i >= 2)
    def _():
        _wait_block(y_hbm, obuf, sem, slot, paths, to_hbm=True)

    sub_time = TIME_BLOCK // MLP_SLICES
    for sub in range(MLP_SLICES):
        x = x_ref[sub * sub_time * SEQ_TILE:(sub + 1) * sub_time * SEQ_TILE, :]
        h = _modulate(_layer_norm(x), 1.0 + mod_ref[4], mod_ref[3]).astype(BF16)
        f = jnp.zeros(x.shape, F32)
        for k in range(D_FF // D_MODEL):
            cols = slice(k * D_MODEL, (k + 1) * D_MODEL)
            a = jnp.square(jax.nn.relu(_dot(h, w1_ref[:, cols]) + b1_ref[:, cols]))
            f = f + _dot(a.astype(BF16), w2_ref[cols, :])
        y = DEEPNORM_ALPHA * x + _modulate(f + b2_ref[...], mod_ref[5])
        obuf[slot, sub * sub_time:(sub + 1) * sub_time] = (
            _layer_norm(y) * g2_ref[...] + be2_ref[...]).reshape(sub_time, SEQ_TILE, D_MODEL)

    _start_block(y_hbm, obuf, sem, i, slot, paths, to_hbm=True)

    @pl.when(i == n - 1)
    def _():
        @pl.when(n >= 2)
        def _():
            _wait_block(y_hbm, obuf, sem, 1 - slot, paths, to_hbm=True)
        _wait_block(y_hbm, obuf, sem, slot, paths, to_hbm=True)


def _mlp(x1, paths, n_blocks, mods, w1, b1, w2, b2, g2, be2):
    const2 = lambda i: (0, 0)
    once = pl.Buffered(1)
    vec = pl.BlockSpec((1, D_MODEL), const2)
    return pl.pallas_call(
        functools.partial(_mlp_kernel, paths=paths),
        grid=(n_blocks,),
        in_specs=[
            _row_spec(D_MODEL, n_blocks),
            pl.BlockSpec((N_MOD, SEQ_TILE, D_MODEL), lambda i: (0, _mod_block(i, paths), 0)),
            pl.BlockSpec((D_MODEL, D_FF), const2, pipeline_mode=once),
            pl.BlockSpec((1, D_FF), const2),
            pl.BlockSpec((D_FF, D_MODEL), const2, pipeline_mode=once),
            vec, vec, vec,
        ],
        out_specs=[pl.BlockSpec(memory_space=pl.ANY)] * len(paths),
        out_shape=[jax.ShapeDtypeStruct((p.n_seq, p.seq_len, D_MODEL), F32) for p in paths],
        scratch_shapes=[pltpu.VMEM((2, TIME_BLOCK, SEQ_TILE, D_MODEL), F32),
                        pltpu.SemaphoreType.DMA((2,))],
        compiler_params=pltpu.CompilerParams(dimension_semantics=("arbitrary",),
                                             vmem_limit_bytes=VMEM_LIMIT),
        name="mlp",
    )(x1, mods, w1, b1, w2, b2, g2, be2)


def _state_cols(x_f, x_b):
    return jnp.concatenate([x_f, x_b, x_f, x_b], axis=-1)


def kernel(x_prompt, x_sample, state_s5, c, c_ctx, w_ada, b_ada, w_in, b_in, s5_lam_re, s5_lam_im, s5_log_dt, s5_b_re, s5_b_im, s5_c_re, s5_c_im, s5_d, w_glu, b_glu, w_proj_a, w_pool, pool_scale, w_proj_b, w_out, b_out, ln1_g, ln1_b, w_mlp1, b_mlp1, w_mlp2, b_mlp2, ln2_g, ln2_b):
    assert w_in.shape[0] == 1, "single-layer backbone"
    n_ctx = x_prompt.shape[0]
    n_lat = x_sample.shape[0]
    g, p, hh = N_GROUPS, S5_STATE, S5_GROUP
    paths, n_blocks = _paths([(x_prompt.shape, x_prompt.shape[1]), (x_sample.shape, GRID_W)])

    n_vec = 1 + n_lat
    n_rows = -(-n_vec // 8) * 8
    cvec = jnp.concatenate([c_ctx[None, :], c, jnp.zeros((n_rows - n_vec, D_MODEL), F32)], axis=0)
    mods = _mods(cvec, w_ada[0], b_ada[0][None, :])
    mods = jnp.concatenate([jnp.broadcast_to(mods[:, 0:1], (N_MOD, n_ctx, D_MODEL)),
                            mods[:, 1:n_vec]], axis=1)

    dirs = lambda x: _state_cols(x[0], x[1])
    rows = jnp.stack(
        [dirs(s5_lam_re[0]), dirs(s5_lam_im[0]),
         dirs(jnp.broadcast_to(s5_log_dt[0][:, :, None], (2, g, p))),
         jnp.tile(s5_d[0, 0].reshape(g, hh), (1, CHUNK)),
         jnp.tile(s5_d[0, 1].reshape(g, hh), (1, CHUNK))], axis=1)
    mats = jnp.stack(
        [dirs(s5_b_re[0].transpose(0, 1, 3, 2)), dirs(s5_b_im[0].transpose(0, 1, 3, 2)),
         dirs(s5_c_re[0]), dirs(s5_c_im[0])], axis=1)
    mt_mat, wb_mat, wct_mat, lam16 = _s5_prep(rows, mats)

    w_in_b = w_in[0].astype(BF16)
    b_in_r = b_in[0][None, :]
    xt, ua, ub, sg = _in_proj((x_prompt, x_sample), paths, n_blocks, mods, w_in_b, b_in_r)

    st = state_s5[:, 0].astype(F32)
    s0 = jnp.concatenate([st[:, 0], st[:, 1]], axis=-1).transpose(2, 1, 0, 3)
    v, fin = _s5_core(ua, paths, mt_mat, wb_mat, wct_mat, lam16, s0)
    new_state = fin.reshape(n_ctx, 1, 2, 2, g, p)

    x1 = _mix_out(xt, v, ub, sg, paths, n_blocks, mods,
                  w_glu[0].astype(BF16), b_glu[0][None, :], w_proj_a[0].astype(BF16),
                  w_pool[0].astype(BF16), pool_scale[0][None, :], w_proj_b[0].astype(BF16),
                  w_out[0].astype(BF16), b_out[0][None, :], ln1_g[0][None, :], ln1_b[0][None, :])
    y_p, y_s = _mlp(x1, paths, n_blocks, mods,
                    w_mlp1[0].astype(BF16), b_mlp1[0][None, :], w_mlp2[0].astype(BF16),
                    b_mlp2[0][None, :], ln2_g[0][None, :], ln2_b[0][None, :])
    return (y_p, y_s, new_state)
```

```python
import functools
from typing import NamedTuple

import jax
import jax.numpy as jnp
from jax import lax
from jax.experimental import pallas as pl
from jax.experimental.pallas import tpu as pltpu

F32 = jnp.float32
BF16 = jnp.bfloat16

D_MODEL = 1024
S5_GROUP = 16
N_GROUPS = D_MODEL // S5_GROUP
S5_STATE = 64
D_POOL = D_MODEL // 2
POOL_WINDOWS = (2, 4, 8, 16)
POOL_GROUP = D_POOL // len(POOL_WINDOWS)
D_IN = D_MODEL + D_POOL + 2 * D_MODEL
D_REST = D_IN - D_MODEL
D_FF = 4 * D_MODEL
N_MOD = 6
GRID_W = 64
DEEPNORM_ALPHA = 2.0 ** 0.25
LN_EPS = 1e-6

CHUNK = 16
CW = CHUNK * S5_GROUP
SW = 4 * S5_STATE
LANES = 128
N_STRIPS = D_MODEL // LANES
GROUPS_PER_STRIP = LANES // S5_GROUP
GROUP_UNROLL = 4
N_POW = 32
SEQ_TILE = 8
TIME_BLOCK = 64
TOK_BLOCK = SEQ_TILE * TIME_BLOCK
HALO = max(POOL_WINDOWS) // 2
IN_SLICES = 2
MLP_SLICES = 2
MIX_SLICES = 2
VMEM_LIMIT = 56 * 1024 * 1024


def _time_major(x3):
    return jnp.swapaxes(x3, 0, 1).reshape(x3.shape[1] * SEQ_TILE, x3.shape[-1])


def _seq_major(x2):
    return jnp.swapaxes(x2.reshape(x2.shape[0] // SEQ_TILE, SEQ_TILE, x2.shape[-1]), 0, 1)


def _modulate(a, mul, add=None):
    a3 = a.reshape(a.shape[0] // SEQ_TILE, SEQ_TILE, a.shape[-1]) * mul[None]
    if add is not None:
        a3 = a3 + add[None]
    return a3.reshape(a.shape)


def _layer_norm(x):
    mu = jnp.mean(x, axis=-1, keepdims=True)
    xc = x - mu
    var = jnp.mean(xc * xc, axis=-1, keepdims=True)
    return xc * lax.rsqrt(var + LN_EPS)


def _sigmoid(x):
    return 0.5 * jnp.tanh(0.5 * x) + 0.5


def _dot(a, b):
    return jnp.dot(a, b, preferred_element_type=F32)


def _dot_nt(a, b, precision=None):
    return lax.dot_general(a, b, (((1,), (1,)), ((), ())), precision=precision,
                           preferred_element_type=F32)


def _mods_kernel(c_ref, w_ref, b_ref, o_ref):
    s = jax.nn.silu(c_ref[...])
    o_ref[0] = jnp.dot(s, w_ref[...], precision=lax.Precision.HIGHEST,
                       preferred_element_type=F32) + b_ref[...]


def _mods(cvec, w_ada, b_ada):
    rows = cvec.shape[0]
    n_out = w_ada.shape[1]
    return pl.pallas_call(
        _mods_kernel,
        grid=(n_out // D_MODEL,),
        in_specs=[
            pl.BlockSpec((rows, D_MODEL), lambda j: (0, 0)),
            pl.BlockSpec((D_MODEL, D_MODEL), lambda j: (0, j)),
            pl.BlockSpec((1, D_MODEL), lambda j: (0, j)),
        ],
        out_specs=pl.BlockSpec((1, rows, D_MODEL), lambda j: (j, 0, 0)),
        out_shape=jax.ShapeDtypeStruct((n_out // D_MODEL, rows, D_MODEL), F32),
        name="mods",
    )(cvec, w_ada, b_ada)


def _s5_prep_kernel(rows_ref, mats_ref, mt_ref, wb_ref, wct_ref, lam_ref,
                    pwr_ref, pwi_ref, ge_ref):
    pg = GROUPS_PER_STRIP
    lre = rows_ref[:, 0:1, :]
    lim = rows_ref[:, 1:2, :]
    dt = jnp.exp(rows_ref[:, 2:3, :])
    a = lre * dt
    b = lim * dt
    col = lax.broadcasted_iota(jnp.int32, (1, 1, SW), 2)
    is_im = col >= 2 * S5_STATE
    is_b = (col & (2 * S5_STATE - 1)) >= S5_STATE

    mag = jnp.exp(a)
    sq_r = mag * jnp.cos(b)
    sq_i = mag * jnp.sin(b)
    pwr_ref[:, 0:1, :] = jnp.ones((pg, 1, SW), F32)
    pwi_ref[:, 0:1, :] = jnp.zeros((pg, 1, SW), F32)
    m = 1
    while m < N_POW:
        lo_r = pwr_ref[:, 0:m, :]
        lo_i = pwi_ref[:, 0:m, :]
        pwr_ref[:, m:2 * m, :] = lo_r * sq_r - lo_i * sq_i
        pwi_ref[:, m:2 * m, :] = lo_r * sq_i + lo_i * sq_r
        sq_r, sq_i = sq_r * sq_r - sq_i * sq_i, 2.0 * sq_r * sq_i
        m *= 2

    def power(kf, kb):
        def row(ref, kk):
            return jnp.zeros((pg, 1, SW), F32) if kk is None else ref[:, kk:kk + 1, :]
        if kf == kb:
            return row(pwr_ref, kf), row(pwi_ref, kf)
        return (jnp.where(is_b, row(pwr_ref, kb), row(pwr_ref, kf)),
                jnp.where(is_b, row(pwi_ref, kb), row(pwi_ref, kf)))

    lbr = pwr_ref[:, 1:2, :]
    lbi = pwi_ref[:, 1:2, :]
    den = lre * lre + lim * lim
    nr = lbr - 1.0
    cr = (nr * lre + lbi * lim) / den
    ci = (lbi * lre - nr * lim) / den
    bre = mats_ref[:, 0]
    bim = mats_ref[:, 1]
    bbr = cr * bre - ci * bim
    bbi = cr * bim + ci * bre
    bx = jnp.where(is_im, bbi, bbr)
    by = jnp.where(is_im, bbr, -bbi)
    cre = mats_ref[:, 2]
    cim = mats_ref[:, 3]
    cx = jnp.where(is_im, -cim, cre)
    cy = jnp.where(is_im, -cre, -cim)

    for t in range(CHUNK):
        rows = slice(t * S5_GROUP, (t + 1) * S5_GROUP)
        pr, pi = power(CHUNK - 1 - t, t)
        wb_ref[:, rows, :] = (pr * bx + pi * by).astype(BF16)
        pr, pi = power(t + 1, CHUNK - t)
        wct_ref[:, rows, :] = (pr * cx + pi * cy).astype(BF16)

    for j in range(2 * CHUNK):
        rows = slice(j * S5_GROUP, (j + 1) * S5_GROUP)
        if j == 2 * CHUNK - 1:
            ge_ref[:, rows, :] = jnp.zeros((pg, S5_GROUP, SW), F32)
            continue
        pr, pi = power(j - (CHUNK - 1) if j >= CHUNK - 1 else None,
                       (CHUNK - 1) - j if j <= CHUNK - 1 else None)
        ge_ref[:, rows, :] = pr * cx + pi * cy

    dsum = rows_ref[:, 3:4, :] + rows_ref[:, 4:5, :]
    r16 = lax.broadcasted_iota(jnp.int32, (S5_GROUP, CW), 0)
    c16 = lax.broadcasted_iota(jnp.int32, (S5_GROUP, CW), 1)
    for gi in range(pg):
        e = _dot_nt(bx[gi], ge_ref[gi], precision=lax.Precision.HIGHEST)
        blocks = []
        for tau in range(CHUNK):
            start = (CHUNK - 1 - tau) * S5_GROUP
            blocks.append(e[:, start:start + CW]
                          + jnp.where(c16 == r16 + tau * S5_GROUP, dsum[gi], 0.0))
        mt_ref[gi] = jnp.concatenate(blocks, axis=0).T.astype(BF16)

    lam_ref[...] = jnp.concatenate(
        [pwr_ref[:, CHUNK:CHUNK + 1, :], pwi_ref[:, CHUNK:CHUNK + 1, :],
         jnp.zeros((pg, 6, SW), F32)], axis=1)


def _s5_prep(rows, mats):
    g = N_GROUPS
    pg = GROUPS_PER_STRIP
    mat_out = pl.BlockSpec((pg, CW, SW), lambda i: (i, 0, 0))
    return pl.pallas_call(
        _s5_prep_kernel,
        grid=(g // pg,),
        in_specs=[pl.BlockSpec((pg,) + rows.shape[1:], lambda i: (i, 0, 0)),
                  pl.BlockSpec((pg,) + mats.shape[1:], lambda i: (i, 0, 0, 0))],
        out_specs=[mat_out, mat_out, mat_out, pl.BlockSpec((pg, 8, SW), lambda i: (i, 0, 0))],
        out_shape=[jax.ShapeDtypeStruct((g, CW, CW), BF16),
                   jax.ShapeDtypeStruct((g, CW, SW), BF16),
                   jax.ShapeDtypeStruct((g, CW, SW), BF16),
                   jax.ShapeDtypeStruct((g, 8, SW), F32)],
        scratch_shapes=[pltpu.VMEM((pg, N_POW, SW), F32), pltpu.VMEM((pg, N_POW, SW), F32),
                        pltpu.VMEM((pg, 2 * CW, SW), F32)],
        name="s5_prep",
    )(rows, mats)


class _Path(NamedTuple):
    n_seq: int
    seq_len: int
    pool_n: int
    first_block: int

    @property
    def time_blocks(self):
        return self.seq_len // TIME_BLOCK

    @property
    def n_blocks(self):
        return (self.n_seq // SEQ_TILE) * self.time_blocks


def _paths(shapes_and_pool):
    paths, first = [], 0
    for (n_seq, seq_len, _), pool_n in shapes_and_pool:
        assert n_seq % SEQ_TILE == 0 and seq_len % TIME_BLOCK == 0
        assert pool_n % TIME_BLOCK == 0 and seq_len % pool_n == 0 and HALO <= TIME_BLOCK
        run_blocks = pool_n // TIME_BLOCK
        assert run_blocks & (run_blocks - 1) == 0 and first % run_blocks == 0
        paths.append(_Path(n_seq, seq_len, pool_n, first))
        first += paths[-1].n_blocks
    return tuple(paths), first


def _block_coords(i, paths):
    coords = []
    for k, p in enumerate(paths):
        end = p.first_block + p.n_blocks
        inside = (i >= p.first_block) & (i < end)
        j = jnp.clip(i - p.first_block, 0, p.n_blocks - 1)
        coords.append((inside, (j // p.time_blocks) * SEQ_TILE, (j % p.time_blocks) * TIME_BLOCK))
    return coords


def _mod_block(i, paths):
    blk, first_mod = 0, 0
    for p in paths:
        j = jnp.clip(i - p.first_block, 0, p.n_blocks - 1)
        blk = jnp.where(i >= p.first_block, first_mod + j // p.time_blocks, blk)
        first_mod += p.n_seq // SEQ_TILE
    return blk


def _row_spec(width, n_blocks, shift=0):
    return pl.BlockSpec((TOK_BLOCK, width),
                        lambda i: (jnp.clip(i + shift, 0, n_blocks - 1), 0))


def _block_copies(hbm_refs, buf, sem, i, slot, paths, to_hbm):
    for (inside, seq0, t0), hbm in zip(_block_coords(i, paths), hbm_refs):
        copies = []
        for s in range(SEQ_TILE):
            rows = hbm.at[seq0 + s, pl.ds(t0, TIME_BLOCK), :]
            tile_rows = buf.at[slot, :, s, :]
            src, dst = (tile_rows, rows) if to_hbm else (rows, tile_rows)
            copies.append(pltpu.make_async_copy(src, dst, sem.at[slot]))
        yield inside, copies


def _start_block(hbm_refs, buf, sem, i, slot, paths, to_hbm):
    for inside, copies in _block_copies(hbm_refs, buf, sem, i, slot, paths, to_hbm):
        @pl.when(inside)
        def _():
            for c in copies:
                c.start()


def _wait_block(hbm_refs, buf, sem, slot, paths, to_hbm):
    _, copies = next(_block_copies(hbm_refs, buf, sem, 0, slot, paths, to_hbm))
    for c in copies:
        c.wait()


def _in_proj_kernel(*refs, paths):
    n_paths = len(paths)
    x_hbm = refs[:n_paths]
    mod_ref, w_ref, b_ref, xt_ref, ua_ref, ub_ref, sg_ref, xbuf, sem = refs[n_paths:]
    i = pl.program_id(0)
    n = pl.num_programs(0)
    slot = i % 2

    @pl.when(i == 0)
    def _():
        _start_block(x_hbm, xbuf, sem, i, slot, paths, to_hbm=False)

    @pl.when(i + 1 < n)
    def _():
        _start_block(x_hbm, xbuf, sem, i + 1, 1 - slot, paths, to_hbm=False)

    _wait_block(x_hbm, xbuf, sem, slot, paths, to_hbm=False)
    sub_time = TIME_BLOCK // IN_SLICES
    for sub in range(IN_SLICES):
        times = slice(sub * sub_time, (sub + 1) * sub_time)
        rows = slice(sub * sub_time * SEQ_TILE, (sub + 1) * sub_time * SEQ_TILE)
        x = xbuf[slot, times].reshape(sub_time * SEQ_TILE, D_MODEL)
        xt_ref[rows, :] = x
        h = _modulate(_layer_norm(x), 1.0 + mod_ref[1], mod_ref[0])
        z = _dot(h.astype(BF16), w_ref[...]) + b_ref[...]
        ua_ref[times] = z[:, :D_MODEL].reshape(sub_time, SEQ_TILE, D_MODEL)
        ub_ref[rows, :] = z[:, D_MODEL:D_MODEL + D_POOL].astype(BF16)
        sg_ref[rows, :] = _sigmoid(z[:, D_MODEL + D_POOL:]).astype(BF16)


def _in_proj(xs, paths, n_blocks, mods, w_in, b_in):
    n_tok = n_blocks * TOK_BLOCK
    const = lambda i: (0, 0)
    once = pl.Buffered(1)
    return pl.pallas_call(
        functools.partial(_in_proj_kernel, paths=paths),
        grid=(n_blocks,),
        in_specs=[pl.BlockSpec(memory_space=pl.ANY)] * len(xs) + [
            pl.BlockSpec((N_MOD, SEQ_TILE, D_MODEL), lambda i: (0, _mod_block(i, paths), 0)),
            pl.BlockSpec((D_MODEL, D_IN), const, pipeline_mode=once),
            pl.BlockSpec((1, D_IN), const),
        ],
        out_specs=[_row_spec(D_MODEL, n_blocks),
                   pl.BlockSpec((TIME_BLOCK, SEQ_TILE, D_MODEL), lambda i: (i, 0, 0)),
                   _row_spec(D_POOL, n_blocks), _row_spec(2 * D_MODEL, n_blocks)],
        out_shape=[jax.ShapeDtypeStruct((n_tok, D_MODEL), F32),
                   jax.ShapeDtypeStruct((n_blocks * TIME_BLOCK, SEQ_TILE, D_MODEL), F32),
                   jax.ShapeDtypeStruct((n_tok, D_POOL), BF16),
                   jax.ShapeDtypeStruct((n_tok, 2 * D_MODEL), BF16)],
        scratch_shapes=[pltpu.VMEM((2, TIME_BLOCK, SEQ_TILE, D_MODEL), F32),
                        pltpu.SemaphoreType.DMA((2,))],
        compiler_params=pltpu.CompilerParams(dimension_semantics=("arbitrary",),
                                             vmem_limit_bytes=VMEM_LIMIT),
        name="in_proj",
    )(*xs, mods, w_in, b_in)


def _s5_core_kernel(ua_ref, mt_ref, wb_ref, wct_ref, lam_ref, s0_ref, v_ref, fin_ref,
                    xt_ref, yt_ref, sloc_ref, sprf_ref, sprb_ref, fsc_ref, *, paths):
    for k, p in enumerate(paths):
        @pl.when(pl.program_id(1) == k)
        def _():
            _s5_strip(ua_ref, mt_ref, wb_ref, wct_ref, lam_ref, s0_ref, v_ref, fin_ref,
                      xt_ref, yt_ref, sloc_ref, sprf_ref, sprb_ref, fsc_ref,
                      n_seq=p.n_seq, seq_len=p.seq_len, has_init=k > 0)


def _s5_strip(ua_ref, mt_ref, wb_ref, wct_ref, lam_ref, s0_ref, v_ref, fin_ref,
              xt_ref, yt_ref, sloc_ref, sprf_ref, sprb_ref, fsc_ref, *, n_seq, seq_len, has_init):
    n_chunk = seq_len // CHUNK
    n_sb = n_seq // SEQ_TILE
    half = 2 * S5_STATE

    for tau in range(CHUNK):
        x_tau = jnp.concatenate([ua_ref[sb * seq_len + c * CHUNK + tau]
                                 for c in range(n_chunk) for sb in range(n_sb)], axis=0)
        xt = x_tau.astype(BF16).T
        for gi in range(GROUPS_PER_STRIP):
            xt_ref[gi, tau * S5_GROUP:(tau + 1) * S5_GROUP, :] = xt[gi * S5_GROUP:(gi + 1) * S5_GROUP, :]

    is_f = lax.broadcasted_iota(jnp.int32, (1, half), 1) < S5_STATE
    col = lax.broadcasted_iota(jnp.int32, (1, SW), 1)
    col_is_f = (col & (half - 1)) < S5_STATE

    def one_group(gi, slot):
        xg = xt_ref[gi]
        u = xg.T
        sloc_ref[slot] = _dot(u, wb_ref[gi])
        ar = lam_ref[gi, 0:1, 0:half]
        ai = lam_ref[gi, 1:2, 0:half]
        if has_init:
            s_re = s0_ref[gi, 0]
            s_im = s0_ref[gi, 1]
        else:
            s_re = jnp.zeros((n_seq, half), F32)
            s_im = jnp.zeros((n_seq, half), F32)
        for c in range(n_chunk):
            rf = pl.ds(c * n_seq, n_seq)
            rb = pl.ds((n_chunk - 1 - c) * n_seq, n_seq)
            sprf_ref[slot, rf, 0:half] = s_re
            sprf_ref[slot, rf, half:SW] = s_im
            sprb_ref[slot, rb, 0:half] = s_re
            sprb_ref[slot, rb, half:SW] = s_im
            l_re = jnp.where(is_f, sloc_ref[slot, rf, 0:half], sloc_ref[slot, rb, 0:half])
            l_im = jnp.where(is_f, sloc_ref[slot, rf, half:SW], sloc_ref[slot, rb, half:SW])
            s_re, s_im = (ar * s_re - ai * s_im + l_re,
                          ar * s_im + ai * s_re + l_im)
        if not has_init:
            fsc_ref[gi, 0] = s_re
            fsc_ref[gi, 1] = s_im
        sprev = jnp.where(col_is_f, sprf_ref[slot], sprb_ref[slot]).astype(BF16)
        yt = _dot(mt_ref[gi], xg) + _dot_nt(wct_ref[gi], sprev)
        yt_ref[gi] = jax.nn.gelu(yt)

    def group_body(i, carry):
        for slot in range(GROUP_UNROLL):
            one_group(i * GROUP_UNROLL + slot, slot)
        return carry

    lax.fori_loop(0, GROUPS_PER_STRIP // GROUP_UNROLL, group_body, 0)

    if not has_init:
        for part in range(2):
            fs = jnp.swapaxes(fsc_ref[:, part], 0, 1)
            fin_ref[:, part] = fs[:, :, 0:S5_STATE]
            fin_ref[:, 2 + part] = fs[:, :, S5_STATE:half]

    for t in range(CHUNK):
        vt = yt_ref[:, t * S5_GROUP:(t + 1) * S5_GROUP, :].reshape(LANES, n_seq * n_chunk)
        v_t = vt.T
        for c in range(n_chunk):
            for sb in range(n_sb):
                r0 = (c * n_sb + sb) * SEQ_TILE
                v_ref[sb * seq_len + c * CHUNK + t] = v_t[r0:r0 + SEQ_TILE, :]


def _s5_core(ua, paths, m, wb, wct, lam, s0):
    ctx, lat = paths
    tiles = ctx.n_blocks * TIME_BLOCK
    rows = tiles * SEQ_TILE // CHUNK
    assert lat.n_blocks * TIME_BLOCK == tiles and s0.shape[2] == lat.n_seq
    gps = GROUPS_PER_STRIP
    strip = pl.BlockSpec((tiles, SEQ_TILE, LANES), lambda j, k: (k, 0, j))
    mat = pl.BlockSpec((gps, CW, SW), lambda j, k: (j, 0, 0))
    return pl.pallas_call(
        functools.partial(_s5_core_kernel, paths=paths),
        grid=(N_STRIPS, len(paths)),
        in_specs=[strip, mat, mat, mat, pl.BlockSpec((gps, 8, SW), lambda j, k: (j, 0, 0)),
                  pl.BlockSpec((gps, 2, lat.n_seq, 2 * S5_STATE), lambda j, k: (j, 0, 0, 0))],
        out_specs=[strip,
                   pl.BlockSpec((ctx.n_seq, 4, gps, S5_STATE), lambda j, k: (0, 0, j, 0))],
        out_shape=[jax.ShapeDtypeStruct(ua.shape, F32),
                   jax.ShapeDtypeStruct((ctx.n_seq, 4, N_GROUPS, S5_STATE), F32)],
        scratch_shapes=[pltpu.VMEM((gps, CW, rows), BF16), pltpu.VMEM((gps, CW, rows), F32),
                        pltpu.VMEM((GROUP_UNROLL, rows, SW), F32),
                        pltpu.VMEM((GROUP_UNROLL, rows, SW), F32),
                        pltpu.VMEM((GROUP_UNROLL, rows, SW), F32),
                        pltpu.VMEM((gps, 2, ctx.n_seq, 2 * S5_STATE), F32)],
        compiler_params=pltpu.CompilerParams(dimension_semantics=("arbitrary", "arbitrary"),
                                             vmem_limit_bytes=VMEM_LIMIT),
        name="s5_core",
    )(ua, m, wb, wct, lam, s0)


def _mix_out_kernel(xt_ref, v_ref, ubp_ref, ub_ref, ubn_ref, sg_ref, mod_ref, wglu_ref, bglu_ref,
                    wpa_ref, wpool_ref, pscale_ref, wpb_ref, wout_ref, bout_ref, g1_ref, b1_ref,
                    o_ref, *, paths):
    i = pl.program_id(0)
    run_mask = 0
    for p in paths:
        run_mask = jnp.where(i >= p.first_block, p.pool_n // TIME_BLOCK - 1, run_mask)
    pos = i & run_mask
    first = pos == 0
    last = pos == run_mask
    halo_rows = HALO * SEQ_TILE
    u_prev = jnp.where(first, 0.0, ubp_ref[TOK_BLOCK - halo_rows:, :].astype(F32))
    u_next = jnp.where(last, 0.0, ubn_ref[:halo_rows, :].astype(F32))
    ext = jnp.concatenate([u_prev, ub_ref[...].astype(F32), u_next], axis=0)

    sub_rows = TOK_BLOCK // MIX_SLICES
    slices = [slice(sub * sub_rows, (sub + 1) * sub_rows) for sub in range(MIX_SLICES)]

    def s5_branch(sub):
        tiles = sub_rows // SEQ_TILE
        v = v_ref[sub * tiles:(sub + 1) * tiles].reshape(sub_rows, D_MODEL)
        glu = v * _sigmoid(_dot(v.astype(BF16), wglu_ref[...]) + bglu_ref[...])
        return _dot(glu.astype(BF16), wpa_ref[...])

    def pool_branch(sub):
        base = halo_rows + sub * sub_rows
        t_idx = (lax.broadcasted_iota(jnp.int32, (sub_rows, POOL_GROUP), 0)
                 + sub * sub_rows) // SEQ_TILE
        pooled = []
        for gi, w in enumerate(POOL_WINDOWS):
            cols = slice(gi * POOL_GROUP, (gi + 1) * POOL_GROUP)
            acc = jnp.zeros((sub_rows, POOL_GROUP), F32)
            for k in range(-(w // 2), w - w // 2):
                r0 = base + k * SEQ_TILE
                acc = acc + ext[r0:r0 + sub_rows, cols]
            lo = jnp.where(first, jnp.maximum(t_idx - w // 2, 0), t_idx - w // 2)
            hi = jnp.where(last, jnp.minimum(t_idx - w // 2 + w, TIME_BLOCK), t_idx - w // 2 + w)
            p = acc / (hi - lo).astype(F32) - ext[base:base + sub_rows, cols]
            pooled.append(_dot(p.astype(BF16), wpool_ref[gi]))
        pm = jnp.concatenate(pooled, axis=1) * pscale_ref[...]
        return _dot(pm.astype(BF16), wpb_ref[...])

    ya = [s5_branch(sub) for sub in range(MIX_SLICES)]
    yb = [pool_branch(sub) for sub in range(MIX_SLICES)]
    tm = []
    for sub, rows in enumerate(slices):
        merged = (sg_ref[rows, 0:D_MODEL].astype(F32) * ya[sub]
                  + sg_ref[rows, D_MODEL:2 * D_MODEL].astype(F32) * yb[sub])
        tm.append(_dot(merged.astype(BF16), wout_ref[...]) + bout_ref[...])
    for sub, rows in enumerate(slices):
        y = DEEPNORM_ALPHA * xt_ref[rows, :] + _modulate(tm[sub], mod_ref[2])
        o_ref[rows, :] = _layer_norm(y) * g1_ref[...] + b1_ref[...]


def _mix_out(xt, v, ub, sg, paths, n_blocks, mods, wglu, bglu, wpa, wpool, pscale, wpb, wout,
             bout, g1, b1):
    const2 = lambda i: (0, 0)
    once = pl.Buffered(1)
    vec = pl.BlockSpec((1, D_MODEL), const2)
    sq = pl.BlockSpec((D_MODEL, D_MODEL), const2, pipeline_mode=once)
    rows = functools.partial(_row_spec, n_blocks=n_blocks)
    return pl.pallas_call(
        functools.partial(_mix_out_kernel, paths=paths),
        grid=(n_blocks,),
        in_specs=[
            rows(D_MODEL),
            pl.BlockSpec((TIME_BLOCK, SEQ_TILE, D_MODEL), lambda i: (i, 0, 0)),
            rows(D_POOL, shift=-1), rows(D_POOL), rows(D_POOL, shift=1),
            rows(2 * D_MODEL),
            pl.BlockSpec((N_MOD, SEQ_TILE, D_MODEL), lambda i: (0, _mod_block(i, paths), 0)),
            sq, vec, sq,
            pl.BlockSpec((len(POOL_WINDOWS), POOL_GROUP, POOL_GROUP), lambda i: (0, 0, 0)),
            pl.BlockSpec((1, D_POOL), const2),
            pl.BlockSpec((D_POOL, D_MODEL), const2, pipeline_mode=once),
            sq, vec, vec, vec,
        ],
        out_specs=rows(D_MODEL),
        out_shape=jax.ShapeDtypeStruct((n_blocks * TOK_BLOCK, D_MODEL), F32),
        compiler_params=pltpu.CompilerParams(vmem_limit_bytes=VMEM_LIMIT),
        name="mix_out",
    )(xt, v, ub, ub, ub, sg, mods, wglu, bglu, wpa, wpool, pscale, wpb, wout, bout, g1, b1)


def _mlp_kernel(*refs, paths):
    n_paths = len(paths)
    x_ref, mod_ref, w1_ref, b1_ref, w2_ref, b2_ref, g2_ref, be2_ref = refs[:8]
    y_hbm = refs[8:8 + n_paths]
    obuf, sem = refs[8 + n_paths:]
    i = pl.program_id(0)
    n = pl.num_programs(0)
    slot = i % 2

    @pl.when(i >= 2)
    def _():
        _wait_block(y_hbm, obuf, sem, slot, paths, to_hbm=True)

    sub_time = TIME_BLOCK // MLP_SLICES
    for sub in range(MLP_SLICES):
        x = x_ref[sub * sub_time * SEQ_TILE:(sub + 1) * sub_time * SEQ_TILE, :]
        h = _modulate(_layer_norm(x), 1.0 + mod_ref[4], mod_ref[3]).astype(BF16)
        f = jnp.zeros(x.shape, F32)
        for k in range(D_FF // D_MODEL):
            cols = slice(k * D_MODEL, (k + 1) * D_MODEL)
            a = jnp.square(jax.nn.relu(_dot(h, w1_ref[:, cols]) + b1_ref[:, cols]))
            f = f + _dot(a.astype(BF16), w2_ref[cols, :])
        y = DEEPNORM_ALPHA * x + _modulate(f + b2_ref[...], mod_ref[5])
        obuf[slot, sub * sub_time:(sub + 1) * sub_time] = (
            _layer_norm(y) * g2_ref[...] + be2_ref[...]).reshape(sub_time, SEQ_TILE, D_MODEL)

    _start_block(y_hbm, obuf, sem, i, slot, paths, to_hbm=True)

    @pl.when(i == n - 1)
    def _():
        @pl.when(n >= 2)
        def _():
            _wait_block(y_hbm, obuf, sem, 1 - slot, paths, to_hbm=True)
        _wait_block(y_hbm, obuf, sem, slot, paths, to_hbm=True)


def _mlp(x1, paths, n_blocks, mods, w1, b1, w2, b2, g2, be2):
    const2 = lambda i: (0, 0)
    once = pl.Buffered(1)
    vec = pl.BlockSpec((1, D_MODEL), const2)
    return pl.pallas_call(
        functools.partial(_mlp_kernel, paths=paths),
        grid=(n_blocks,),
        in_specs=[
            _row_spec(D_MODEL, n_blocks),
            pl.BlockSpec((N_MOD, SEQ_TILE, D_MODEL), lambda i: (0, _mod_block(i, paths), 0)),
            pl.BlockSpec((D_MODEL, D_FF), const2, pipeline_mode=once),
            pl.BlockSpec((1, D_FF), const2),
            pl.BlockSpec((D_FF, D_MODEL), const2, pipeline_mode=once),
            vec, vec, vec,
        ],
        out_specs=[pl.BlockSpec(memory_space=pl.ANY)] * len(paths),
        out_shape=[jax.ShapeDtypeStruct((p.n_seq, p.seq_len, D_MODEL), F32) for p in paths],
        scratch_shapes=[pltpu.VMEM((2, TIME_BLOCK, SEQ_TILE, D_MODEL), F32),
                        pltpu.SemaphoreType.DMA((2,))],
        compiler_params=pltpu.CompilerParams(dimension_semantics=("arbitrary",),
                                             vmem_limit_bytes=VMEM_LIMIT),
        name="mlp",
    )(x1, mods, w1, b1, w2, b2, g2, be2)


def _state_cols(x_f, x_b):
    return jnp.concatenate([x_f, x_b, x_f, x_b], axis=-1)


def kernel(x_prompt, x_sample, state_s5, c, c_ctx, w_ada, b_ada, w_in, b_in, s5_lam_re, s5_lam_im, s5_log_dt, s5_b_re, s5_b_im, s5_c_re, s5_c_im, s5_d, w_glu, b_glu, w_proj_a, w_pool, pool_scale, w_proj_b, w_out, b_out, ln1_g, ln1_b, w_mlp1, b_mlp1, w_mlp2, b_mlp2, ln2_g, ln2_b):
    assert w_in.shape[0] == 1, "single-layer backbone"
    n_ctx = x_prompt.shape[0]
    n_lat = x_sample.shape[0]
    g, p, hh = N_GROUPS, S5_STATE, S5_GROUP
    paths, n_blocks = _paths([(x_prompt.shape, x_prompt.shape[1]), (x_sample.shape, GRID_W)])

    n_vec = 1 + n_lat
    n_rows = -(-n_vec // 8) * 8
    cvec = jnp.concatenate([c_ctx[None, :], c, jnp.zeros((n_rows - n_vec, D_MODEL), F32)], axis=0)
    mods = _mods(cvec, w_ada[0], b_ada[0][None, :])
    mods = jnp.concatenate([jnp.broadcast_to(mods[:, 0:1], (N_MOD, n_ctx, D_MODEL)),
                            mods[:, 1:n_vec]], axis=1)

    dirs = lambda x: _state_cols(x[0], x[1])
    rows = jnp.stack(
        [dirs(s5_lam_re[0]), dirs(s5_lam_im[0]),
         dirs(jnp.broadcast_to(s5_log_dt[0][:, :, None], (2, g, p))),
         jnp.tile(s5_d[0, 0].reshape(g, hh), (1, CHUNK)),
         jnp.tile(s5_d[0, 1].reshape(g, hh), (1, CHUNK))], axis=1)
    mats = jnp.stack(
        [dirs(s5_b_re[0].transpose(0, 1, 3, 2)), dirs(s5_b_im[0].transpose(0, 1, 3, 2)),
         dirs(s5_c_re[0]), dirs(s5_c_im[0])], axis=1)
    mt_mat, wb_mat, wct_mat, lam16 = _s5_prep(rows, mats)

    w_in_b = w_in[0].astype(BF16)
    b_in_r = b_in[0][None, :]
    xt, ua, ub, sg = _in_proj((x_prompt, x_sample), paths, n_blocks, mods, w_in_b, b_in_r)

    st = state_s5[:, 0].astype(F32)
    s0 = jnp.concatenate([st[:, 0], st[:, 1]], axis=-1).transpose(2, 1, 0, 3)
    v, fin = _s5_core(ua, paths, mt_mat, wb_mat, wct_mat, lam16, s0)
    new_state = fin.reshape(n_ctx, 1, 2, 2, g, p)

    x1 = _mix_out(xt, v, ub, sg, paths, n_blocks, mods,
                  w_glu[0].astype(BF16), b_glu[0][None, :], w_proj_a[0].astype(BF16),
                  w_pool[0].astype(BF16), pool_scale[0][None, :], w_proj_b[0].astype(BF16),
                  w_out[0].astype(BF16), b_out[0][None, :], ln1_g[0][None, :], ln1_b[0][None, :])
    y_p, y_s = _mlp(x1, paths, n_blocks, mods,
                    w_mlp1[0].astype(BF16), b_mlp1[0][None, :], w_mlp2[0].astype(BF16),
                    b_mlp2[0][None, :], ln2_g[0][None, :], ln2_b[0][None, :])
    return (y_p, y_s, new_state)
```

```python
import functools
from typing import NamedTuple

import jax
import jax.numpy as jnp
from jax import lax
from jax.experimental import pallas as pl
from jax.experimental.pallas import tpu as pltpu

F32 = jnp.float32
BF16 = jnp.bfloat16

D_MODEL = 1024
S5_GROUP = 16
N_GROUPS = D_MODEL // S5_GROUP
S5_STATE = 64
D_POOL = D_MODEL // 2
POOL_WINDOWS = (2, 4, 8, 16)
POOL_GROUP = D_POOL // len(POOL_WINDOWS)
D_IN = D_MODEL + D_POOL + 2 * D_MODEL
D_REST = D_IN - D_MODEL
D_FF = 4 * D_MODEL
N_MOD = 6
GRID_W = 64
DEEPNORM_ALPHA = 2.0 ** 0.25
LN_EPS = 1e-6

CHUNK = 16
CW = CHUNK * S5_GROUP
SW = 4 * S5_STATE
LANES = 128
N_STRIPS = D_MODEL // LANES
GROUPS_PER_STRIP = LANES // S5_GROUP
GROUP_UNROLL = 4
N_POW = 32
SEQ_TILE = 8
TIME_BLOCK = 128
TOK_BLOCK = SEQ_TILE * TIME_BLOCK
HALO = max(POOL_WINDOWS) // 2
IN_SLICES = 4
MLP_SLICES = 4
MIX_SLICES = 4
VMEM_LIMIT = 56 * 1024 * 1024


def _time_major(x3):
    return jnp.swapaxes(x3, 0, 1).reshape(x3.shape[1] * SEQ_TILE, x3.shape[-1])


def _seq_major(x2):
    return jnp.swapaxes(x2.reshape(x2.shape[0] // SEQ_TILE, SEQ_TILE, x2.shape[-1]), 0, 1)


def _modulate(a, mul, add=None):
    a3 = a.reshape(a.shape[0] // SEQ_TILE, SEQ_TILE, a.shape[-1]) * mul[None]
    if add is not None:
        a3 = a3 + add[None]
    return a3.reshape(a.shape)


def _layer_norm(x):
    mu = jnp.mean(x, axis=-1, keepdims=True)
    xc = x - mu
    var = jnp.mean(xc * xc, axis=-1, keepdims=True)
    return xc * lax.rsqrt(var + LN_EPS)


def _sigmoid(x):
    return 0.5 * jnp.tanh(0.5 * x) + 0.5


def _dot(a, b):
    return jnp.dot(a, b, preferred_element_type=F32)


def _dot_nt(a, b, precision=None):
    return lax.dot_general(a, b, (((1,), (1,)), ((), ())), precision=precision,
                           preferred_element_type=F32)


def _mods_kernel(c_ref, w_ref, b_ref, o_ref):
    s = jax.nn.silu(c_ref[...])
    o_ref[0] = jnp.dot(s, w_ref[...], precision=lax.Precision.HIGHEST,
                       preferred_element_type=F32) + b_ref[...]


def _mods(cvec, w_ada, b_ada):
    rows = cvec.shape[0]
    n_out = w_ada.shape[1]
    return pl.pallas_call(
        _mods_kernel,
        grid=(n_out // D_MODEL,),
        in_specs=[
            pl.BlockSpec((rows, D_MODEL), lambda j: (0, 0)),
            pl.BlockSpec((D_MODEL, D_MODEL), lambda j: (0, j)),
            pl.BlockSpec((1, D_MODEL), lambda j: (0, j)),
        ],
        out_specs=pl.BlockSpec((1, rows, D_MODEL), lambda j: (j, 0, 0)),
        out_shape=jax.ShapeDtypeStruct((n_out // D_MODEL, rows, D_MODEL), F32),
        name="mods",
    )(cvec, w_ada, b_ada)


def _s5_prep_kernel(rows_ref, mats_ref, mt_ref, wb_ref, wct_ref, lam_ref,
                    pwr_ref, pwi_ref, ge_ref):
    pg = GROUPS_PER_STRIP
    lre = rows_ref[:, 0:1, :]
    lim = rows_ref[:, 1:2, :]
    dt = jnp.exp(rows_ref[:, 2:3, :])
    a = lre * dt
    b = lim * dt
    col = lax.broadcasted_iota(jnp.int32, (1, 1, SW), 2)
    is_im = col >= 2 * S5_STATE
    is_b = (col & (2 * S5_STATE - 1)) >= S5_STATE

    mag = jnp.exp(a)
    sq_r = mag * jnp.cos(b)
    sq_i = mag * jnp.sin(b)
    pwr_ref[:, 0:1, :] = jnp.ones((pg, 1, SW), F32)
    pwi_ref[:, 0:1, :] = jnp.zeros((pg, 1, SW), F32)
    m = 1
    while m < N_POW:
        lo_r = pwr_ref[:, 0:m, :]
        lo_i = pwi_ref[:, 0:m, :]
        pwr_ref[:, m:2 * m, :] = lo_r * sq_r - lo_i * sq_i
        pwi_ref[:, m:2 * m, :] = lo_r * sq_i + lo_i * sq_r
        sq_r, sq_i = sq_r * sq_r - sq_i * sq_i, 2.0 * sq_r * sq_i
        m *= 2

    def power(kf, kb):
        def row(ref, kk):
            return jnp.zeros((pg, 1, SW), F32) if kk is None else ref[:, kk:kk + 1, :]
        if kf == kb:
            return row(pwr_ref, kf), row(pwi_ref, kf)
        return (jnp.where(is_b, row(pwr_ref, kb), row(pwr_ref, kf)),
                jnp.where(is_b, row(pwi_ref, kb), row(pwi_ref, kf)))

    lbr = pwr_ref[:, 1:2, :]
    lbi = pwi_ref[:, 1:2, :]
    den = lre * lre + lim * lim
    nr = lbr - 1.0
    cr = (nr * lre + lbi * lim) / den
    ci = (lbi * lre - nr * lim) / den
    bre = mats_ref[:, 0]
    bim = mats_ref[:, 1]
    bbr = cr * bre - ci * bim
    bbi = cr * bim + ci * bre
    bx = jnp.where(is_im, bbi, bbr)
    by = jnp.where(is_im, bbr, -bbi)
    cre = mats_ref[:, 2]
    cim = mats_ref[:, 3]
    cx = jnp.where(is_im, -cim, cre)
    cy = jnp.where(is_im, -cre, -cim)

    for t in range(CHUNK):
        rows = slice(t * S5_GROUP, (t + 1) * S5_GROUP)
        pr, pi = power(CHUNK - 1 - t, t)
        wb_ref[:, rows, :] = (pr * bx + pi * by).astype(BF16)
        pr, pi = power(t + 1, CHUNK - t)
        wct_ref[:, rows, :] = (pr * cx + pi * cy).astype(BF16)

    for j in range(2 * CHUNK):
        rows = slice(j * S5_GROUP, (j + 1) * S5_GROUP)
        if j == 2 * CHUNK - 1:
            ge_ref[:, rows, :] = jnp.zeros((pg, S5_GROUP, SW), F32)
            continue
        pr, pi = power(j - (CHUNK - 1) if j >= CHUNK - 1 else None,
                       (CHUNK - 1) - j if j <= CHUNK - 1 else None)
        ge_ref[:, rows, :] = pr * cx + pi * cy

    dsum = rows_ref[:, 3:4, :] + rows_ref[:, 4:5, :]
    r16 = lax.broadcasted_iota(jnp.int32, (S5_GROUP, CW), 0)
    c16 = lax.broadcasted_iota(jnp.int32, (S5_GROUP, CW), 1)
    for gi in range(pg):
        e = _dot_nt(bx[gi], ge_ref[gi], precision=lax.Precision.HIGHEST)
        blocks = []
        for tau in range(CHUNK):
            start = (CHUNK - 1 - tau) * S5_GROUP
            blocks.append(e[:, start:start + CW]
                          + jnp.where(c16 == r16 + tau * S5_GROUP, dsum[gi], 0.0))
        mt_ref[gi] = jnp.concatenate(blocks, axis=0).T.astype(BF16)

    lam_ref[...] = jnp.concatenate(
        [pwr_ref[:, CHUNK:CHUNK + 1, :], pwi_ref[:, CHUNK:CHUNK + 1, :],
         jnp.zeros((pg, 6, SW), F32)], axis=1)


def _s5_prep(rows, mats):
    g = N_GROUPS
    pg = GROUPS_PER_STRIP
    mat_out = pl.BlockSpec((pg, CW, SW), lambda i: (i, 0, 0))
    return pl.pallas_call(
        _s5_prep_kernel,
        grid=(g // pg,),
        in_specs=[pl.BlockSpec((pg,) + rows.shape[1:], lambda i: (i, 0, 0)),
                  pl.BlockSpec((pg,) + mats.shape[1:], lambda i: (i, 0, 0, 0))],
        out_specs=[mat_out, mat_out, mat_out, pl.BlockSpec((pg, 8, SW), lambda i: (i, 0, 0))],
        out_shape=[jax.ShapeDtypeStruct((g, CW, CW), BF16),
                   jax.ShapeDtypeStruct((g, CW, SW), BF16),
                   jax.ShapeDtypeStruct((g, CW, SW), BF16),
                   jax.ShapeDtypeStruct((g, 8, SW), F32)],
        scratch_shapes=[pltpu.VMEM((pg, N_POW, SW), F32), pltpu.VMEM((pg, N_POW, SW), F32),
                        pltpu.VMEM((pg, 2 * CW, SW), F32)],
        name="s5_prep",
    )(rows, mats)


class _Path(NamedTuple):
    n_seq: int
    seq_len: int
    pool_n: int
    first_block: int

    @property
    def time_blocks(self):
        return self.seq_len // TIME_BLOCK

    @property
    def n_blocks(self):
        return (self.n_seq // SEQ_TILE) * self.time_blocks


def _paths(shapes_and_pool):
    paths, first = [], 0
    for (n_seq, seq_len, _), pool_n in shapes_and_pool:
        assert n_seq % SEQ_TILE == 0 and seq_len % TIME_BLOCK == 0
        sub_time = TIME_BLOCK // MIX_SLICES
        assert pool_n % sub_time == 0 and seq_len % pool_n == 0 and HALO <= sub_time
        assert pool_n & (pool_n - 1) == 0
        paths.append(_Path(n_seq, seq_len, pool_n, first))
        first += paths[-1].n_blocks
    return tuple(paths), first


def _block_coords(i, paths):
    coords = []
    for k, p in enumerate(paths):
        end = p.first_block + p.n_blocks
        inside = (i >= p.first_block) & (i < end)
        j = jnp.clip(i - p.first_block, 0, p.n_blocks - 1)
        coords.append((inside, (j // p.time_blocks) * SEQ_TILE, (j % p.time_blocks) * TIME_BLOCK))
    return coords


def _mod_block(i, paths):
    blk, first_mod = 0, 0
    for p in paths:
        j = jnp.clip(i - p.first_block, 0, p.n_blocks - 1)
        blk = jnp.where(i >= p.first_block, first_mod + j // p.time_blocks, blk)
        first_mod += p.n_seq // SEQ_TILE
    return blk


def _row_spec(width, n_blocks, shift=0):
    return pl.BlockSpec((TOK_BLOCK, width),
                        lambda i: (jnp.clip(i + shift, 0, n_blocks - 1), 0))


def _block_copies(hbm_refs, buf, sem, i, slot, paths, to_hbm):
    for (inside, seq0, t0), hbm in zip(_block_coords(i, paths), hbm_refs):
        copies = []
        for s in range(SEQ_TILE):
            rows = hbm.at[seq0 + s, pl.ds(t0, TIME_BLOCK), :]
            tile_rows = buf.at[slot, :, s, :]
            src, dst = (tile_rows, rows) if to_hbm else (rows, tile_rows)
            copies.append(pltpu.make_async_copy(src, dst, sem.at[slot]))
        yield inside, copies


def _start_block(hbm_refs, buf, sem, i, slot, paths, to_hbm):
    for inside, copies in _block_copies(hbm_refs, buf, sem, i, slot, paths, to_hbm):
        @pl.when(inside)
        def _():
            for c in copies:
                c.start()


def _wait_block(hbm_refs, buf, sem, slot, paths, to_hbm):
    _, copies = next(_block_copies(hbm_refs, buf, sem, 0, slot, paths, to_hbm))
    for c in copies:
        c.wait()


def _in_proj_kernel(*refs, paths):
    n_paths = len(paths)
    x_hbm = refs[:n_paths]
    mod_ref, w_ref, b_ref, xt_ref, ua_ref, ub_ref, sg_ref, xbuf, sem = refs[n_paths:]
    i = pl.program_id(0)
    n = pl.num_programs(0)
    slot = i % 2

    @pl.when(i == 0)
    def _():
        _start_block(x_hbm, xbuf, sem, i, slot, paths, to_hbm=False)

    @pl.when(i + 1 < n)
    def _():
        _start_block(x_hbm, xbuf, sem, i + 1, 1 - slot, paths, to_hbm=False)

    _wait_block(x_hbm, xbuf, sem, slot, paths, to_hbm=False)
    sub_time = TIME_BLOCK // IN_SLICES
    for sub in range(IN_SLICES):
        times = slice(sub * sub_time, (sub + 1) * sub_time)
        rows = slice(sub * sub_time * SEQ_TILE, (sub + 1) * sub_time * SEQ_TILE)
        x = xbuf[slot, times].reshape(sub_time * SEQ_TILE, D_MODEL)
        xt_ref[rows, :] = x
        h = _modulate(_layer_norm(x), 1.0 + mod_ref[1], mod_ref[0])
        z = _dot(h.astype(BF16), w_ref[...]) + b_ref[...]
        ua_ref[times] = z[:, :D_MODEL].reshape(sub_time, SEQ_TILE, D_MODEL)
        ub_ref[rows, :] = z[:, D_MODEL:D_MODEL + D_POOL].astype(BF16)
        sg_ref[rows, :] = _sigmoid(z[:, D_MODEL + D_POOL:]).astype(BF16)


def _in_proj(xs, paths, n_blocks, mods, w_in, b_in):
    n_tok = n_blocks * TOK_BLOCK
    const = lambda i: (0, 0)
    once = pl.Buffered(1)
    return pl.pallas_call(
        functools.partial(_in_proj_kernel, paths=paths),
        grid=(n_blocks,),
        in_specs=[pl.BlockSpec(memory_space=pl.ANY)] * len(xs) + [
            pl.BlockSpec((N_MOD, SEQ_TILE, D_MODEL), lambda i: (0, _mod_block(i, paths), 0)),
            pl.BlockSpec((D_MODEL, D_IN), const, pipeline_mode=once),
            pl.BlockSpec((1, D_IN), const),
        ],
        out_specs=[_row_spec(D_MODEL, n_blocks),
                   pl.BlockSpec((TIME_BLOCK, SEQ_TILE, D_MODEL), lambda i: (i, 0, 0)),
                   _row_spec(D_POOL, n_blocks), _row_spec(2 * D_MODEL, n_blocks)],
        out_shape=[jax.ShapeDtypeStruct((n_tok, D_MODEL), F32),
                   jax.ShapeDtypeStruct((n_blocks * TIME_BLOCK, SEQ_TILE, D_MODEL), F32),
                   jax.ShapeDtypeStruct((n_tok, D_POOL), BF16),
                   jax.ShapeDtypeStruct((n_tok, 2 * D_MODEL), BF16)],
        scratch_shapes=[pltpu.VMEM((2, TIME_BLOCK, SEQ_TILE, D_MODEL), F32),
                        pltpu.SemaphoreType.DMA((2,))],
        compiler_params=pltpu.CompilerParams(dimension_semantics=("arbitrary",),
                                             vmem_limit_bytes=VMEM_LIMIT),
        name="in_proj",
    )(*xs, mods, w_in, b_in)


def _s5_core_kernel(ua_ref, mt_ref, wb_ref, wct_ref, lam_ref, s0_ref, v_ref, fin_ref,
                    xt_ref, yt_ref, sloc_ref, sprf_ref, sprb_ref, fsc_ref, *, paths):
    for k, p in enumerate(paths):
        @pl.when(pl.program_id(1) == k)
        def _():
            _s5_strip(ua_ref, mt_ref, wb_ref, wct_ref, lam_ref, s0_ref, v_ref, fin_ref,
                      xt_ref, yt_ref, sloc_ref, sprf_ref, sprb_ref, fsc_ref,
                      n_seq=p.n_seq, seq_len=p.seq_len, has_init=k > 0)


def _s5_strip(ua_ref, mt_ref, wb_ref, wct_ref, lam_ref, s0_ref, v_ref, fin_ref,
              xt_ref, yt_ref, sloc_ref, sprf_ref, sprb_ref, fsc_ref, *, n_seq, seq_len, has_init):
    n_chunk = seq_len // CHUNK
    n_sb = n_seq // SEQ_TILE
    half = 2 * S5_STATE

    for tau in range(CHUNK):
        x_tau = jnp.concatenate([ua_ref[sb * seq_len + c * CHUNK + tau]
                                 for c in range(n_chunk) for sb in range(n_sb)], axis=0)
        xt = x_tau.astype(BF16).T
        for gi in range(GROUPS_PER_STRIP):
            xt_ref[gi, tau * S5_GROUP:(tau + 1) * S5_GROUP, :] = xt[gi * S5_GROUP:(gi + 1) * S5_GROUP, :]

    is_f = lax.broadcasted_iota(jnp.int32, (1, half), 1) < S5_STATE
    col = lax.broadcasted_iota(jnp.int32, (1, SW), 1)
    col_is_f = (col & (half - 1)) < S5_STATE

    def one_group(gi, slot):
        xg = xt_ref[gi]
        u = xg.T
        sloc_ref[slot] = _dot(u, wb_ref[gi])
        ar = lam_ref[gi, 0:1, 0:half]
        ai = lam_ref[gi, 1:2, 0:half]
        if has_init:
            s_re = s0_ref[gi, 0]
            s_im = s0_ref[gi, 1]
        else:
            s_re = jnp.zeros((n_seq, half), F32)
            s_im = jnp.zeros((n_seq, half), F32)
        for c in range(n_chunk):
            rf = pl.ds(c * n_seq, n_seq)
            rb = pl.ds((n_chunk - 1 - c) * n_seq, n_seq)
            sprf_ref[slot, rf, 0:half] = s_re
            sprf_ref[slot, rf, half:SW] = s_im
            sprb_ref[slot, rb, 0:half] = s_re
            sprb_ref[slot, rb, half:SW] = s_im
            l_re = jnp.where(is_f, sloc_ref[slot, rf, 0:half], sloc_ref[slot, rb, 0:half])
            l_im = jnp.where(is_f, sloc_ref[slot, rf, half:SW], sloc_ref[slot, rb, half:SW])
            s_re, s_im = (ar * s_re - ai * s_im + l_re,
                          ar * s_im + ai * s_re + l_im)
        if not has_init:
            fsc_ref[gi, 0] = s_re
            fsc_ref[gi, 1] = s_im
        sprev = jnp.where(col_is_f, sprf_ref[slot], sprb_ref[slot]).astype(BF16)
        yt = _dot(mt_ref[gi], xg) + _dot_nt(wct_ref[gi], sprev)
        yt_ref[gi] = jax.nn.gelu(yt)

    def group_body(i, carry):
        for slot in range(GROUP_UNROLL):
            one_group(i * GROUP_UNROLL + slot, slot)
        return carry

    lax.fori_loop(0, GROUPS_PER_STRIP // GROUP_UNROLL, group_body, 0)

    if not has_init:
        for part in range(2):
            fs = jnp.swapaxes(fsc_ref[:, part], 0, 1)
            fin_ref[:, part] = fs[:, :, 0:S5_STATE]
            fin_ref[:, 2 + part] = fs[:, :, S5_STATE:half]

    for t in range(CHUNK):
        vt = yt_ref[:, t * S5_GROUP:(t + 1) * S5_GROUP, :].reshape(LANES, n_seq * n_chunk)
        v_t = vt.T
        for c in range(n_chunk):
            for sb in range(n_sb):
                r0 = (c * n_sb + sb) * SEQ_TILE
                v_ref[sb * seq_len + c * CHUNK + t] = v_t[r0:r0 + SEQ_TILE, :]


def _s5_core(ua, paths, m, wb, wct, lam, s0):
    ctx, lat = paths
    tiles = ctx.n_blocks * TIME_BLOCK
    rows = tiles * SEQ_TILE // CHUNK
    assert lat.n_blocks * TIME_BLOCK == tiles and s0.shape[2] == lat.n_seq
    gps = GROUPS_PER_STRIP
    strip = pl.BlockSpec((tiles, SEQ_TILE, LANES), lambda j, k: (k, 0, j))
    mat = pl.BlockSpec((gps, CW, SW), lambda j, k: (j, 0, 0))
    return pl.pallas_call(
        functools.partial(_s5_core_kernel, paths=paths),
        grid=(N_STRIPS, len(paths)),
        in_specs=[strip, mat, mat, mat, pl.BlockSpec((gps, 8, SW), lambda j, k: (j, 0, 0)),
                  pl.BlockSpec((gps, 2, lat.n_seq, 2 * S5_STATE), lambda j, k: (j, 0, 0, 0))],
        out_specs=[strip,
                   pl.BlockSpec((ctx.n_seq, 4, gps, S5_STATE), lambda j, k: (0, 0, j, 0))],
        out_shape=[jax.ShapeDtypeStruct(ua.shape, F32),
                   jax.ShapeDtypeStruct((ctx.n_seq, 4, N_GROUPS, S5_STATE), F32)],
        scratch_shapes=[pltpu.VMEM((gps, CW, rows), BF16), pltpu.VMEM((gps, CW, rows), F32),
                        pltpu.VMEM((GROUP_UNROLL, rows, SW), F32),
                        pltpu.VMEM((GROUP_UNROLL, rows, SW), F32),
                        pltpu.VMEM((GROUP_UNROLL, rows, SW), F32),
                        pltpu.VMEM((gps, 2, ctx.n_seq, 2 * S5_STATE), F32)],
        compiler_params=pltpu.CompilerParams(dimension_semantics=("arbitrary", "arbitrary"),
                                             vmem_limit_bytes=VMEM_LIMIT),
        name="s5_core",
    )(ua, m, wb, wct, lam, s0)


def _mix_out_kernel(xt_ref, v_ref, ubp_ref, ub_ref, ubn_ref, sg_ref, mod_ref, wglu_ref, bglu_ref,
                    wpa_ref, wpool_ref, pscale_ref, wpb_ref, wout_ref, bout_ref, g1_ref, b1_ref,
                    o_ref, *, paths):
    i = pl.program_id(0)
    t_block, run_mask = 0, 0
    for p in paths:
        j = jnp.clip(i - p.first_block, 0, p.n_blocks - 1)
        t_block = jnp.where(i >= p.first_block, (j % p.time_blocks) * TIME_BLOCK, t_block)
        run_mask = jnp.where(i >= p.first_block, p.pool_n - 1, run_mask)
    halo_rows = HALO * SEQ_TILE
    ext = jnp.concatenate([ubp_ref[...], ub_ref[...], ubn_ref[...]], axis=0).astype(F32)

    sub_rows = TOK_BLOCK // MIX_SLICES
    sub_time = TIME_BLOCK // MIX_SLICES
    slices = [slice(sub * sub_rows, (sub + 1) * sub_rows) for sub in range(MIX_SLICES)]

    def s5_branch(sub):
        tiles = sub_rows // SEQ_TILE
        v = v_ref[sub * tiles:(sub + 1) * tiles].reshape(sub_rows, D_MODEL)
        glu = v * _sigmoid(_dot(v.astype(BF16), wglu_ref[...]) + bglu_ref[...])
        return _dot(glu.astype(BF16), wpa_ref[...])

    def pool_branch(sub):
        pos = (t_block + sub * sub_time) & run_mask
        first = pos == 0
        last = pos + sub_time == run_mask + 1
        base = sub * sub_rows
        win = jnp.concatenate(
            [jnp.where(first, 0.0, ext[base:base + halo_rows]),
             ext[base + halo_rows:base + halo_rows + sub_rows],
             jnp.where(last, 0.0, ext[base + halo_rows + sub_rows:base + 2 * halo_rows + sub_rows])],
            axis=0)
        t_idx = lax.broadcasted_iota(jnp.int32, (sub_rows, POOL_GROUP), 0) // SEQ_TILE
        pooled = []
        for gi, w in enumerate(POOL_WINDOWS):
            cols = slice(gi * POOL_GROUP, (gi + 1) * POOL_GROUP)
            acc = jnp.zeros((sub_rows, POOL_GROUP), F32)
            for k in range(-(w // 2), w - w // 2):
                r0 = halo_rows + k * SEQ_TILE
                acc = acc + win[r0:r0 + sub_rows, cols]
            lo = jnp.where(first, jnp.maximum(t_idx - w // 2, 0), t_idx - w // 2)
            hi = jnp.where(last, jnp.minimum(t_idx - w // 2 + w, sub_time), t_idx - w // 2 + w)
            p = acc / (hi - lo).astype(F32) - win[halo_rows:halo_rows + sub_rows, cols]
            pooled.append(_dot(p.astype(BF16), wpool_ref[gi]))
        pm = jnp.concatenate(pooled, axis=1) * pscale_ref[...]
        return _dot(pm.astype(BF16), wpb_ref[...])

    ya = [s5_branch(sub) for sub in range(MIX_SLICES)]
    yb = [pool_branch(sub) for sub in range(MIX_SLICES)]
    tm = []
    for sub, rows in enumerate(slices):
        merged = (sg_ref[rows, 0:D_MODEL].astype(F32) * ya[sub]
                  + sg_ref[rows, D_MODEL:2 * D_MODEL].astype(F32) * yb[sub])
        tm.append(_dot(merged.astype(BF16), wout_ref[...]) + bout_ref[...])
    for sub, rows in enumerate(slices):
        y = DEEPNORM_ALPHA * xt_ref[rows, :] + _modulate(tm[sub], mod_ref[2])
        o_ref[rows, :] = _layer_norm(y) * g1_ref[...] + b1_ref[...]


def _mix_out(xt, v, ub, sg, paths, n_blocks, mods, wglu, bglu, wpa, wpool, pscale, wpb, wout,
             bout, g1, b1):
    const2 = lambda i: (0, 0)
    once = pl.Buffered(1)
    vec = pl.BlockSpec((1, D_MODEL), const2)
    sq = pl.BlockSpec((D_MODEL, D_MODEL), const2, pipeline_mode=once)
    rows = functools.partial(_row_spec, n_blocks=n_blocks)
    halo_rows = HALO * SEQ_TILE
    halos_per_block = TOK_BLOCK // halo_rows
    n_halo_blocks = n_blocks * halos_per_block
    halo = lambda index: pl.BlockSpec((halo_rows, D_POOL), lambda i: (index(i), 0))
    return pl.pallas_call(
        functools.partial(_mix_out_kernel, paths=paths),
        grid=(n_blocks,),
        in_specs=[
            rows(D_MODEL),
            pl.BlockSpec((TIME_BLOCK, SEQ_TILE, D_MODEL), lambda i: (i, 0, 0)),
            halo(lambda i: jnp.maximum(i * halos_per_block - 1, 0)), rows(D_POOL),
            halo(lambda i: jnp.minimum((i + 1) * halos_per_block, n_halo_blocks - 1)),
            rows(2 * D_MODEL),
            pl.BlockSpec((N_MOD, SEQ_TILE, D_MODEL), lambda i: (0, _mod_block(i, paths), 0)),
            sq, vec, sq,
            pl.BlockSpec((len(POOL_WINDOWS), POOL_GROUP, POOL_GROUP), lambda i: (0, 0, 0)),
            pl.BlockSpec((1, D_POOL), const2),
            pl.BlockSpec((D_POOL, D_MODEL), const2, pipeline_mode=once),
            sq, vec, vec, vec,
        ],
        out_specs=rows(D_MODEL),
        out_shape=jax.ShapeDtypeStruct((n_blocks * TOK_BLOCK, D_MODEL), F32),
        compiler_params=pltpu.CompilerParams(vmem_limit_bytes=VMEM_LIMIT),
        name="mix_out",
    )(xt, v, ub, ub, ub, sg, mods, wglu, bglu, wpa, wpool, pscale, wpb, wout, bout, g1, b1)


def _mlp_kernel(*refs, paths):
    n_paths = len(paths)
    x_ref, mod_ref, w1_ref, b1_ref, w2_ref, b2_ref, g2_ref, be2_ref = refs[:8]
    y_hbm = refs[8:8 + n_paths]
    obuf, sem = refs[8 + n_paths:]
    i = pl.program_id(0)
    n = pl.num_programs(0)
    slot = i % 2

    @pl.when(i >= 2)
    def _():
        _wait_block(y_hbm, obuf, sem, slot, paths, to_hbm=True)

    sub_time = TIME_BLOCK // MLP_SLICES
    for sub in range(MLP_SLICES):
        x = x_ref[sub * sub_time * SEQ_TILE:(sub + 1) * sub_time * SEQ_TILE, :]
        h = _modulate(_layer_norm(x), 1.0 + mod_ref[4], mod_ref[3]).astype(BF16)
        f = jnp.zeros(x.shape, F32)
        for k in range(D_FF // D_MODEL):
            cols = slice(k * D_MODEL, (k + 1) * D_MODEL)
            a = jnp.square(jax.nn.relu(_dot(h, w1_ref[:, cols]) + b1_ref[:, cols]))
            f = f + _dot(a.astype(BF16), w2_ref[cols, :])
        y = DEEPNORM_ALPHA * x + _modulate(f + b2_ref[...], mod_ref[5])
        obuf[slot, sub * sub_time:(sub + 1) * sub_time] = (
            _layer_norm(y) * g2_ref[...] + be2_ref[...]).reshape(sub_time, SEQ_TILE, D_MODEL)

    _start_block(y_hbm, obuf, sem, i, slot, paths, to_hbm=True)

    @pl.when(i == n - 1)
    def _():
        @pl.when(n >= 2)
        def _():
            _wait_block(y_hbm, obuf, sem, 1 - slot, paths, to_hbm=True)
        _wait_block(y_hbm, obuf, sem, slot, paths, to_hbm=True)


def _mlp(x1, paths, n_blocks, mods, w1, b1, w2, b2, g2, be2):
    const2 = lambda i: (0, 0)
    once = pl.Buffered(1)
    vec = pl.BlockSpec((1, D_MODEL), const2)
    return pl.pallas_call(
        functools.partial(_mlp_kernel, paths=paths),
        grid=(n_blocks,),
        in_specs=[
            _row_spec(D_MODEL, n_blocks),
            pl.BlockSpec((N_MOD, SEQ_TILE, D_MODEL), lambda i: (0, _mod_block(i, paths), 0)),
            pl.BlockSpec((D_MODEL, D_FF), const2, pipeline_mode=once),
            pl.BlockSpec((1, D_FF), const2),
            pl.BlockSpec((D_FF, D_MODEL), const2, pipeline_mode=once),
            vec, vec, vec,
        ],
        out_specs=[pl.BlockSpec(memory_space=pl.ANY)] * len(paths),
        out_shape=[jax.ShapeDtypeStruct((p.n_seq, p.seq_len, D_MODEL), F32) for p in paths],
        scratch_shapes=[pltpu.VMEM((2, TIME_BLOCK, SEQ_TILE, D_MODEL), F32),
                        pltpu.SemaphoreType.DMA((2,))],
        compiler_params=pltpu.CompilerParams(dimension_semantics=("arbitrary",),
                                             vmem_limit_bytes=VMEM_LIMIT),
        name="mlp",
    )(x1, mods, w1, b1, w2, b2, g2, be2)


def _state_cols(x_f, x_b):
    return jnp.concatenate([x_f, x_b, x_f, x_b], axis=-1)


def kernel(x_prompt, x_sample, state_s5, c, c_ctx, w_ada, b_ada, w_in, b_in, s5_lam_re, s5_lam_im, s5_log_dt, s5_b_re, s5_b_im, s5_c_re, s5_c_im, s5_d, w_glu, b_glu, w_proj_a, w_pool, pool_scale, w_proj_b, w_out, b_out, ln1_g, ln1_b, w_mlp1, b_mlp1, w_mlp2, b_mlp2, ln2_g, ln2_b):
    assert w_in.shape[0] == 1, "single-layer backbone"
    n_ctx = x_prompt.shape[0]
    n_lat = x_sample.shape[0]
    g, p, hh = N_GROUPS, S5_STATE, S5_GROUP
    paths, n_blocks = _paths([(x_prompt.shape, x_prompt.shape[1]), (x_sample.shape, GRID_W)])

    n_vec = 1 + n_lat
    n_rows = -(-n_vec // 8) * 8
    cvec = jnp.concatenate([c_ctx[None, :], c, jnp.zeros((n_rows - n_vec, D_MODEL), F32)], axis=0)
    mods = _mods(cvec, w_ada[0], b_ada[0][None, :])
    mods = jnp.concatenate([jnp.broadcast_to(mods[:, 0:1], (N_MOD, n_ctx, D_MODEL)),
                            mods[:, 1:n_vec]], axis=1)

    dirs = lambda x: _state_cols(x[0], x[1])
    rows = jnp.stack(
        [dirs(s5_lam_re[0]), dirs(s5_lam_im[0]),
         dirs(jnp.broadcast_to(s5_log_dt[0][:, :, None], (2, g, p))),
         jnp.tile(s5_d[0, 0].reshape(g, hh), (1, CHUNK)),
         jnp.tile(s5_d[0, 1].reshape(g, hh), (1, CHUNK))], axis=1)
    mats = jnp.stack(
        [dirs(s5_b_re[0].transpose(0, 1, 3, 2)), dirs(s5_b_im[0].transpose(0, 1, 3, 2)),
         dirs(s5_c_re[0]), dirs(s5_c_im[0])], axis=1)
    mt_mat, wb_mat, wct_mat, lam16 = _s5_prep(rows, mats)

    w_in_b = w_in[0].astype(BF16)
    b_in_r = b_in[0][None, :]
    xt, ua, ub, sg = _in_proj((x_prompt, x_sample), paths, n_blocks, mods, w_in_b, b_in_r)

    st = state_s5[:, 0].astype(F32)
    s0 = jnp.concatenate([st[:, 0], st[:, 1]], axis=-1).transpose(2, 1, 0, 3)
    v, fin = _s5_core(ua, paths, mt_mat, wb_mat, wct_mat, lam16, s0)
    new_state = fin.reshape(n_ctx, 1, 2, 2, g, p)

    x1 = _mix_out(xt, v, ub, sg, paths, n_blocks, mods,
                  w_glu[0].astype(BF16), b_glu[0][None, :], w_proj_a[0].astype(BF16),
                  w_pool[0].astype(BF16), pool_scale[0][None, :], w_proj_b[0].astype(BF16),
                  w_out[0].astype(BF16), b_out[0][None, :], ln1_g[0][None, :], ln1_b[0][None, :])
    y_p, y_s = _mlp(x1, paths, n_blocks, mods,
                    w_mlp1[0].astype(BF16), b_mlp1[0][None, :], w_mlp2[0].astype(BF16),
                    b_mlp2[0][None, :], ln2_g[0][None, :], ln2_b[0][None, :])
    return (y_p, y_s, new_state)
```

```python
import functools
from typing import NamedTuple

import jax
import jax.numpy as jnp
from jax import lax
from jax.experimental import pallas as pl
from jax.experimental.pallas import tpu as pltpu

F32 = jnp.float32
BF16 = jnp.bfloat16

D_MODEL = 1024
S5_GROUP = 16
N_GROUPS = D_MODEL // S5_GROUP
S5_STATE = 64
D_POOL = D_MODEL // 2
POOL_WINDOWS = (2, 4, 8, 16)
POOL_GROUP = D_POOL // len(POOL_WINDOWS)
D_IN = D_MODEL + D_POOL + 2 * D_MODEL
D_REST = D_IN - D_MODEL
D_FF = 4 * D_MODEL
N_MOD = 6
GRID_W = 64
DEEPNORM_ALPHA = 2.0 ** 0.25
LN_EPS = 1e-6

CHUNK = 16
CW = CHUNK * S5_GROUP
SW = 4 * S5_STATE
LANES = 128
N_STRIPS = D_MODEL // LANES
GROUPS_PER_STRIP = LANES // S5_GROUP
GROUP_UNROLL = 4
N_POW = 32
SEQ_TILE = 8
TIME_BLOCK = 128
TOK_BLOCK = SEQ_TILE * TIME_BLOCK
HALO = max(POOL_WINDOWS) // 2
IN_SLICES = 4
MLP_SLICES = 4
MIX_SLICES = 4
WEIGHT_CHUNK_ROWS = 128
VMEM_LIMIT = 56 * 1024 * 1024


def _time_major(x3):
    return jnp.swapaxes(x3, 0, 1).reshape(x3.shape[1] * SEQ_TILE, x3.shape[-1])


def _seq_major(x2):
    return jnp.swapaxes(x2.reshape(x2.shape[0] // SEQ_TILE, SEQ_TILE, x2.shape[-1]), 0, 1)


def _modulate(a, mul, add=None):
    a3 = a.reshape(a.shape[0] // SEQ_TILE, SEQ_TILE, a.shape[-1]) * mul[None]
    if add is not None:
        a3 = a3 + add[None]
    return a3.reshape(a.shape)


def _layer_norm(x):
    mu = jnp.mean(x, axis=-1, keepdims=True)
    xc = x - mu
    var = jnp.mean(xc * xc, axis=-1, keepdims=True)
    return xc * lax.rsqrt(var + LN_EPS)


def _sigmoid(x):
    return 0.5 * jnp.tanh(0.5 * x) + 0.5


def _dot(a, b):
    return jnp.dot(a, b, preferred_element_type=F32)


def _dot_nt(a, b, precision=None):
    return lax.dot_general(a, b, (((1,), (1,)), ((), ())), precision=precision,
                           preferred_element_type=F32)


def _mods_kernel(c_ref, w_ref, b_ref, o_ref):
    s = jax.nn.silu(c_ref[...])
    o_ref[0] = jnp.dot(s, w_ref[...], precision=lax.Precision.HIGHEST,
                       preferred_element_type=F32) + b_ref[...]


def _mods(cvec, w_ada, b_ada):
    rows = cvec.shape[0]
    n_out = w_ada.shape[1]
    return pl.pallas_call(
        _mods_kernel,
        grid=(n_out // D_MODEL,),
        in_specs=[
            pl.BlockSpec((rows, D_MODEL), lambda j: (0, 0)),
            pl.BlockSpec((D_MODEL, D_MODEL), lambda j: (0, j)),
            pl.BlockSpec((1, D_MODEL), lambda j: (0, j)),
        ],
        out_specs=pl.BlockSpec((1, rows, D_MODEL), lambda j: (j, 0, 0)),
        out_shape=jax.ShapeDtypeStruct((n_out // D_MODEL, rows, D_MODEL), F32),
        name="mods",
    )(cvec, w_ada, b_ada)


def _s5_prep_kernel(rows_ref, mats_ref, mt_ref, wb_ref, wct_ref, lam_ref,
                    pwr_ref, pwi_ref, ge_ref):
    pg = GROUPS_PER_STRIP
    lre = rows_ref[:, 0:1, :]
    lim = rows_ref[:, 1:2, :]
    dt = jnp.exp(rows_ref[:, 2:3, :])
    a = lre * dt
    b = lim * dt
    col = lax.broadcasted_iota(jnp.int32, (1, 1, SW), 2)
    is_im = col >= 2 * S5_STATE
    is_b = (col & (2 * S5_STATE - 1)) >= S5_STATE

    mag = jnp.exp(a)
    sq_r = mag * jnp.cos(b)
    sq_i = mag * jnp.sin(b)
    pwr_ref[:, 0:1, :] = jnp.ones((pg, 1, SW), F32)
    pwi_ref[:, 0:1, :] = jnp.zeros((pg, 1, SW), F32)
    m = 1
    while m < N_POW:
        lo_r = pwr_ref[:, 0:m, :]
        lo_i = pwi_ref[:, 0:m, :]
        pwr_ref[:, m:2 * m, :] = lo_r * sq_r - lo_i * sq_i
        pwi_ref[:, m:2 * m, :] = lo_r * sq_i + lo_i * sq_r
        sq_r, sq_i = sq_r * sq_r - sq_i * sq_i, 2.0 * sq_r * sq_i
        m *= 2

    def power(kf, kb):
        def row(ref, kk):
            return jnp.zeros((pg, 1, SW), F32) if kk is None else ref[:, kk:kk + 1, :]
        if kf == kb:
            return row(pwr_ref, kf), row(pwi_ref, kf)
        return (jnp.where(is_b, row(pwr_ref, kb), row(pwr_ref, kf)),
                jnp.where(is_b, row(pwi_ref, kb), row(pwi_ref, kf)))

    lbr = pwr_ref[:, 1:2, :]
    lbi = pwi_ref[:, 1:2, :]
    den = lre * lre + lim * lim
    nr = lbr - 1.0
    cr = (nr * lre + lbi * lim) / den
    ci = (lbi * lre - nr * lim) / den
    bre = mats_ref[:, 0]
    bim = mats_ref[:, 1]
    bbr = cr * bre - ci * bim
    bbi = cr * bim + ci * bre
    bx = jnp.where(is_im, bbi, bbr)
    by = jnp.where(is_im, bbr, -bbi)
    cre = mats_ref[:, 2]
    cim = mats_ref[:, 3]
    cx = jnp.where(is_im, -cim, cre)
    cy = jnp.where(is_im, -cre, -cim)

    for t in range(CHUNK):
        rows = slice(t * S5_GROUP, (t + 1) * S5_GROUP)
        pr, pi = power(CHUNK - 1 - t, t)
        wb_ref[:, rows, :] = (pr * bx + pi * by).astype(BF16)
        pr, pi = power(t + 1, CHUNK - t)
        wct_ref[:, rows, :] = (pr * cx + pi * cy).astype(BF16)

    for j in range(2 * CHUNK):
        rows = slice(j * S5_GROUP, (j + 1) * S5_GROUP)
        if j == 2 * CHUNK - 1:
            ge_ref[:, rows, :] = jnp.zeros((pg, S5_GROUP, SW), F32)
            continue
        pr, pi = power(j - (CHUNK - 1) if j >= CHUNK - 1 else None,
                       (CHUNK - 1) - j if j <= CHUNK - 1 else None)
        ge_ref[:, rows, :] = pr * cx + pi * cy

    dsum = rows_ref[:, 3:4, :] + rows_ref[:, 4:5, :]
    r16 = lax.broadcasted_iota(jnp.int32, (S5_GROUP, CW), 0)
    c16 = lax.broadcasted_iota(jnp.int32, (S5_GROUP, CW), 1)
    for gi in range(pg):
        e = _dot_nt(bx[gi], ge_ref[gi], precision=lax.Precision.HIGHEST)
        blocks = []
        for tau in range(CHUNK):
            start = (CHUNK - 1 - tau) * S5_GROUP
            blocks.append(e[:, start:start + CW]
                          + jnp.where(c16 == r16 + tau * S5_GROUP, dsum[gi], 0.0))
        mt_ref[gi] = jnp.concatenate(blocks, axis=0).T.astype(BF16)

    lam_ref[...] = jnp.concatenate(
        [pwr_ref[:, CHUNK:CHUNK + 1, :], pwi_ref[:, CHUNK:CHUNK + 1, :],
         jnp.zeros((pg, 6, SW), F32)], axis=1)


def _s5_prep(rows, mats):
    g = N_GROUPS
    pg = GROUPS_PER_STRIP
    mat_out = pl.BlockSpec((pg, CW, SW), lambda i: (i, 0, 0))
    return pl.pallas_call(
        _s5_prep_kernel,
        grid=(g // pg,),
        in_specs=[pl.BlockSpec((pg,) + rows.shape[1:], lambda i: (i, 0, 0)),
                  pl.BlockSpec((pg,) + mats.shape[1:], lambda i: (i, 0, 0, 0))],
        out_specs=[mat_out, mat_out, mat_out, pl.BlockSpec((pg, 8, SW), lambda i: (i, 0, 0))],
        out_shape=[jax.ShapeDtypeStruct((g, CW, CW), BF16),
                   jax.ShapeDtypeStruct((g, CW, SW), BF16),
                   jax.ShapeDtypeStruct((g, CW, SW), BF16),
                   jax.ShapeDtypeStruct((g, 8, SW), F32)],
        scratch_shapes=[pltpu.VMEM((pg, N_POW, SW), F32), pltpu.VMEM((pg, N_POW, SW), F32),
                        pltpu.VMEM((pg, 2 * CW, SW), F32)],
        name="s5_prep",
    )(rows, mats)


class _Path(NamedTuple):
    n_seq: int
    seq_len: int
    pool_n: int
    first_block: int

    @property
    def time_blocks(self):
        return self.seq_len // TIME_BLOCK

    @property
    def n_blocks(self):
        return (self.n_seq // SEQ_TILE) * self.time_blocks


def _paths(shapes_and_pool):
    paths, first = [], 0
    for (n_seq, seq_len, _), pool_n in shapes_and_pool:
        assert n_seq % SEQ_TILE == 0 and seq_len % TIME_BLOCK == 0
        sub_time = TIME_BLOCK // MIX_SLICES
        assert pool_n % sub_time == 0 and seq_len % pool_n == 0 and HALO <= sub_time
        assert pool_n & (pool_n - 1) == 0
        paths.append(_Path(n_seq, seq_len, pool_n, first))
        first += paths[-1].n_blocks
    return tuple(paths), first


def _block_coords(i, paths):
    coords = []
    for k, p in enumerate(paths):
        end = p.first_block + p.n_blocks
        inside = (i >= p.first_block) & (i < end)
        j = jnp.clip(i - p.first_block, 0, p.n_blocks - 1)
        coords.append((inside, (j // p.time_blocks) * SEQ_TILE, (j % p.time_blocks) * TIME_BLOCK))
    return coords


def _mod_block(i, paths):
    blk, first_mod = 0, 0
    for p in paths:
        j = jnp.clip(i - p.first_block, 0, p.n_blocks - 1)
        blk = jnp.where(i >= p.first_block, first_mod + j // p.time_blocks, blk)
        first_mod += p.n_seq // SEQ_TILE
    return blk


def _row_spec(width, n_blocks, shift=0):
    return pl.BlockSpec((TOK_BLOCK, width),
                        lambda i: (jnp.clip(i + shift, 0, n_blocks - 1), 0))


def _block_copies(hbm_refs, buf, sem, i, slot, paths, to_hbm):
    for (inside, seq0, t0), hbm in zip(_block_coords(i, paths), hbm_refs):
        copies = []
        for s in range(SEQ_TILE):
            rows = hbm.at[seq0 + s, pl.ds(t0, TIME_BLOCK), :]
            tile_rows = buf.at[slot, :, s, :]
            src, dst = (tile_rows, rows) if to_hbm else (rows, tile_rows)
            copies.append(pltpu.make_async_copy(src, dst, sem.at[slot]))
        yield inside, copies


def _start_block(hbm_refs, buf, sem, i, slot, paths, to_hbm):
    for inside, copies in _block_copies(hbm_refs, buf, sem, i, slot, paths, to_hbm):
        @pl.when(inside)
        def _():
            for c in copies:
                c.start()


def _wait_block(hbm_refs, buf, sem, slot, paths, to_hbm):
    _, copies = next(_block_copies(hbm_refs, buf, sem, 0, slot, paths, to_hbm))
    for c in copies:
        c.wait()


def _stage_weight(w_hbm, w_ref, stage, sem):
    rows = stage.shape[1]
    n_chunks = w_hbm.shape[0] // rows

    def chunk_copy(c):
        return pltpu.make_async_copy(w_hbm.at[pl.ds(c * rows, rows), :], stage.at[c % 2],
                                     sem.at[c % 2])

    chunk_copy(0).start()
    for c in range(n_chunks):
        if c + 1 < n_chunks:
            chunk_copy(c + 1).start()
        chunk_copy(c).wait()
        w_ref[c * rows:(c + 1) * rows, :] = stage[c % 2].astype(BF16)


def _in_proj_kernel(*refs, paths):
    n_paths = len(paths)
    x_hbm = refs[:n_paths]
    (mod_ref, w_hbm, b_ref, xt_ref, ua_ref, ub_ref, sg_ref,
     xbuf, sem, w_ref, wstage, wsem) = refs[n_paths:]
    i = pl.program_id(0)
    n = pl.num_programs(0)
    slot = i % 2

    @pl.when(i == 0)
    def _():
        _start_block(x_hbm, xbuf, sem, i, slot, paths, to_hbm=False)
        _stage_weight(w_hbm, w_ref, wstage, wsem)

    @pl.when(i + 1 < n)
    def _():
        _start_block(x_hbm, xbuf, sem, i + 1, 1 - slot, paths, to_hbm=False)

    _wait_block(x_hbm, xbuf, sem, slot, paths, to_hbm=False)
    sub_time = TIME_BLOCK // IN_SLICES
    for sub in range(IN_SLICES):
        times = slice(sub * sub_time, (sub + 1) * sub_time)
        rows = slice(sub * sub_time * SEQ_TILE, (sub + 1) * sub_time * SEQ_TILE)
        x = xbuf[slot, times].reshape(sub_time * SEQ_TILE, D_MODEL)
        xt_ref[rows, :] = x
        h = _modulate(_layer_norm(x), 1.0 + mod_ref[1], mod_ref[0])
        z = _dot(h.astype(BF16), w_ref[...]) + b_ref[...]
        ua_ref[times] = z[:, :D_MODEL].reshape(sub_time, SEQ_TILE, D_MODEL)
        ub_ref[rows, :] = z[:, D_MODEL:D_MODEL + D_POOL].astype(BF16)
        sg_ref[rows, :] = _sigmoid(z[:, D_MODEL + D_POOL:]).astype(BF16)


def _in_proj(xs, paths, n_blocks, mods, w_in, b_in):
    n_tok = n_blocks * TOK_BLOCK
    const = lambda i: (0, 0)
    return pl.pallas_call(
        functools.partial(_in_proj_kernel, paths=paths),
        grid=(n_blocks,),
        in_specs=[pl.BlockSpec(memory_space=pl.ANY)] * len(xs) + [
            pl.BlockSpec((N_MOD, SEQ_TILE, D_MODEL), lambda i: (0, _mod_block(i, paths), 0)),
            pl.BlockSpec(memory_space=pl.ANY),
            pl.BlockSpec((1, D_IN), const),
        ],
        out_specs=[_row_spec(D_MODEL, n_blocks),
                   pl.BlockSpec((TIME_BLOCK, SEQ_TILE, D_MODEL), lambda i: (i, 0, 0)),
                   _row_spec(D_POOL, n_blocks), _row_spec(2 * D_MODEL, n_blocks)],
        out_shape=[jax.ShapeDtypeStruct((n_tok, D_MODEL), F32),
                   jax.ShapeDtypeStruct((n_blocks * TIME_BLOCK, SEQ_TILE, D_MODEL), F32),
                   jax.ShapeDtypeStruct((n_tok, D_POOL), BF16),
                   jax.ShapeDtypeStruct((n_tok, 2 * D_MODEL), BF16)],
        scratch_shapes=[pltpu.VMEM((2, TIME_BLOCK, SEQ_TILE, D_MODEL), F32),
                        pltpu.SemaphoreType.DMA((2,)),
                        pltpu.VMEM((D_MODEL, D_IN), BF16),
                        pltpu.VMEM((2, WEIGHT_CHUNK_ROWS, D_IN), F32),
                        pltpu.SemaphoreType.DMA((2,))],
        compiler_params=pltpu.CompilerParams(dimension_semantics=("arbitrary",),
                                             vmem_limit_bytes=VMEM_LIMIT),
        name="in_proj",
    )(*xs, mods, w_in, b_in)


def _s5_core_kernel(ua_ref, mt_ref, wb_ref, wct_ref, lam_ref, s0_ref, v_ref, fin_ref,
                    xt_ref, yt_ref, sloc_ref, sprf_ref, sprb_ref, fsc_ref, *, paths):
    for k, p in enumerate(paths):
        @pl.when(pl.program_id(1) == k)
        def _():
            _s5_strip(ua_ref, mt_ref, wb_ref, wct_ref, lam_ref, s0_ref, v_ref, fin_ref,
                      xt_ref, yt_ref, sloc_ref, sprf_ref, sprb_ref, fsc_ref,
                      n_seq=p.n_seq, seq_len=p.seq_len, has_init=k > 0)


def _s5_strip(ua_ref, mt_ref, wb_ref, wct_ref, lam_ref, s0_ref, v_ref, fin_ref,
              xt_ref, yt_ref, sloc_ref, sprf_ref, sprb_ref, fsc_ref, *, n_seq, seq_len, has_init):
    n_chunk = seq_len // CHUNK
    n_sb = n_seq // SEQ_TILE
    half = 2 * S5_STATE

    for tau in range(CHUNK):
        x_tau = jnp.concatenate([ua_ref[sb * seq_len + c * CHUNK + tau]
                                 for c in range(n_chunk) for sb in range(n_sb)], axis=0)
        xt = x_tau.astype(BF16).T
        for gi in range(GROUPS_PER_STRIP):
            xt_ref[gi, tau * S5_GROUP:(tau + 1) * S5_GROUP, :] = xt[gi * S5_GROUP:(gi + 1) * S5_GROUP, :]

    is_f = lax.broadcasted_iota(jnp.int32, (1, half), 1) < S5_STATE
    col = lax.broadcasted_iota(jnp.int32, (1, SW), 1)
    col_is_f = (col & (half - 1)) < S5_STATE

    def one_group(gi, slot):
        xg = xt_ref[gi]
        u = xg.T
        sloc_ref[slot] = _dot(u, wb_ref[gi])
        ar = lam_ref[gi, 0:1, 0:half]
        ai = lam_ref[gi, 1:2, 0:half]
        if has_init:
            s_re = s0_ref[gi, 0]
            s_im = s0_ref[gi, 1]
        else:
            s_re = jnp.zeros((n_seq, half), F32)
            s_im = jnp.zeros((n_seq, half), F32)
        for c in range(n_chunk):
            rf = pl.ds(c * n_seq, n_seq)
            rb = pl.ds((n_chunk - 1 - c) * n_seq, n_seq)
            sprf_ref[slot, rf, 0:half] = s_re
            sprf_ref[slot, rf, half:SW] = s_im
            sprb_ref[slot, rb, 0:half] = s_re
            sprb_ref[slot, rb, half:SW] = s_im
            l_re = jnp.where(is_f, sloc_ref[slot, rf, 0:half], sloc_ref[slot, rb, 0:half])
            l_im = jnp.where(is_f, sloc_ref[slot, rf, half:SW], sloc_ref[slot, rb, half:SW])
            s_re, s_im = (ar * s_re - ai * s_im + l_re,
                          ar * s_im + ai * s_re + l_im)
        if not has_init:
            fsc_ref[gi, 0] = s_re
            fsc_ref[gi, 1] = s_im
        sprev = jnp.where(col_is_f, sprf_ref[slot], sprb_ref[slot]).astype(BF16)
        yt = _dot(mt_ref[gi], xg) + _dot_nt(wct_ref[gi], sprev)
        yt_ref[gi] = jax.nn.gelu(yt)

    def group_body(i, carry):
        for slot in range(GROUP_UNROLL):
            one_group(i * GROUP_UNROLL + slot, slot)
        return carry

    lax.fori_loop(0, GROUPS_PER_STRIP // GROUP_UNROLL, group_body, 0)

    if not has_init:
        for part in range(2):
            fs = jnp.swapaxes(fsc_ref[:, part], 0, 1)
            fin_ref[:, part] = fs[:, :, 0:S5_STATE]
            fin_ref[:, 2 + part] = fs[:, :, S5_STATE:half]

    for t in range(CHUNK):
        vt = yt_ref[:, t * S5_GROUP:(t + 1) * S5_GROUP, :].reshape(LANES, n_seq * n_chunk)
        v_t = vt.T
        for c in range(n_chunk):
            for sb in range(n_sb):
                r0 = (c * n_sb + sb) * SEQ_TILE
                v_ref[sb * seq_len + c * CHUNK + t] = v_t[r0:r0 + SEQ_TILE, :]


def _s5_core(ua, paths, m, wb, wct, lam, s0):
    ctx, lat = paths
    tiles = ctx.n_blocks * TIME_BLOCK
    rows = tiles * SEQ_TILE // CHUNK
    assert lat.n_blocks * TIME_BLOCK == tiles and s0.shape[2] == lat.n_seq
    gps = GROUPS_PER_STRIP
    strip = pl.BlockSpec((tiles, SEQ_TILE, LANES), lambda j, k: (k, 0, j))
    mat = pl.BlockSpec((gps, CW, SW), lambda j, k: (j, 0, 0))
    return pl.pallas_call(
        functools.partial(_s5_core_kernel, paths=paths),
        grid=(N_STRIPS, len(paths)),
        in_specs=[strip, mat, mat, mat, pl.BlockSpec((gps, 8, SW), lambda j, k: (j, 0, 0)),
                  pl.BlockSpec((gps, 2, lat.n_seq, 2 * S5_STATE), lambda j, k: (j, 0, 0, 0))],
        out_specs=[strip,
                   pl.BlockSpec((ctx.n_seq, 4, gps, S5_STATE), lambda j, k: (0, 0, j, 0))],
        out_shape=[jax.ShapeDtypeStruct(ua.shape, F32),
                   jax.ShapeDtypeStruct((ctx.n_seq, 4, N_GROUPS, S5_STATE), F32)],
        scratch_shapes=[pltpu.VMEM((gps, CW, rows), BF16), pltpu.VMEM((gps, CW, rows), F32),
                        pltpu.VMEM((GROUP_UNROLL, rows, SW), F32),
                        pltpu.VMEM((GROUP_UNROLL, rows, SW), F32),
                        pltpu.VMEM((GROUP_UNROLL, rows, SW), F32),
                        pltpu.VMEM((gps, 2, ctx.n_seq, 2 * S5_STATE), F32)],
        compiler_params=pltpu.CompilerParams(dimension_semantics=("arbitrary", "arbitrary"),
                                             vmem_limit_bytes=VMEM_LIMIT),
        name="s5_core",
    )(ua, m, wb, wct, lam, s0)


def _mix_out_kernel(xt_ref, v_ref, ubp_ref, ub_ref, ubn_ref, sg_ref, mod_ref, wglu_ref, bglu_ref,
                    wpa_ref, wpool_ref, pscale_ref, wpb_ref, wout_ref, bout_ref, g1_ref, b1_ref,
                    o_ref, *, paths):
    i = pl.program_id(0)
    t_block, run_mask = 0, 0
    for p in paths:
        j = jnp.clip(i - p.first_block, 0, p.n_blocks - 1)
        t_block = jnp.where(i >= p.first_block, (j % p.time_blocks) * TIME_BLOCK, t_block)
        run_mask = jnp.where(i >= p.first_block, p.pool_n - 1, run_mask)
    halo_rows = HALO * SEQ_TILE
    ext = jnp.concatenate([ubp_ref[...], ub_ref[...], ubn_ref[...]], axis=0).astype(F32)

    sub_rows = TOK_BLOCK // MIX_SLICES
    sub_time = TIME_BLOCK // MIX_SLICES
    slices = [slice(sub * sub_rows, (sub + 1) * sub_rows) for sub in range(MIX_SLICES)]

    def s5_branch(sub):
        tiles = sub_rows // SEQ_TILE
        v = v_ref[sub * tiles:(sub + 1) * tiles].reshape(sub_rows, D_MODEL)
        glu = v * _sigmoid(_dot(v.astype(BF16), wglu_ref[...]) + bglu_ref[...])
        return _dot(glu.astype(BF16), wpa_ref[...])

    def pool_branch(sub):
        pos = (t_block + sub * sub_time) & run_mask
        first = pos == 0
        last = pos + sub_time == run_mask + 1
        base = sub * sub_rows
        win = jnp.concatenate(
            [jnp.where(first, 0.0, ext[base:base + halo_rows]),
             ext[base + halo_rows:base + halo_rows + sub_rows],
             jnp.where(last, 0.0, ext[base + halo_rows + sub_rows:base + 2 * halo_rows + sub_rows])],
            axis=0)
        t_idx = lax.broadcasted_iota(jnp.int32, (sub_rows, POOL_GROUP), 0) // SEQ_TILE
        pooled = []
        for gi, w in enumerate(POOL_WINDOWS):
            cols = slice(gi * POOL_GROUP, (gi + 1) * POOL_GROUP)
            acc = jnp.zeros((sub_rows, POOL_GROUP), F32)
            for k in range(-(w // 2), w - w // 2):
                r0 = halo_rows + k * SEQ_TILE
                acc = acc + win[r0:r0 + sub_rows, cols]
            lo = jnp.where(first, jnp.maximum(t_idx - w // 2, 0), t_idx - w // 2)
            hi = jnp.where(last, jnp.minimum(t_idx - w // 2 + w, sub_time), t_idx - w // 2 + w)
            p = acc / (hi - lo).astype(F32) - win[halo_rows:halo_rows + sub_rows, cols]
            pooled.append(_dot(p.astype(BF16), wpool_ref[gi]))
        pm = jnp.concatenate(pooled, axis=1) * pscale_ref[...]
        return _dot(pm.astype(BF16), wpb_ref[...])

    ya = [s5_branch(sub) for sub in range(MIX_SLICES)]
    yb = [pool_branch(sub) for sub in range(MIX_SLICES)]
    tm = []
    for sub, rows in enumerate(slices):
        merged = (sg_ref[rows, 0:D_MODEL].astype(F32) * ya[sub]
                  + sg_ref[rows, D_MODEL:2 * D_MODEL].astype(F32) * yb[sub])
        tm.append(_dot(merged.astype(BF16), wout_ref[...]) + bout_ref[...])
    for sub, rows in enumerate(slices):
        y = DEEPNORM_ALPHA * xt_ref[rows, :] + _modulate(tm[sub], mod_ref[2])
        o_ref[rows, :] = _layer_norm(y) * g1_ref[...] + b1_ref[...]


def _mix_out(xt, v, ub, sg, paths, n_blocks, mods, wglu, bglu, wpa, wpool, pscale, wpb, wout,
             bout, g1, b1):
    const2 = lambda i: (0, 0)
    once = pl.Buffered(1)
    vec = pl.BlockSpec((1, D_MODEL), const2)
    sq = pl.BlockSpec((D_MODEL, D_MODEL), const2, pipeline_mode=once)
    rows = functools.partial(_row_spec, n_blocks=n_blocks)
    halo_rows = HALO * SEQ_TILE
    halos_per_block = TOK_BLOCK // halo_rows
    n_halo_blocks = n_blocks * halos_per_block
    halo = lambda index: pl.BlockSpec((halo_rows, D_POOL), lambda i: (index(i), 0))
    return pl.pallas_call(
        functools.partial(_mix_out_kernel, paths=paths),
        grid=(n_blocks,),
        in_specs=[
            rows(D_MODEL),
            pl.BlockSpec((TIME_BLOCK, SEQ_TILE, D_MODEL), lambda i: (i, 0, 0)),
            halo(lambda i: jnp.maximum(i * halos_per_block - 1, 0)), rows(D_POOL),
            halo(lambda i: jnp.minimum((i + 1) * halos_per_block, n_halo_blocks - 1)),
            rows(2 * D_MODEL),
            pl.BlockSpec((N_MOD, SEQ_TILE, D_MODEL), lambda i: (0, _mod_block(i, paths), 0)),
            sq, vec, sq,
            pl.BlockSpec((len(POOL_WINDOWS), POOL_GROUP, POOL_GROUP), lambda i: (0, 0, 0)),
            pl.BlockSpec((1, D_POOL), const2),
            pl.BlockSpec((D_POOL, D_MODEL), const2, pipeline_mode=once),
            sq, vec, vec, vec,
        ],
        out_specs=rows(D_MODEL),
        out_shape=jax.ShapeDtypeStruct((n_blocks * TOK_BLOCK, D_MODEL), F32),
        compiler_params=pltpu.CompilerParams(vmem_limit_bytes=VMEM_LIMIT),
        name="mix_out",
    )(xt, v, ub, ub, ub, sg, mods, wglu, bglu, wpa, wpool, pscale, wpb, wout, bout, g1, b1)


def _mlp_kernel(*refs, paths):
    n_paths = len(paths)
    x_ref, mod_ref, w1_hbm, b1_ref, w2_hbm, b2_ref, g2_ref, be2_ref = refs[:8]
    y_hbm = refs[8:8 + n_paths]
    obuf, sem, w1_ref, w2_ref, stage1, stage2, wsem = refs[8 + n_paths:]
    i = pl.program_id(0)
    n = pl.num_programs(0)
    slot = i % 2

    @pl.when(i == 0)
    def _():
        _stage_weight(w1_hbm, w1_ref, stage1, wsem)
        _stage_weight(w2_hbm, w2_ref, stage2, wsem)

    @pl.when(i >= 2)
    def _():
        _wait_block(y_hbm, obuf, sem, slot, paths, to_hbm=True)

    sub_time = TIME_BLOCK // MLP_SLICES
    for sub in range(MLP_SLICES):
        x = x_ref[sub * sub_time * SEQ_TILE:(sub + 1) * sub_time * SEQ_TILE, :]
        h = _modulate(_layer_norm(x), 1.0 + mod_ref[4], mod_ref[3]).astype(BF16)
        f = jnp.zeros(x.shape, F32)
        for k in range(D_FF // D_MODEL):
            cols = slice(k * D_MODEL, (k + 1) * D_MODEL)
            a = jnp.square(jax.nn.relu(_dot(h, w1_ref[:, cols]) + b1_ref[:, cols]))
            f = f + _dot(a.astype(BF16), w2_ref[cols, :])
        y = DEEPNORM_ALPHA * x + _modulate(f + b2_ref[...], mod_ref[5])
        obuf[slot, sub * sub_time:(sub + 1) * sub_time] = (
            _layer_norm(y) * g2_ref[...] + be2_ref[...]).reshape(sub_time, SEQ_TILE, D_MODEL)

    _start_block(y_hbm, obuf, sem, i, slot, paths, to_hbm=True)

    @pl.when(i == n - 1)
    def _():
        @pl.when(n >= 2)
        def _():
            _wait_block(y_hbm, obuf, sem, 1 - slot, paths, to_hbm=True)
        _wait_block(y_hbm, obuf, sem, slot, paths, to_hbm=True)


def _mlp(x1, paths, n_blocks, mods, w1, b1, w2, b2, g2, be2):
    const2 = lambda i: (0, 0)
    vec = pl.BlockSpec((1, D_MODEL), const2)
    w2_chunk_rows = WEIGHT_CHUNK_ROWS * D_FF // D_MODEL
    return pl.pallas_call(
        functools.partial(_mlp_kernel, paths=paths),
        grid=(n_blocks,),
        in_specs=[
            _row_spec(D_MODEL, n_blocks),
            pl.BlockSpec((N_MOD, SEQ_TILE, D_MODEL), lambda i: (0, _mod_block(i, paths), 0)),
            pl.BlockSpec(memory_space=pl.ANY),
            pl.BlockSpec((1, D_FF), const2),
            pl.BlockSpec(memory_space=pl.ANY),
            vec, vec, vec,
        ],
        out_specs=[pl.BlockSpec(memory_space=pl.ANY)] * len(paths),
        out_shape=[jax.ShapeDtypeStruct((p.n_seq, p.seq_len, D_MODEL), F32) for p in paths],
        scratch_shapes=[pltpu.VMEM((2, TIME_BLOCK, SEQ_TILE, D_MODEL), F32),
                        pltpu.SemaphoreType.DMA((2,)),
                        pltpu.VMEM((D_MODEL, D_FF), BF16), pltpu.VMEM((D_FF, D_MODEL), BF16),
                        pltpu.VMEM((2, WEIGHT_CHUNK_ROWS, D_FF), F32),
                        pltpu.VMEM((2, w2_chunk_rows, D_MODEL), F32),
                        pltpu.SemaphoreType.DMA((2,))],
        compiler_params=pltpu.CompilerParams(dimension_semantics=("arbitrary",),
                                             vmem_limit_bytes=VMEM_LIMIT),
        name="mlp",
    )(x1, mods, w1, b1, w2, b2, g2, be2)


def _state_cols(x_f, x_b):
    return jnp.concatenate([x_f, x_b, x_f, x_b], axis=-1)


def kernel(x_prompt, x_sample, state_s5, c, c_ctx, w_ada, b_ada, w_in, b_in, s5_lam_re, s5_lam_im, s5_log_dt, s5_b_re, s5_b_im, s5_c_re, s5_c_im, s5_d, w_glu, b_glu, w_proj_a, w_pool, pool_scale, w_proj_b, w_out, b_out, ln1_g, ln1_b, w_mlp1, b_mlp1, w_mlp2, b_mlp2, ln2_g, ln2_b):
    assert w_in.shape[0] == 1, "single-layer backbone"
    n_ctx = x_prompt.shape[0]
    n_lat = x_sample.shape[0]
    g, p, hh = N_GROUPS, S5_STATE, S5_GROUP
    paths, n_blocks = _paths([(x_prompt.shape, x_prompt.shape[1]), (x_sample.shape, GRID_W)])

    n_vec = 1 + n_lat
    n_rows = -(-n_vec // 8) * 8
    cvec = jnp.concatenate([c_ctx[None, :], c, jnp.zeros((n_rows - n_vec, D_MODEL), F32)], axis=0)
    mods = _mods(cvec, w_ada[0], b_ada[0][None, :])
    mods = jnp.concatenate([jnp.broadcast_to(mods[:, 0:1], (N_MOD, n_ctx, D_MODEL)),
                            mods[:, 1:n_vec]], axis=1)

    dirs = lambda x: _state_cols(x[0], x[1])
    rows = jnp.stack(
        [dirs(s5_lam_re[0]), dirs(s5_lam_im[0]),
         dirs(jnp.broadcast_to(s5_log_dt[0][:, :, None], (2, g, p))),
         jnp.tile(s5_d[0, 0].reshape(g, hh), (1, CHUNK)),
         jnp.tile(s5_d[0, 1].reshape(g, hh), (1, CHUNK))], axis=1)
    mats = jnp.stack(
        [dirs(s5_b_re[0].transpose(0, 1, 3, 2)), dirs(s5_b_im[0].transpose(0, 1, 3, 2)),
         dirs(s5_c_re[0]), dirs(s5_c_im[0])], axis=1)
    mt_mat, wb_mat, wct_mat, lam16 = _s5_prep(rows, mats)

    xt, ua, ub, sg = _in_proj((x_prompt, x_sample), paths, n_blocks, mods, w_in[0],
                              b_in[0][None, :])

    st = state_s5[:, 0].astype(F32)
    s0 = jnp.concatenate([st[:, 0], st[:, 1]], axis=-1).transpose(2, 1, 0, 3)
    v, fin = _s5_core(ua, paths, mt_mat, wb_mat, wct_mat, lam16, s0)
    new_state = fin.reshape(n_ctx, 1, 2, 2, g, p)

    x1 = _mix_out(xt, v, ub, sg, paths, n_blocks, mods,
                  w_glu[0].astype(BF16), b_glu[0][None, :], w_proj_a[0].astype(BF16),
                  w_pool[0].astype(BF16), pool_scale[0][None, :], w_proj_b[0].astype(BF16),
                  w_out[0].astype(BF16), b_out[0][None, :], ln1_g[0][None, :], ln1_b[0][None, :])
    y_p, y_s = _mlp(x1, paths, n_blocks, mods,
                    w_mlp1[0], b_mlp1[0][None, :], w_mlp2[0],
                    b_mlp2[0][None, :], ln2_g[0][None, :], ln2_b[0][None, :])
    return (y_p, y_s, new_state)
```

```python
import functools
from typing import NamedTuple

import jax
import jax.numpy as jnp
from jax import lax
from jax.experimental import pallas as pl
from jax.experimental.pallas import tpu as pltpu

F32 = jnp.float32
BF16 = jnp.bfloat16

D_MODEL = 1024
S5_GROUP = 16
N_GROUPS = D_MODEL // S5_GROUP
S5_STATE = 64
D_POOL = D_MODEL // 2
POOL_WINDOWS = (2, 4, 8, 16)
POOL_GROUP = D_POOL // len(POOL_WINDOWS)
D_IN = D_MODEL + D_POOL + 2 * D_MODEL
D_REST = D_IN - D_MODEL
D_FF = 4 * D_MODEL
N_MOD = 6
GRID_W = 64
DEEPNORM_ALPHA = 2.0 ** 0.25
LN_EPS = 1e-6

CHUNK = 16
CW = CHUNK * S5_GROUP
SW = 4 * S5_STATE
LANES = 128
N_STRIPS = D_MODEL // LANES
GROUPS_PER_STRIP = LANES // S5_GROUP
GROUP_UNROLL = 4
N_POW = 32
SEQ_TILE = 8
TIME_BLOCK = 128
TOK_BLOCK = SEQ_TILE * TIME_BLOCK
HALO = max(POOL_WINDOWS) // 2
IN_SLICES = 4
MLP_SLICES = 4
MIX_SLICES = 4
WEIGHT_CHUNK_ROWS = 64
WEIGHT_SLOTS = 4
VMEM_LIMIT = 56 * 1024 * 1024


def _time_major(x3):
    return jnp.swapaxes(x3, 0, 1).reshape(x3.shape[1] * SEQ_TILE, x3.shape[-1])


def _seq_major(x2):
    return jnp.swapaxes(x2.reshape(x2.shape[0] // SEQ_TILE, SEQ_TILE, x2.shape[-1]), 0, 1)


def _modulate(a, mul, add=None):
    a3 = a.reshape(a.shape[0] // SEQ_TILE, SEQ_TILE, a.shape[-1]) * mul[None]
    if add is not None:
        a3 = a3 + add[None]
    return a3.reshape(a.shape)


def _layer_norm(x):
    mu = jnp.mean(x, axis=-1, keepdims=True)
    xc = x - mu
    var = jnp.mean(xc * xc, axis=-1, keepdims=True)
    return xc * lax.rsqrt(var + LN_EPS)


def _sigmoid(x):
    return 0.5 * jnp.tanh(0.5 * x) + 0.5


def _dot(a, b):
    return jnp.dot(a, b, preferred_element_type=F32)


def _dot_nt(a, b, precision=None):
    return lax.dot_general(a, b, (((1,), (1,)), ((), ())), precision=precision,
                           preferred_element_type=F32)


def _mods_kernel(c_ref, w_ref, b_ref, o_ref):
    s = jax.nn.silu(c_ref[...])
    o_ref[0] = jnp.dot(s, w_ref[...], precision=lax.Precision.HIGHEST,
                       preferred_element_type=F32) + b_ref[...]


def _mods(cvec, w_ada, b_ada):
    rows = cvec.shape[0]
    n_out = w_ada.shape[1]
    return pl.pallas_call(
        _mods_kernel,
        grid=(n_out // D_MODEL,),
        in_specs=[
            pl.BlockSpec((rows, D_MODEL), lambda j: (0, 0)),
            pl.BlockSpec((D_MODEL, D_MODEL), lambda j: (0, j)),
            pl.BlockSpec((1, D_MODEL), lambda j: (0, j)),
        ],
        out_specs=pl.BlockSpec((1, rows, D_MODEL), lambda j: (j, 0, 0)),
        out_shape=jax.ShapeDtypeStruct((n_out // D_MODEL, rows, D_MODEL), F32),
        name="mods",
    )(cvec, w_ada, b_ada)


def _s5_prep_kernel(rows_ref, mats_ref, mt_ref, wb_ref, wct_ref, lam_ref,
                    pwr_ref, pwi_ref, ge_ref):
    pg = GROUPS_PER_STRIP
    lre = rows_ref[:, 0:1, :]
    lim = rows_ref[:, 1:2, :]
    dt = jnp.exp(rows_ref[:, 2:3, :])
    a = lre * dt
    b = lim * dt
    col = lax.broadcasted_iota(jnp.int32, (1, 1, SW), 2)
    is_im = col >= 2 * S5_STATE
    is_b = (col & (2 * S5_STATE - 1)) >= S5_STATE

    mag = jnp.exp(a)
    sq_r = mag * jnp.cos(b)
    sq_i = mag * jnp.sin(b)
    pwr_ref[:, 0:1, :] = jnp.ones((pg, 1, SW), F32)
    pwi_ref[:, 0:1, :] = jnp.zeros((pg, 1, SW), F32)
    m = 1
    while m < N_POW:
        lo_r = pwr_ref[:, 0:m, :]
        lo_i = pwi_ref[:, 0:m, :]
        pwr_ref[:, m:2 * m, :] = lo_r * sq_r - lo_i * sq_i
        pwi_ref[:, m:2 * m, :] = lo_r * sq_i + lo_i * sq_r
        sq_r, sq_i = sq_r * sq_r - sq_i * sq_i, 2.0 * sq_r * sq_i
        m *= 2

    def power(kf, kb):
        def row(ref, kk):
            return jnp.zeros((pg, 1, SW), F32) if kk is None else ref[:, kk:kk + 1, :]
        if kf == kb:
            return row(pwr_ref, kf), row(pwi_ref, kf)
        return (jnp.where(is_b, row(pwr_ref, kb), row(pwr_ref, kf)),
                jnp.where(is_b, row(pwi_ref, kb), row(pwi_ref, kf)))

    lbr = pwr_ref[:, 1:2, :]
    lbi = pwi_ref[:, 1:2, :]
    den = lre * lre + lim * lim
    nr = lbr - 1.0
    cr = (nr * lre + lbi * lim) / den
    ci = (lbi * lre - nr * lim) / den
    bre = mats_ref[:, 0]
    bim = mats_ref[:, 1]
    bbr = cr * bre - ci * bim
    bbi = cr * bim + ci * bre
    bx = jnp.where(is_im, bbi, bbr)
    by = jnp.where(is_im, bbr, -bbi)
    cre = mats_ref[:, 2]
    cim = mats_ref[:, 3]
    cx = jnp.where(is_im, -cim, cre)
    cy = jnp.where(is_im, -cre, -cim)

    for t in range(CHUNK):
        rows = slice(t * S5_GROUP, (t + 1) * S5_GROUP)
        pr, pi = power(CHUNK - 1 - t, t)
        wb_ref[:, rows, :] = (pr * bx + pi * by).astype(BF16)
        pr, pi = power(t + 1, CHUNK - t)
        wct_ref[:, rows, :] = (pr * cx + pi * cy).astype(BF16)

    for j in range(2 * CHUNK):
        rows = slice(j * S5_GROUP, (j + 1) * S5_GROUP)
        if j == 2 * CHUNK - 1:
            ge_ref[:, rows, :] = jnp.zeros((pg, S5_GROUP, SW), F32)
            continue
        pr, pi = power(j - (CHUNK - 1) if j >= CHUNK - 1 else None,
                       (CHUNK - 1) - j if j <= CHUNK - 1 else None)
        ge_ref[:, rows, :] = pr * cx + pi * cy

    dsum = rows_ref[:, 3:4, :] + rows_ref[:, 4:5, :]
    r16 = lax.broadcasted_iota(jnp.int32, (S5_GROUP, CW), 0)
    c16 = lax.broadcasted_iota(jnp.int32, (S5_GROUP, CW), 1)
    for gi in range(pg):
        e = _dot_nt(bx[gi], ge_ref[gi], precision=lax.Precision.HIGHEST)
        blocks = []
        for tau in range(CHUNK):
            start = (CHUNK - 1 - tau) * S5_GROUP
            blocks.append(e[:, start:start + CW]
                          + jnp.where(c16 == r16 + tau * S5_GROUP, dsum[gi], 0.0))
        mt_ref[gi] = jnp.concatenate(blocks, axis=0).T.astype(BF16)

    lam_ref[...] = jnp.concatenate(
        [pwr_ref[:, CHUNK:CHUNK + 1, :], pwi_ref[:, CHUNK:CHUNK + 1, :],
         jnp.zeros((pg, 6, SW), F32)], axis=1)


def _s5_prep(rows, mats):
    g = N_GROUPS
    pg = GROUPS_PER_STRIP
    mat_out = pl.BlockSpec((pg, CW, SW), lambda i: (i, 0, 0))
    return pl.pallas_call(
        _s5_prep_kernel,
        grid=(g // pg,),
        in_specs=[pl.BlockSpec((pg,) + rows.shape[1:], lambda i: (i, 0, 0)),
                  pl.BlockSpec((pg,) + mats.shape[1:], lambda i: (i, 0, 0, 0))],
        out_specs=[mat_out, mat_out, mat_out, pl.BlockSpec((pg, 8, SW), lambda i: (i, 0, 0))],
        out_shape=[jax.ShapeDtypeStruct((g, CW, CW), BF16),
                   jax.ShapeDtypeStruct((g, CW, SW), BF16),
                   jax.ShapeDtypeStruct((g, CW, SW), BF16),
                   jax.ShapeDtypeStruct((g, 8, SW), F32)],
        scratch_shapes=[pltpu.VMEM((pg, N_POW, SW), F32), pltpu.VMEM((pg, N_POW, SW), F32),
                        pltpu.VMEM((pg, 2 * CW, SW), F32)],
        name="s5_prep",
    )(rows, mats)


class _Path(NamedTuple):
    n_seq: int
    seq_len: int
    pool_n: int
    first_block: int

    @property
    def time_blocks(self):
        return self.seq_len // TIME_BLOCK

    @property
    def n_blocks(self):
        return (self.n_seq // SEQ_TILE) * self.time_blocks


def _paths(shapes_and_pool):
    paths, first = [], 0
    for (n_seq, seq_len, _), pool_n in shapes_and_pool:
        assert n_seq % SEQ_TILE == 0 and seq_len % TIME_BLOCK == 0
        sub_time = TIME_BLOCK // MIX_SLICES
        assert pool_n % sub_time == 0 and seq_len % pool_n == 0 and HALO <= sub_time
        assert pool_n & (pool_n - 1) == 0
        paths.append(_Path(n_seq, seq_len, pool_n, first))
        first += paths[-1].n_blocks
    return tuple(paths), first


def _block_coords(i, paths):
    coords = []
    for k, p in enumerate(paths):
        end = p.first_block + p.n_blocks
        inside = (i >= p.first_block) & (i < end)
        j = jnp.clip(i - p.first_block, 0, p.n_blocks - 1)
        coords.append((inside, (j // p.time_blocks) * SEQ_TILE, (j % p.time_blocks) * TIME_BLOCK))
    return coords


def _mod_block(i, paths):
    blk, first_mod = 0, 0
    for p in paths:
        j = jnp.clip(i - p.first_block, 0, p.n_blocks - 1)
        blk = jnp.where(i >= p.first_block, first_mod + j // p.time_blocks, blk)
        first_mod += p.n_seq // SEQ_TILE
    return blk


def _row_spec(width, n_blocks, shift=0):
    return pl.BlockSpec((TOK_BLOCK, width),
                        lambda i: (jnp.clip(i + shift, 0, n_blocks - 1), 0))


def _block_copies(hbm_refs, buf, sem, i, slot, paths, to_hbm):
    for (inside, seq0, t0), hbm in zip(_block_coords(i, paths), hbm_refs):
        copies = []
        for s in range(SEQ_TILE):
            rows = hbm.at[seq0 + s, pl.ds(t0, TIME_BLOCK), :]
            tile_rows = buf.at[slot, :, s, :]
            src, dst = (tile_rows, rows) if to_hbm else (rows, tile_rows)
            copies.append(pltpu.make_async_copy(src, dst, sem.at[slot]))
        yield inside, copies


def _start_block(hbm_refs, buf, sem, i, slot, paths, to_hbm):
    for inside, copies in _block_copies(hbm_refs, buf, sem, i, slot, paths, to_hbm):
        @pl.when(inside)
        def _():
            for c in copies:
                c.start()


def _wait_block(hbm_refs, buf, sem, slot, paths, to_hbm):
    _, copies = next(_block_copies(hbm_refs, buf, sem, 0, slot, paths, to_hbm))
    for c in copies:
        c.wait()


def _stage_weight(w_hbm, w_ref, stage, sem):
    n_slots, rows = stage.shape[0], stage.shape[1]
    n_chunks = w_hbm.shape[0] // rows

    def chunk_copy(c):
        return pltpu.make_async_copy(w_hbm.at[pl.ds(c * rows, rows), :], stage.at[c % n_slots],
                                     sem.at[c % n_slots])

    for c in range(min(n_slots - 1, n_chunks)):
        chunk_copy(c).start()
    for c in range(n_chunks):
        if c + n_slots - 1 < n_chunks:
            chunk_copy(c + n_slots - 1).start()
        chunk_copy(c).wait()
        w_ref[c * rows:(c + 1) * rows, :] = stage[c % n_slots].astype(BF16)


def _in_proj_kernel(*refs, paths):
    n_paths = len(paths)
    x_hbm = refs[:n_paths]
    (mod_ref, w_hbm, b_ref, xt_ref, ua_ref, ub_ref, sg_ref,
     xbuf, sem, w_ref, wstage, wsem) = refs[n_paths:]
    i = pl.program_id(0)
    n = pl.num_programs(0)
    slot = i % 2

    @pl.when(i == 0)
    def _():
        _start_block(x_hbm, xbuf, sem, i, slot, paths, to_hbm=False)
        _stage_weight(w_hbm, w_ref, wstage, wsem)

    @pl.when(i + 1 < n)
    def _():
        _start_block(x_hbm, xbuf, sem, i + 1, 1 - slot, paths, to_hbm=False)

    _wait_block(x_hbm, xbuf, sem, slot, paths, to_hbm=False)
    sub_time = TIME_BLOCK // IN_SLICES
    for sub in range(IN_SLICES):
        times = slice(sub * sub_time, (sub + 1) * sub_time)
        rows = slice(sub * sub_time * SEQ_TILE, (sub + 1) * sub_time * SEQ_TILE)
        x = xbuf[slot, times].reshape(sub_time * SEQ_TILE, D_MODEL)
        xt_ref[rows, :] = x
        h = _modulate(_layer_norm(x), 1.0 + mod_ref[1], mod_ref[0])
        z = _dot(h.astype(BF16), w_ref[...]) + b_ref[...]
        ua_ref[times] = z[:, :D_MODEL].reshape(sub_time, SEQ_TILE, D_MODEL)
        ub_ref[rows, :] = z[:, D_MODEL:D_MODEL + D_POOL].astype(BF16)
        sg_ref[rows, :] = _sigmoid(z[:, D_MODEL + D_POOL:]).astype(BF16)


def _in_proj(xs, paths, n_blocks, mods, w_in, b_in):
    n_tok = n_blocks * TOK_BLOCK
    const = lambda i: (0, 0)
    return pl.pallas_call(
        functools.partial(_in_proj_kernel, paths=paths),
        grid=(n_blocks,),
        in_specs=[pl.BlockSpec(memory_space=pl.ANY)] * len(xs) + [
            pl.BlockSpec((N_MOD, SEQ_TILE, D_MODEL), lambda i: (0, _mod_block(i, paths), 0)),
            pl.BlockSpec(memory_space=pl.ANY),
            pl.BlockSpec((1, D_IN), const),
        ],
        out_specs=[_row_spec(D_MODEL, n_blocks),
                   pl.BlockSpec((TIME_BLOCK, SEQ_TILE, D_MODEL), lambda i: (i, 0, 0)),
                   _row_spec(D_POOL, n_blocks), _row_spec(2 * D_MODEL, n_blocks)],
        out_shape=[jax.ShapeDtypeStruct((n_tok, D_MODEL), F32),
                   jax.ShapeDtypeStruct((n_blocks * TIME_BLOCK, SEQ_TILE, D_MODEL), F32),
                   jax.ShapeDtypeStruct((n_tok, D_POOL), BF16),
                   jax.ShapeDtypeStruct((n_tok, 2 * D_MODEL), BF16)],
        scratch_shapes=[pltpu.VMEM((2, TIME_BLOCK, SEQ_TILE, D_MODEL), F32),
                        pltpu.SemaphoreType.DMA((2,)),
                        pltpu.VMEM((D_MODEL, D_IN), BF16),
                        pltpu.VMEM((WEIGHT_SLOTS, WEIGHT_CHUNK_ROWS, D_IN), F32),
                        pltpu.SemaphoreType.DMA((WEIGHT_SLOTS,))],
        compiler_params=pltpu.CompilerParams(dimension_semantics=("arbitrary",),
                                             vmem_limit_bytes=VMEM_LIMIT),
        name="in_proj",
    )(*xs, mods, w_in, b_in)


def _s5_core_kernel(ua_ref, mt_ref, wb_ref, wct_ref, lam_ref, s0_ref, v_ref, fin_ref,
                    xt_ref, yt_ref, sloc_ref, sprf_ref, sprb_ref, fsc_ref, *, paths):
    for k, p in enumerate(paths):
        @pl.when(pl.program_id(1) == k)
        def _():
            _s5_strip(ua_ref, mt_ref, wb_ref, wct_ref, lam_ref, s0_ref, v_ref, fin_ref,
                      xt_ref, yt_ref, sloc_ref, sprf_ref, sprb_ref, fsc_ref,
                      n_seq=p.n_seq, seq_len=p.seq_len, has_init=k > 0)


def _s5_strip(ua_ref, mt_ref, wb_ref, wct_ref, lam_ref, s0_ref, v_ref, fin_ref,
              xt_ref, yt_ref, sloc_ref, sprf_ref, sprb_ref, fsc_ref, *, n_seq, seq_len, has_init):
    n_chunk = seq_len // CHUNK
    n_sb = n_seq // SEQ_TILE
    half = 2 * S5_STATE

    for tau in range(CHUNK):
        x_tau = jnp.concatenate([ua_ref[sb * seq_len + c * CHUNK + tau]
                                 for c in range(n_chunk) for sb in range(n_sb)], axis=0)
        xt = x_tau.astype(BF16).T
        for gi in range(GROUPS_PER_STRIP):
            xt_ref[gi, tau * S5_GROUP:(tau + 1) * S5_GROUP, :] = xt[gi * S5_GROUP:(gi + 1) * S5_GROUP, :]

    is_f = lax.broadcasted_iota(jnp.int32, (1, half), 1) < S5_STATE
    col = lax.broadcasted_iota(jnp.int32, (1, SW), 1)
    col_is_f = (col & (half - 1)) < S5_STATE

    def one_group(gi, slot):
        xg = xt_ref[gi]
        u = xg.T
        sloc_ref[slot] = _dot(u, wb_ref[gi])
        ar = lam_ref[gi, 0:1, 0:half]
        ai = lam_ref[gi, 1:2, 0:half]
        if has_init:
            s_re = s0_ref[gi, 0]
            s_im = s0_ref[gi, 1]
        else:
            s_re = jnp.zeros((n_seq, half), F32)
            s_im = jnp.zeros((n_seq, half), F32)
        for c in range(n_chunk):
            rf = pl.ds(c * n_seq, n_seq)
            rb = pl.ds((n_chunk - 1 - c) * n_seq, n_seq)
            sprf_ref[slot, rf, 0:half] = s_re
            sprf_ref[slot, rf, half:SW] = s_im
            sprb_ref[slot, rb, 0:half] = s_re
            sprb_ref[slot, rb, half:SW] = s_im
            l_re = jnp.where(is_f, sloc_ref[slot, rf, 0:half], sloc_ref[slot, rb, 0:half])
            l_im = jnp.where(is_f, sloc_ref[slot, rf, half:SW], sloc_ref[slot, rb, half:SW])
            s_re, s_im = (ar * s_re - ai * s_im + l_re,
                          ar * s_im + ai * s_re + l_im)
        if not has_init:
            fsc_ref[gi, 0] = s_re
            fsc_ref[gi, 1] = s_im
        sprev = jnp.where(col_is_f, sprf_ref[slot], sprb_ref[slot]).astype(BF16)
        yt = _dot(mt_ref[gi], xg) + _dot_nt(wct_ref[gi], sprev)
        yt_ref[gi] = jax.nn.gelu(yt)

    def group_body(i, carry):
        for slot in range(GROUP_UNROLL):
            one_group(i * GROUP_UNROLL + slot, slot)
        return carry

    lax.fori_loop(0, GROUPS_PER_STRIP // GROUP_UNROLL, group_body, 0)

    if not has_init:
        for part in range(2):
            fs = jnp.swapaxes(fsc_ref[:, part], 0, 1)
            fin_ref[:, part] = fs[:, :, 0:S5_STATE]
            fin_ref[:, 2 + part] = fs[:, :, S5_STATE:half]

    for t in range(CHUNK):
        vt = yt_ref[:, t * S5_GROUP:(t + 1) * S5_GROUP, :].reshape(LANES, n_seq * n_chunk)
        v_t = vt.T
        for c in range(n_chunk):
            for sb in range(n_sb):
                r0 = (c * n_sb + sb) * SEQ_TILE
                v_ref[sb * seq_len + c * CHUNK + t] = v_t[r0:r0 + SEQ_TILE, :]


def _s5_core(ua, paths, m, wb, wct, lam, s0):
    ctx, lat = paths
    tiles = ctx.n_blocks * TIME_BLOCK
    rows = tiles * SEQ_TILE // CHUNK
    assert lat.n_blocks * TIME_BLOCK == tiles and s0.shape[2] == lat.n_seq
    gps = GROUPS_PER_STRIP
    strip = pl.BlockSpec((tiles, SEQ_TILE, LANES), lambda j, k: (k, 0, j))
    mat = pl.BlockSpec((gps, CW, SW), lambda j, k: (j, 0, 0))
    return pl.pallas_call(
        functools.partial(_s5_core_kernel, paths=paths),
        grid=(N_STRIPS, len(paths)),
        in_specs=[strip, mat, mat, mat, pl.BlockSpec((gps, 8, SW), lambda j, k: (j, 0, 0)),
                  pl.BlockSpec((gps, 2, lat.n_seq, 2 * S5_STATE), lambda j, k: (j, 0, 0, 0))],
        out_specs=[strip,
                   pl.BlockSpec((ctx.n_seq, 4, gps, S5_STATE), lambda j, k: (0, 0, j, 0))],
        out_shape=[jax.ShapeDtypeStruct(ua.shape, F32),
                   jax.ShapeDtypeStruct((ctx.n_seq, 4, N_GROUPS, S5_STATE), F32)],
        scratch_shapes=[pltpu.VMEM((gps, CW, rows), BF16), pltpu.VMEM((gps, CW, rows), F32),
                        pltpu.VMEM((GROUP_UNROLL, rows, SW), F32),
                        pltpu.VMEM((GROUP_UNROLL, rows, SW), F32),
                        pltpu.VMEM((GROUP_UNROLL, rows, SW), F32),
                        pltpu.VMEM((gps, 2, ctx.n_seq, 2 * S5_STATE), F32)],
        compiler_params=pltpu.CompilerParams(dimension_semantics=("arbitrary", "arbitrary"),
                                             vmem_limit_bytes=VMEM_LIMIT),
        name="s5_core",
    )(ua, m, wb, wct, lam, s0)


def _mix_out_kernel(xt_ref, v_ref, ubp_ref, ub_ref, ubn_ref, sg_ref, mod_ref, wglu_ref, bglu_ref,
                    wpa_ref, wpool_ref, pscale_ref, wpb_ref, wout_ref, bout_ref, g1_ref, b1_ref,
                    o_ref, *, paths):
    i = pl.program_id(0)
    t_block, run_mask = 0, 0
    for p in paths:
        j = jnp.clip(i - p.first_block, 0, p.n_blocks - 1)
        t_block = jnp.where(i >= p.first_block, (j % p.time_blocks) * TIME_BLOCK, t_block)
        run_mask = jnp.where(i >= p.first_block, p.pool_n - 1, run_mask)
    halo_rows = HALO * SEQ_TILE
    ext = jnp.concatenate([ubp_ref[...], ub_ref[...], ubn_ref[...]], axis=0).astype(F32)

    sub_rows = TOK_BLOCK // MIX_SLICES
    sub_time = TIME_BLOCK // MIX_SLICES
    slices = [slice(sub * sub_rows, (sub + 1) * sub_rows) for sub in range(MIX_SLICES)]

    def s5_branch(sub):
        tiles = sub_rows // SEQ_TILE
        v = v_ref[sub * tiles:(sub + 1) * tiles].reshape(sub_rows, D_MODEL)
        glu = v * _sigmoid(_dot(v.astype(BF16), wglu_ref[...]) + bglu_ref[...])
        return _dot(glu.astype(BF16), wpa_ref[...])

    def pool_branch(sub):
        pos = (t_block + sub * sub_time) & run_mask
        first = pos == 0
        last = pos + sub_time == run_mask + 1
        base = sub * sub_rows
        win = jnp.concatenate(
            [jnp.where(first, 0.0, ext[base:base + halo_rows]),
             ext[base + halo_rows:base + halo_rows + sub_rows],
             jnp.where(last, 0.0, ext[base + halo_rows + sub_rows:base + 2 * halo_rows + sub_rows])],
            axis=0)
        t_idx = lax.broadcasted_iota(jnp.int32, (sub_rows, POOL_GROUP), 0) // SEQ_TILE
        pooled = []
        for gi, w in enumerate(POOL_WINDOWS):
            cols = slice(gi * POOL_GROUP, (gi + 1) * POOL_GROUP)
            acc = jnp.zeros((sub_rows, POOL_GROUP), F32)
            for k in range(-(w // 2), w - w // 2):
                r0 = halo_rows + k * SEQ_TILE
                acc = acc + win[r0:r0 + sub_rows, cols]
            lo = jnp.where(first, jnp.maximum(t_idx - w // 2, 0), t_idx - w // 2)
            hi = jnp.where(last, jnp.minimum(t_idx - w // 2 + w, sub_time), t_idx - w // 2 + w)
            p = acc / (hi - lo).astype(F32) - win[halo_rows:halo_rows + sub_rows, cols]
            pooled.append(_dot(p.astype(BF16), wpool_ref[gi]))
        pm = jnp.concatenate(pooled, axis=1) * pscale_ref[...]
        return _dot(pm.astype(BF16), wpb_ref[...])

    ya = [s5_branch(sub) for sub in range(MIX_SLICES)]
    yb = [pool_branch(sub) for sub in range(MIX_SLICES)]
    tm = []
    for sub, rows in enumerate(slices):
        merged = (sg_ref[rows, 0:D_MODEL].astype(F32) * ya[sub]
                  + sg_ref[rows, D_MODEL:2 * D_MODEL].astype(F32) * yb[sub])
        tm.append(_dot(merged.astype(BF16), wout_ref[...]) + bout_ref[...])
    for sub, rows in enumerate(slices):
        y = DEEPNORM_ALPHA * xt_ref[rows, :] + _modulate(tm[sub], mod_ref[2])
        o_ref[rows, :] = _layer_norm(y) * g1_ref[...] + b1_ref[...]


def _mix_out(xt, v, ub, sg, paths, n_blocks, mods, wglu, bglu, wpa, wpool, pscale, wpb, wout,
             bout, g1, b1):
    const2 = lambda i: (0, 0)
    once = pl.Buffered(1)
    vec = pl.BlockSpec((1, D_MODEL), const2)
    sq = pl.BlockSpec((D_MODEL, D_MODEL), const2, pipeline_mode=once)
    rows = functools.partial(_row_spec, n_blocks=n_blocks)
    halo_rows = HALO * SEQ_TILE
    halos_per_block = TOK_BLOCK // halo_rows
    n_halo_blocks = n_blocks * halos_per_block
    halo = lambda index: pl.BlockSpec((halo_rows, D_POOL), lambda i: (index(i), 0))
    return pl.pallas_call(
        functools.partial(_mix_out_kernel, paths=paths),
        grid=(n_blocks,),
        in_specs=[
            rows(D_MODEL),
            pl.BlockSpec((TIME_BLOCK, SEQ_TILE, D_MODEL), lambda i: (i, 0, 0)),
            halo(lambda i: jnp.maximum(i * halos_per_block - 1, 0)), rows(D_POOL),
            halo(lambda i: jnp.minimum((i + 1) * halos_per_block, n_halo_blocks - 1)),
            rows(2 * D_MODEL),
            pl.BlockSpec((N_MOD, SEQ_TILE, D_MODEL), lambda i: (0, _mod_block(i, paths), 0)),
            sq, vec, sq,
            pl.BlockSpec((len(POOL_WINDOWS), POOL_GROUP, POOL_GROUP), lambda i: (0, 0, 0)),
            pl.BlockSpec((1, D_POOL), const2),
            pl.BlockSpec((D_POOL, D_MODEL), const2, pipeline_mode=once),
            sq, vec, vec, vec,
        ],
        out_specs=rows(D_MODEL),
        out_shape=jax.ShapeDtypeStruct((n_blocks * TOK_BLOCK, D_MODEL), F32),
        compiler_params=pltpu.CompilerParams(vmem_limit_bytes=VMEM_LIMIT),
        name="mix_out",
    )(xt, v, ub, ub, ub, sg, mods, wglu, bglu, wpa, wpool, pscale, wpb, wout, bout, g1, b1)


def _mlp_kernel(*refs, paths):
    n_paths = len(paths)
    x_ref, mod_ref, w1_hbm, b1_ref, w2_hbm, b2_ref, g2_ref, be2_ref = refs[:8]
    y_hbm = refs[8:8 + n_paths]
    obuf, sem, w1_ref, w2_ref, stage1, stage2, wsem = refs[8 + n_paths:]
    i = pl.program_id(0)
    n = pl.num_programs(0)
    slot = i % 2

    @pl.when(i == 0)
    def _():
        _stage_weight(w1_hbm, w1_ref, stage1, wsem)
        _stage_weight(w2_hbm, w2_ref, stage2, wsem)

    @pl.when(i >= 2)
    def _():
        _wait_block(y_hbm, obuf, sem, slot, paths, to_hbm=True)

    sub_time = TIME_BLOCK // MLP_SLICES
    for sub in range(MLP_SLICES):
        x = x_ref[sub * sub_time * SEQ_TILE:(sub + 1) * sub_time * SEQ_TILE, :]
        h = _modulate(_layer_norm(x), 1.0 + mod_ref[4], mod_ref[3]).astype(BF16)
        f = jnp.zeros(x.shape, F32)
        for k in range(D_FF // D_MODEL):
            cols = slice(k * D_MODEL, (k + 1) * D_MODEL)
            a = jnp.square(jax.nn.relu(_dot(h, w1_ref[:, cols]) + b1_ref[:, cols]))
            f = f + _dot(a.astype(BF16), w2_ref[cols, :])
        y = DEEPNORM_ALPHA * x + _modulate(f + b2_ref[...], mod_ref[5])
        obuf[slot, sub * sub_time:(sub + 1) * sub_time] = (
            _layer_norm(y) * g2_ref[...] + be2_ref[...]).reshape(sub_time, SEQ_TILE, D_MODEL)

    _start_block(y_hbm, obuf, sem, i, slot, paths, to_hbm=True)

    @pl.when(i == n - 1)
    def _():
        @pl.when(n >= 2)
        def _():
            _wait_block(y_hbm, obuf, sem, 1 - slot, paths, to_hbm=True)
        _wait_block(y_hbm, obuf, sem, slot, paths, to_hbm=True)


def _mlp(x1, paths, n_blocks, mods, w1, b1, w2, b2, g2, be2):
    const2 = lambda i: (0, 0)
    vec = pl.BlockSpec((1, D_MODEL), const2)
    w2_chunk_rows = WEIGHT_CHUNK_ROWS * D_FF // D_MODEL
    return pl.pallas_call(
        functools.partial(_mlp_kernel, paths=paths),
        grid=(n_blocks,),
        in_specs=[
            _row_spec(D_MODEL, n_blocks),
            pl.BlockSpec((N_MOD, SEQ_TILE, D_MODEL), lambda i: (0, _mod_block(i, paths), 0)),
            pl.BlockSpec(memory_space=pl.ANY),
            pl.BlockSpec((1, D_FF), const2),
            pl.BlockSpec(memory_space=pl.ANY),
            vec, vec, vec,
        ],
        out_specs=[pl.BlockSpec(memory_space=pl.ANY)] * len(paths),
        out_shape=[jax.ShapeDtypeStruct((p.n_seq, p.seq_len, D_MODEL), F32) for p in paths],
        scratch_shapes=[pltpu.VMEM((2, TIME_BLOCK, SEQ_TILE, D_MODEL), F32),
                        pltpu.SemaphoreType.DMA((2,)),
                        pltpu.VMEM((D_MODEL, D_FF), BF16), pltpu.VMEM((D_FF, D_MODEL), BF16),
                        pltpu.VMEM((WEIGHT_SLOTS, WEIGHT_CHUNK_ROWS, D_FF), F32),
                        pltpu.VMEM((WEIGHT_SLOTS, w2_chunk_rows, D_MODEL), F32),
                        pltpu.SemaphoreType.DMA((WEIGHT_SLOTS,))],
        compiler_params=pltpu.CompilerParams(dimension_semantics=("arbitrary",),
                                             vmem_limit_bytes=VMEM_LIMIT),
        name="mlp",
    )(x1, mods, w1, b1, w2, b2, g2, be2)


def _state_cols(x_f, x_b):
    return jnp.concatenate([x_f, x_b, x_f, x_b], axis=-1)


def kernel(x_prompt, x_sample, state_s5, c, c_ctx, w_ada, b_ada, w_in, b_in, s5_lam_re, s5_lam_im, s5_log_dt, s5_b_re, s5_b_im, s5_c_re, s5_c_im, s5_d, w_glu, b_glu, w_proj_a, w_pool, pool_scale, w_proj_b, w_out, b_out, ln1_g, ln1_b, w_mlp1, b_mlp1, w_mlp2, b_mlp2, ln2_g, ln2_b):
    assert w_in.shape[0] == 1, "single-layer backbone"
    n_ctx = x_prompt.shape[0]
    n_lat = x_sample.shape[0]
    g, p, hh = N_GROUPS, S5_STATE, S5_GROUP
    paths, n_blocks = _paths([(x_prompt.shape, x_prompt.shape[1]), (x_sample.shape, GRID_W)])

    n_vec = 1 + n_lat
    n_rows = -(-n_vec // 8) * 8
    cvec = jnp.concatenate([c_ctx[None, :], c, jnp.zeros((n_rows - n_vec, D_MODEL), F32)], axis=0)
    mods = _mods(cvec, w_ada[0], b_ada[0][None, :])
    mods = jnp.concatenate([jnp.broadcast_to(mods[:, 0:1], (N_MOD, n_ctx, D_MODEL)),
                            mods[:, 1:n_vec]], axis=1)

    dirs = lambda x: _state_cols(x[0], x[1])
    rows = jnp.stack(
        [dirs(s5_lam_re[0]), dirs(s5_lam_im[0]),
         dirs(jnp.broadcast_to(s5_log_dt[0][:, :, None], (2, g, p))),
         jnp.tile(s5_d[0, 0].reshape(g, hh), (1, CHUNK)),
         jnp.tile(s5_d[0, 1].reshape(g, hh), (1, CHUNK))], axis=1)
    mats = jnp.stack(
        [dirs(s5_b_re[0].transpose(0, 1, 3, 2)), dirs(s5_b_im[0].transpose(0, 1, 3, 2)),
         dirs(s5_c_re[0]), dirs(s5_c_im[0])], axis=1)
    mt_mat, wb_mat, wct_mat, lam16 = _s5_prep(rows, mats)

    xt, ua, ub, sg = _in_proj((x_prompt, x_sample), paths, n_blocks, mods, w_in[0],
                              b_in[0][None, :])

    st = state_s5[:, 0].astype(F32)
    s0 = jnp.concatenate([st[:, 0], st[:, 1]], axis=-1).transpose(2, 1, 0, 3)
    v, fin = _s5_core(ua, paths, mt_mat, wb_mat, wct_mat, lam16, s0)
    new_state = fin.reshape(n_ctx, 1, 2, 2, g, p)

    x1 = _mix_out(xt, v, ub, sg, paths, n_blocks, mods,
                  w_glu[0].astype(BF16), b_glu[0][None, :], w_proj_a[0].astype(BF16),
                  w_pool[0].astype(BF16), pool_scale[0][None, :], w_proj_b[0].astype(BF16),
                  w_out[0].astype(BF16), b_out[0][None, :], ln1_g[0][None, :], ln1_b[0][None, :])
    y_p, y_s = _mlp(x1, paths, n_blocks, mods,
                    w_mlp1[0], b_mlp1[0][None, :], w_mlp2[0],
                    b_mlp2[0][None, :], ln2_g[0][None, :], ln2_b[0][None, :])
    return (y_p, y_s, new_state)
```

```python
import functools
from typing import NamedTuple

import jax
import jax.numpy as jnp
from jax import lax
from jax.experimental import pallas as pl
from jax.experimental.pallas import tpu as pltpu

F32 = jnp.float32
BF16 = jnp.bfloat16

D_MODEL = 1024
S5_GROUP = 16
N_GROUPS = D_MODEL // S5_GROUP
S5_STATE = 64
D_POOL = D_MODEL // 2
POOL_WINDOWS = (2, 4, 8, 16)
POOL_GROUP = D_POOL // len(POOL_WINDOWS)
D_IN = D_MODEL + D_POOL + 2 * D_MODEL
D_REST = D_IN - D_MODEL
D_FF = 4 * D_MODEL
N_MOD = 6
GRID_W = 64
DEEPNORM_ALPHA = 2.0 ** 0.25
LN_EPS = 1e-6

CHUNK = 16
CW = CHUNK * S5_GROUP
SW = 4 * S5_STATE
LANES = 128
N_STRIPS = D_MODEL // LANES
GROUPS_PER_STRIP = LANES // S5_GROUP
GROUP_UNROLL = 4
N_POW = 32
SEQ_TILE = 8
TIME_BLOCK = 128
TOK_BLOCK = SEQ_TILE * TIME_BLOCK
HALO = max(POOL_WINDOWS) // 2
IN_SLICES = 4
MLP_SLICES = 4
MIX_SLICES = 4
VMEM_LIMIT = 56 * 1024 * 1024


def _time_major(x3):
    return jnp.swapaxes(x3, 0, 1).reshape(x3.shape[1] * SEQ_TILE, x3.shape[-1])


def _seq_major(x2):
    return jnp.swapaxes(x2.reshape(x2.shape[0] // SEQ_TILE, SEQ_TILE, x2.shape[-1]), 0, 1)


def _modulate(a, mul, add=None):
    a3 = a.reshape(a.shape[0] // SEQ_TILE, SEQ_TILE, a.shape[-1]) * mul[None]
    if add is not None:
        a3 = a3 + add[None]
    return a3.reshape(a.shape)


def _layer_norm(x):
    mu = jnp.mean(x, axis=-1, keepdims=True)
    xc = x - mu
    var = jnp.mean(xc * xc, axis=-1, keepdims=True)
    return xc * lax.rsqrt(var + LN_EPS)


def _sigmoid(x):
    return 0.5 * jnp.tanh(0.5 * x) + 0.5


def _dot(a, b):
    return jnp.dot(a, b, preferred_element_type=F32)


def _dot_nt(a, b, precision=None):
    return lax.dot_general(a, b, (((1,), (1,)), ((), ())), precision=precision,
                           preferred_element_type=F32)


def _mods_kernel(c_ref, w_ref, b_ref, o_ref):
    s = jax.nn.silu(c_ref[...])
    o_ref[0] = jnp.dot(s, w_ref[...], precision=lax.Precision.HIGHEST,
                       preferred_element_type=F32) + b_ref[...]


def _mods(cvec, w_ada, b_ada):
    rows = cvec.shape[0]
    n_out = w_ada.shape[1]
    return pl.pallas_call(
        _mods_kernel,
        grid=(n_out // D_MODEL,),
        in_specs=[
            pl.BlockSpec((rows, D_MODEL), lambda j: (0, 0)),
            pl.BlockSpec((D_MODEL, D_MODEL), lambda j: (0, j)),
            pl.BlockSpec((1, D_MODEL), lambda j: (0, j)),
        ],
        out_specs=pl.BlockSpec((1, rows, D_MODEL), lambda j: (j, 0, 0)),
        out_shape=jax.ShapeDtypeStruct((n_out // D_MODEL, rows, D_MODEL), F32),
        name="mods",
    )(cvec, w_ada, b_ada)


def _cast_specs(weights, n_steps, step_index):
    specs, shapes = [], []
    for w in weights:
        rows = w.shape[0] // n_steps
        assert rows * n_steps == w.shape[0] and rows % 16 == 0
        specs.append(pl.BlockSpec((rows, w.shape[1]), lambda *g: (step_index(*g), 0)))
        shapes.append(jax.ShapeDtypeStruct(w.shape, BF16))
    return specs, shapes


def _cast_chunks(src_refs, dst_refs):
    for src, dst in zip(src_refs, dst_refs):
        dst[...] = src[...].astype(BF16)


def _s5_prep_kernel(*refs, n_cast):
    rows_ref, mats_ref = refs[:2]
    mt_ref, wb_ref, wct_ref, lam_ref = refs[2 + n_cast:6 + n_cast]
    _cast_chunks(refs[2:2 + n_cast], refs[6 + n_cast:6 + 2 * n_cast])
    _s5_prep_body(rows_ref, mats_ref, mt_ref, wb_ref, wct_ref, lam_ref, *refs[6 + 2 * n_cast:])


def _s5_prep_body(rows_ref, mats_ref, mt_ref, wb_ref, wct_ref, lam_ref,
                  pwr_ref, pwi_ref, ge_ref):
    pg = GROUPS_PER_STRIP
    lre = rows_ref[:, 0:1, :]
    lim = rows_ref[:, 1:2, :]
    dt = jnp.exp(rows_ref[:, 2:3, :])
    a = lre * dt
    b = lim * dt
    col = lax.broadcasted_iota(jnp.int32, (1, 1, SW), 2)
    is_im = col >= 2 * S5_STATE
    is_b = (col & (2 * S5_STATE - 1)) >= S5_STATE

    mag = jnp.exp(a)
    sq_r = mag * jnp.cos(b)
    sq_i = mag * jnp.sin(b)
    pwr_ref[:, 0:1, :] = jnp.ones((pg, 1, SW), F32)
    pwi_ref[:, 0:1, :] = jnp.zeros((pg, 1, SW), F32)
    m = 1
    while m < N_POW:
        lo_r = pwr_ref[:, 0:m, :]
        lo_i = pwi_ref[:, 0:m, :]
        pwr_ref[:, m:2 * m, :] = lo_r * sq_r - lo_i * sq_i
        pwi_ref[:, m:2 * m, :] = lo_r * sq_i + lo_i * sq_r
        sq_r, sq_i = sq_r * sq_r - sq_i * sq_i, 2.0 * sq_r * sq_i
        m *= 2

    def power(kf, kb):
        def row(ref, kk):
            return jnp.zeros((pg, 1, SW), F32) if kk is None else ref[:, kk:kk + 1, :]
        if kf == kb:
            return row(pwr_ref, kf), row(pwi_ref, kf)
        return (jnp.where(is_b, row(pwr_ref, kb), row(pwr_ref, kf)),
                jnp.where(is_b, row(pwi_ref, kb), row(pwi_ref, kf)))

    lbr = pwr_ref[:, 1:2, :]
    lbi = pwi_ref[:, 1:2, :]
    den = lre * lre + lim * lim
    nr = lbr - 1.0
    cr = (nr * lre + lbi * lim) / den
    ci = (lbi * lre - nr * lim) / den
    bre = mats_ref[:, 0]
    bim = mats_ref[:, 1]
    bbr = cr * bre - ci * bim
    bbi = cr * bim + ci * bre
    bx = jnp.where(is_im, bbi, bbr)
    by = jnp.where(is_im, bbr, -bbi)
    cre = mats_ref[:, 2]
    cim = mats_ref[:, 3]
    cx = jnp.where(is_im, -cim, cre)
    cy = jnp.where(is_im, -cre, -cim)

    for t in range(CHUNK):
        rows = slice(t * S5_GROUP, (t + 1) * S5_GROUP)
        pr, pi = power(CHUNK - 1 - t, t)
        wb_ref[:, rows, :] = (pr * bx + pi * by).astype(BF16)
        pr, pi = power(t + 1, CHUNK - t)
        wct_ref[:, rows, :] = (pr * cx + pi * cy).astype(BF16)

    for j in range(2 * CHUNK):
        rows = slice(j * S5_GROUP, (j + 1) * S5_GROUP)
        if j == 2 * CHUNK - 1:
            ge_ref[:, rows, :] = jnp.zeros((pg, S5_GROUP, SW), F32)
            continue
        pr, pi = power(j - (CHUNK - 1) if j >= CHUNK - 1 else None,
                       (CHUNK - 1) - j if j <= CHUNK - 1 else None)
        ge_ref[:, rows, :] = pr * cx + pi * cy

    dsum = rows_ref[:, 3:4, :] + rows_ref[:, 4:5, :]
    r16 = lax.broadcasted_iota(jnp.int32, (S5_GROUP, CW), 0)
    c16 = lax.broadcasted_iota(jnp.int32, (S5_GROUP, CW), 1)
    for gi in range(pg):
        e = _dot_nt(bx[gi], ge_ref[gi], precision=lax.Precision.HIGHEST)
        blocks = []
        for tau in range(CHUNK):
            start = (CHUNK - 1 - tau) * S5_GROUP
            blocks.append(e[:, start:start + CW]
                          + jnp.where(c16 == r16 + tau * S5_GROUP, dsum[gi], 0.0))
        mt_ref[gi] = jnp.concatenate(blocks, axis=0).T.astype(BF16)

    lam_ref[...] = jnp.concatenate(
        [pwr_ref[:, CHUNK:CHUNK + 1, :], pwi_ref[:, CHUNK:CHUNK + 1, :],
         jnp.zeros((pg, 6, SW), F32)], axis=1)


def _s5_prep(rows, mats, cast_weights):
    g = N_GROUPS
    pg = GROUPS_PER_STRIP
    n_steps = g // pg
    mat_out = pl.BlockSpec((pg, CW, SW), lambda i: (i, 0, 0))
    cast_specs, cast_shapes = _cast_specs(cast_weights, n_steps, lambda i: i)
    return pl.pallas_call(
        functools.partial(_s5_prep_kernel, n_cast=len(cast_weights)),
        grid=(n_steps,),
        in_specs=[pl.BlockSpec((pg,) + rows.shape[1:], lambda i: (i, 0, 0)),
                  pl.BlockSpec((pg,) + mats.shape[1:], lambda i: (i, 0, 0, 0))] + cast_specs,
        out_specs=[mat_out, mat_out, mat_out,
                   pl.BlockSpec((pg, 8, SW), lambda i: (i, 0, 0))] + cast_specs,
        out_shape=[jax.ShapeDtypeStruct((g, CW, CW), BF16),
                   jax.ShapeDtypeStruct((g, CW, SW), BF16),
                   jax.ShapeDtypeStruct((g, CW, SW), BF16),
                   jax.ShapeDtypeStruct((g, 8, SW), F32)] + cast_shapes,
        scratch_shapes=[pltpu.VMEM((pg, N_POW, SW), F32), pltpu.VMEM((pg, N_POW, SW), F32),
                        pltpu.VMEM((pg, 2 * CW, SW), F32)],
        name="s5_prep",
    )(rows, mats, *cast_weights)


class _Path(NamedTuple):
    n_seq: int
    seq_len: int
    pool_n: int
    first_block: int

    @property
    def time_blocks(self):
        return self.seq_len // TIME_BLOCK

    @property
    def n_blocks(self):
        return (self.n_seq // SEQ_TILE) * self.time_blocks


def _paths(shapes_and_pool):
    paths, first = [], 0
    for (n_seq, seq_len, _), pool_n in shapes_and_pool:
        assert n_seq % SEQ_TILE == 0 and seq_len % TIME_BLOCK == 0
        sub_time = TIME_BLOCK // MIX_SLICES
        assert pool_n % sub_time == 0 and seq_len % pool_n == 0 and HALO <= sub_time
        assert pool_n & (pool_n - 1) == 0
        paths.append(_Path(n_seq, seq_len, pool_n, first))
        first += paths[-1].n_blocks
    return tuple(paths), first


def _block_coords(i, paths):
    coords = []
    for k, p in enumerate(paths):
        end = p.first_block + p.n_blocks
        inside = (i >= p.first_block) & (i < end)
        j = jnp.clip(i - p.first_block, 0, p.n_blocks - 1)
        coords.append((inside, (j // p.time_blocks) * SEQ_TILE, (j % p.time_blocks) * TIME_BLOCK))
    return coords


def _mod_block(i, paths):
    blk, first_mod = 0, 0
    for p in paths:
        j = jnp.clip(i - p.first_block, 0, p.n_blocks - 1)
        blk = jnp.where(i >= p.first_block, first_mod + j // p.time_blocks, blk)
        first_mod += p.n_seq // SEQ_TILE
    return blk


def _row_spec(width, n_blocks, shift=0):
    return pl.BlockSpec((TOK_BLOCK, width),
                        lambda i: (jnp.clip(i + shift, 0, n_blocks - 1), 0))


def _block_copies(hbm_refs, buf, sem, i, slot, paths, to_hbm):
    for (inside, seq0, t0), hbm in zip(_block_coords(i, paths), hbm_refs):
        copies = []
        for s in range(SEQ_TILE):
            rows = hbm.at[seq0 + s, pl.ds(t0, TIME_BLOCK), :]
            tile_rows = buf.at[slot, :, s, :]
            src, dst = (tile_rows, rows) if to_hbm else (rows, tile_rows)
            copies.append(pltpu.make_async_copy(src, dst, sem.at[slot]))
        yield inside, copies


def _start_block(hbm_refs, buf, sem, i, slot, paths, to_hbm):
    for inside, copies in _block_copies(hbm_refs, buf, sem, i, slot, paths, to_hbm):
        @pl.when(inside)
        def _():
            for c in copies:
                c.start()


def _wait_block(hbm_refs, buf, sem, slot, paths, to_hbm):
    _, copies = next(_block_copies(hbm_refs, buf, sem, 0, slot, paths, to_hbm))
    for c in copies:
        c.wait()


def _in_proj_kernel(*refs, paths):
    n_paths = len(paths)
    x_hbm = refs[:n_paths]
    mod_ref, w_ref, b_ref, xt_ref, ua_ref, ub_ref, sg_ref, xbuf, sem = refs[n_paths:]
    i = pl.program_id(0)
    n = pl.num_programs(0)
    slot = i % 2

    @pl.when(i == 0)
    def _():
        _start_block(x_hbm, xbuf, sem, i, slot, paths, to_hbm=False)

    @pl.when(i + 1 < n)
    def _():
        _start_block(x_hbm, xbuf, sem, i + 1, 1 - slot, paths, to_hbm=False)

    _wait_block(x_hbm, xbuf, sem, slot, paths, to_hbm=False)
    sub_time = TIME_BLOCK // IN_SLICES
    for sub in range(IN_SLICES):
        times = slice(sub * sub_time, (sub + 1) * sub_time)
        rows = slice(sub * sub_time * SEQ_TILE, (sub + 1) * sub_time * SEQ_TILE)
        x = xbuf[slot, times].reshape(sub_time * SEQ_TILE, D_MODEL)
        xt_ref[rows, :] = x
        h = _modulate(_layer_norm(x), 1.0 + mod_ref[1], mod_ref[0])
        z = _dot(h.astype(BF16), w_ref[...]) + b_ref[...]
        ua_ref[times] = z[:, :D_MODEL].reshape(sub_time, SEQ_TILE, D_MODEL)
        ub_ref[rows, :] = z[:, D_MODEL:D_MODEL + D_POOL].astype(BF16)
        sg_ref[rows, :] = _sigmoid(z[:, D_MODEL + D_POOL:]).astype(BF16)


def _in_proj(xs, paths, n_blocks, mods, w_in, b_in):
    n_tok = n_blocks * TOK_BLOCK
    const = lambda i: (0, 0)
    once = pl.Buffered(1)
    return pl.pallas_call(
        functools.partial(_in_proj_kernel, paths=paths),
        grid=(n_blocks,),
        in_specs=[pl.BlockSpec(memory_space=pl.ANY)] * len(xs) + [
            pl.BlockSpec((N_MOD, SEQ_TILE, D_MODEL), lambda i: (0, _mod_block(i, paths), 0)),
            pl.BlockSpec((D_MODEL, D_IN), const, pipeline_mode=once),
            pl.BlockSpec((1, D_IN), const),
        ],
        out_specs=[_row_spec(D_MODEL, n_blocks),
                   pl.BlockSpec((TIME_BLOCK, SEQ_TILE, D_MODEL), lambda i: (i, 0, 0)),
                   _row_spec(D_POOL, n_blocks), _row_spec(2 * D_MODEL, n_blocks)],
        out_shape=[jax.ShapeDtypeStruct((n_tok, D_MODEL), F32),
                   jax.ShapeDtypeStruct((n_blocks * TIME_BLOCK, SEQ_TILE, D_MODEL), F32),
                   jax.ShapeDtypeStruct((n_tok, D_POOL), BF16),
                   jax.ShapeDtypeStruct((n_tok, 2 * D_MODEL), BF16)],
        scratch_shapes=[pltpu.VMEM((2, TIME_BLOCK, SEQ_TILE, D_MODEL), F32),
                        pltpu.SemaphoreType.DMA((2,))],
        compiler_params=pltpu.CompilerParams(dimension_semantics=("arbitrary",),
                                             vmem_limit_bytes=VMEM_LIMIT),
        name="in_proj",
    )(*xs, mods, w_in, b_in)


def _s5_core_kernel(*refs, paths, n_cast):
    ua_ref, mt_ref, wb_ref, wct_ref, lam_ref, s0_ref = refs[:6]
    v_ref, fin_ref = refs[6 + n_cast:8 + n_cast]
    _cast_chunks(refs[6:6 + n_cast], refs[8 + n_cast:8 + 2 * n_cast])
    xt_ref, yt_ref, sloc_ref, sprf_ref, sprb_ref, fsc_ref = refs[8 + 2 * n_cast:]
    for k, p in enumerate(paths):
        @pl.when(pl.program_id(1) == k)
        def _():
            _s5_strip(ua_ref, mt_ref, wb_ref, wct_ref, lam_ref, s0_ref, v_ref, fin_ref,
                      xt_ref, yt_ref, sloc_ref, sprf_ref, sprb_ref, fsc_ref,
                      n_seq=p.n_seq, seq_len=p.seq_len, has_init=k > 0)


def _s5_strip(ua_ref, mt_ref, wb_ref, wct_ref, lam_ref, s0_ref, v_ref, fin_ref,
              xt_ref, yt_ref, sloc_ref, sprf_ref, sprb_ref, fsc_ref, *, n_seq, seq_len, has_init):
    n_chunk = seq_len // CHUNK
    n_sb = n_seq // SEQ_TILE
    half = 2 * S5_STATE

    for tau in range(CHUNK):
        x_tau = jnp.concatenate([ua_ref[sb * seq_len + c * CHUNK + tau]
                                 for c in range(n_chunk) for sb in range(n_sb)], axis=0)
        xt = x_tau.astype(BF16).T
        for gi in range(GROUPS_PER_STRIP):
            xt_ref[gi, tau * S5_GROUP:(tau + 1) * S5_GROUP, :] = xt[gi * S5_GROUP:(gi + 1) * S5_GROUP, :]

    is_f = lax.broadcasted_iota(jnp.int32, (1, half), 1) < S5_STATE
    col = lax.broadcasted_iota(jnp.int32, (1, SW), 1)
    col_is_f = (col & (half - 1)) < S5_STATE

    def one_group(gi, slot):
        xg = xt_ref[gi]
        u = xg.T
        sloc_ref[slot] = _dot(u, wb_ref[gi])
        ar = lam_ref[gi, 0:1, 0:half]
        ai = lam_ref[gi, 1:2, 0:half]
        if has_init:
            s_re = s0_ref[gi, 0]
            s_im = s0_ref[gi, 1]
        else:
            s_re = jnp.zeros((n_seq, half), F32)
            s_im = jnp.zeros((n_seq, half), F32)
        for c in range(n_chunk):
            rf = pl.ds(c * n_seq, n_seq)
            rb = pl.ds((n_chunk - 1 - c) * n_seq, n_seq)
            sprf_ref[slot, rf, 0:half] = s_re
            sprf_ref[slot, rf, half:SW] = s_im
            sprb_ref[slot, rb, 0:half] = s_re
            sprb_ref[slot, rb, half:SW] = s_im
            l_re = jnp.where(is_f, sloc_ref[slot, rf, 0:half], sloc_ref[slot, rb, 0:half])
            l_im = jnp.where(is_f, sloc_ref[slot, rf, half:SW], sloc_ref[slot, rb, half:SW])
            s_re, s_im = (ar * s_re - ai * s_im + l_re,
                          ar * s_im + ai * s_re + l_im)
        if not has_init:
            fsc_ref[gi, 0] = s_re
            fsc_ref[gi, 1] = s_im
        sprev = jnp.where(col_is_f, sprf_ref[slot], sprb_ref[slot]).astype(BF16)
        yt = _dot(mt_ref[gi], xg) + _dot_nt(wct_ref[gi], sprev)
        yt_ref[gi] = jax.nn.gelu(yt)

    def group_body(i, carry):
        for slot in range(GROUP_UNROLL):
            one_group(i * GROUP_UNROLL + slot, slot)
        return carry

    lax.fori_loop(0, GROUPS_PER_STRIP // GROUP_UNROLL, group_body, 0)

    if not has_init:
        for part in range(2):
            fs = jnp.swapaxes(fsc_ref[:, part], 0, 1)
            fin_ref[:, part] = fs[:, :, 0:S5_STATE]
            fin_ref[:, 2 + part] = fs[:, :, S5_STATE:half]

    for t in range(CHUNK):
        vt = yt_ref[:, t * S5_GROUP:(t + 1) * S5_GROUP, :].reshape(LANES, n_seq * n_chunk)
        v_t = vt.T
        for c in range(n_chunk):
            for sb in range(n_sb):
                r0 = (c * n_sb + sb) * SEQ_TILE
                v_ref[sb * seq_len + c * CHUNK + t] = v_t[r0:r0 + SEQ_TILE, :]


def _s5_core(ua, paths, m, wb, wct, lam, s0, cast_weights):
    ctx, lat = paths
    n_steps = N_STRIPS * len(paths)
    cast_specs, cast_shapes = _cast_specs(cast_weights, n_steps, lambda j, k: j * len(paths) + k)
    tiles = ctx.n_blocks * TIME_BLOCK
    rows = tiles * SEQ_TILE // CHUNK
    assert lat.n_blocks * TIME_BLOCK == tiles and s0.shape[2] == lat.n_seq
    gps = GROUPS_PER_STRIP
    strip = pl.BlockSpec((tiles, SEQ_TILE, LANES), lambda j, k: (k, 0, j))
    mat = pl.BlockSpec((gps, CW, SW), lambda j, k: (j, 0, 0))
    return pl.pallas_call(
        functools.partial(_s5_core_kernel, paths=paths, n_cast=len(cast_weights)),
        grid=(N_STRIPS, len(paths)),
        in_specs=[strip, mat, mat, mat, pl.BlockSpec((gps, 8, SW), lambda j, k: (j, 0, 0)),
                  pl.BlockSpec((gps, 2, lat.n_seq, 2 * S5_STATE), lambda j, k: (j, 0, 0, 0))
                  ] + cast_specs,
        out_specs=[strip,
                   pl.BlockSpec((ctx.n_seq, 4, gps, S5_STATE), lambda j, k: (0, 0, j, 0))
                   ] + cast_specs,
        out_shape=[jax.ShapeDtypeStruct(ua.shape, F32),
                   jax.ShapeDtypeStruct((ctx.n_seq, 4, N_GROUPS, S5_STATE), F32)] + cast_shapes,
        scratch_shapes=[pltpu.VMEM((gps, CW, rows), BF16), pltpu.VMEM((gps, CW, rows), F32),
                        pltpu.VMEM((GROUP_UNROLL, rows, SW), F32),
                        pltpu.VMEM((GROUP_UNROLL, rows, SW), F32),
                        pltpu.VMEM((GROUP_UNROLL, rows, SW), F32),
                        pltpu.VMEM((gps, 2, ctx.n_seq, 2 * S5_STATE), F32)],
        compiler_params=pltpu.CompilerParams(dimension_semantics=("arbitrary", "arbitrary"),
                                             vmem_limit_bytes=VMEM_LIMIT),
        name="s5_core",
    )(ua, m, wb, wct, lam, s0, *cast_weights)


def _mix_out_kernel(xt_ref, v_ref, ubp_ref, ub_ref, ubn_ref, sg_ref, mod_ref, wglu_ref, bglu_ref,
                    wpa_ref, wpool_ref, pscale_ref, wpb_ref, wout_ref, bout_ref, g1_ref, b1_ref,
                    o_ref, *, paths):
    i = pl.program_id(0)
    t_block, run_mask = 0, 0
    for p in paths:
        j = jnp.clip(i - p.first_block, 0, p.n_blocks - 1)
        t_block = jnp.where(i >= p.first_block, (j % p.time_blocks) * TIME_BLOCK, t_block)
        run_mask = jnp.where(i >= p.first_block, p.pool_n - 1, run_mask)
    halo_rows = HALO * SEQ_TILE
    ext = jnp.concatenate([ubp_ref[...], ub_ref[...], ubn_ref[...]], axis=0).astype(F32)

    sub_rows = TOK_BLOCK // MIX_SLICES
    sub_time = TIME_BLOCK // MIX_SLICES
    slices = [slice(sub * sub_rows, (sub + 1) * sub_rows) for sub in range(MIX_SLICES)]

    def s5_branch(sub):
        tiles = sub_rows // SEQ_TILE
        v = v_ref[sub * tiles:(sub + 1) * tiles].reshape(sub_rows, D_MODEL)
        glu = v * _sigmoid(_dot(v.astype(BF16), wglu_ref[...]) + bglu_ref[...])
        return _dot(glu.astype(BF16), wpa_ref[...])

    def pool_branch(sub):
        pos = (t_block + sub * sub_time) & run_mask
        first = pos == 0
        last = pos + sub_time == run_mask + 1
        base = sub * sub_rows
        win = jnp.concatenate(
            [jnp.where(first, 0.0, ext[base:base + halo_rows]),
             ext[base + halo_rows:base + halo_rows + sub_rows],
             jnp.where(last, 0.0, ext[base + halo_rows + sub_rows:base + 2 * halo_rows + sub_rows])],
            axis=0)
        t_idx = lax.broadcasted_iota(jnp.int32, (sub_rows, POOL_GROUP), 0) // SEQ_TILE
        pooled = []
        for gi, w in enumerate(POOL_WINDOWS):
            cols = slice(gi * POOL_GROUP, (gi + 1) * POOL_GROUP)
            acc = jnp.zeros((sub_rows, POOL_GROUP), F32)
            for k in range(-(w // 2), w - w // 2):
                r0 = halo_rows + k * SEQ_TILE
                acc = acc + win[r0:r0 + sub_rows, cols]
            lo = jnp.where(first, jnp.maximum(t_idx - w // 2, 0), t_idx - w // 2)
            hi = jnp.where(last, jnp.minimum(t_idx - w // 2 + w, sub_time), t_idx - w // 2 + w)
            p = acc / (hi - lo).astype(F32) - win[halo_rows:halo_rows + sub_rows, cols]
            pooled.append(_dot(p.astype(BF16), wpool_ref[gi]))
        pm = jnp.concatenate(pooled, axis=1) * pscale_ref[...]
        return _dot(pm.astype(BF16), wpb_ref[...])

    ya = [s5_branch(sub) for sub in range(MIX_SLICES)]
    yb = [pool_branch(sub) for sub in range(MIX_SLICES)]
    tm = []
    for sub, rows in enumerate(slices):
        merged = (sg_ref[rows, 0:D_MODEL].astype(F32) * ya[sub]
                  + sg_ref[rows, D_MODEL:2 * D_MODEL].astype(F32) * yb[sub])
        tm.append(_dot(merged.astype(BF16), wout_ref[...]) + bout_ref[...])
    for sub, rows in enumerate(slices):
        y = DEEPNORM_ALPHA * xt_ref[rows, :] + _modulate(tm[sub], mod_ref[2])
        o_ref[rows, :] = _layer_norm(y) * g1_ref[...] + b1_ref[...]


def _mix_out(xt, v, ub, sg, paths, n_blocks, mods, wglu, bglu, wpa, wpool, pscale, wpb, wout,
             bout, g1, b1):
    const2 = lambda i: (0, 0)
    once = pl.Buffered(1)
    vec = pl.BlockSpec((1, D_MODEL), const2)
    sq = pl.BlockSpec((D_MODEL, D_MODEL), const2, pipeline_mode=once)
    rows = functools.partial(_row_spec, n_blocks=n_blocks)
    halo_rows = HALO * SEQ_TILE
    halos_per_block = TOK_BLOCK // halo_rows
    n_halo_blocks = n_blocks * halos_per_block
    halo = lambda index: pl.BlockSpec((halo_rows, D_POOL), lambda i: (index(i), 0))
    return pl.pallas_call(
        functools.partial(_mix_out_kernel, paths=paths),
        grid=(n_blocks,),
        in_specs=[
            rows(D_MODEL),
            pl.BlockSpec((TIME_BLOCK, SEQ_TILE, D_MODEL), lambda i: (i, 0, 0)),
            halo(lambda i: jnp.maximum(i * halos_per_block - 1, 0)), rows(D_POOL),
            halo(lambda i: jnp.minimum((i + 1) * halos_per_block, n_halo_blocks - 1)),
            rows(2 * D_MODEL),
            pl.BlockSpec((N_MOD, SEQ_TILE, D_MODEL), lambda i: (0, _mod_block(i, paths), 0)),
            sq, vec, sq,
            pl.BlockSpec((len(POOL_WINDOWS), POOL_GROUP, POOL_GROUP), lambda i: (0, 0, 0)),
            pl.BlockSpec((1, D_POOL), const2),
            pl.BlockSpec((D_POOL, D_MODEL), const2, pipeline_mode=once),
            sq, vec, vec, vec,
        ],
        out_specs=rows(D_MODEL),
        out_shape=jax.ShapeDtypeStruct((n_blocks * TOK_BLOCK, D_MODEL), F32),
        compiler_params=pltpu.CompilerParams(vmem_limit_bytes=VMEM_LIMIT),
        name="mix_out",
    )(xt, v, ub, ub, ub, sg, mods, wglu, bglu, wpa, wpool, pscale, wpb, wout, bout, g1, b1)


def _mlp_kernel(*refs, paths):
    n_paths = len(paths)
    x_ref, mod_ref, w1_ref, b1_ref, w2_ref, b2_ref, g2_ref, be2_ref = refs[:8]
    y_hbm = refs[8:8 + n_paths]
    obuf, sem = refs[8 + n_paths:]
    i = pl.program_id(0)
    n = pl.num_programs(0)
    slot = i % 2

    @pl.when(i >= 2)
    def _():
        _wait_block(y_hbm, obuf, sem, slot, paths, to_hbm=True)

    sub_time = TIME_BLOCK // MLP_SLICES
    for sub in range(MLP_SLICES):
        x = x_ref[sub * sub_time * SEQ_TILE:(sub + 1) * sub_time * SEQ_TILE, :]
        h = _modulate(_layer_norm(x), 1.0 + mod_ref[4], mod_ref[3]).astype(BF16)
        f = jnp.zeros(x.shape, F32)
        for k in range(D_FF // D_MODEL):
            cols = slice(k * D_MODEL, (k + 1) * D_MODEL)
            a = jnp.square(jax.nn.relu(_dot(h, w1_ref[:, cols]) + b1_ref[:, cols]))
            f = f + _dot(a.astype(BF16), w2_ref[cols, :])
        y = DEEPNORM_ALPHA * x + _modulate(f + b2_ref[...], mod_ref[5])
        obuf[slot, sub * sub_time:(sub + 1) * sub_time] = (
            _layer_norm(y) * g2_ref[...] + be2_ref[...]).reshape(sub_time, SEQ_TILE, D_MODEL)

    _start_block(y_hbm, obuf, sem, i, slot, paths, to_hbm=True)

    @pl.when(i == n - 1)
    def _():
        @pl.when(n >= 2)
        def _():
            _wait_block(y_hbm, obuf, sem, 1 - slot, paths, to_hbm=True)
        _wait_block(y_hbm, obuf, sem, slot, paths, to_hbm=True)


def _mlp(x1, paths, n_blocks, mods, w1, b1, w2, b2, g2, be2):
    const2 = lambda i: (0, 0)
    vec = pl.BlockSpec((1, D_MODEL), const2)
    once = pl.Buffered(1)
    return pl.pallas_call(
        functools.partial(_mlp_kernel, paths=paths),
        grid=(n_blocks,),
        in_specs=[
            _row_spec(D_MODEL, n_blocks),
            pl.BlockSpec((N_MOD, SEQ_TILE, D_MODEL), lambda i: (0, _mod_block(i, paths), 0)),
            pl.BlockSpec((D_MODEL, D_FF), const2, pipeline_mode=once),
            pl.BlockSpec((1, D_FF), const2),
            pl.BlockSpec((D_FF, D_MODEL), const2, pipeline_mode=once),
            vec, vec, vec,
        ],
        out_specs=[pl.BlockSpec(memory_space=pl.ANY)] * len(paths),
        out_shape=[jax.ShapeDtypeStruct((p.n_seq, p.seq_len, D_MODEL), F32) for p in paths],
        scratch_shapes=[pltpu.VMEM((2, TIME_BLOCK, SEQ_TILE, D_MODEL), F32),
                        pltpu.SemaphoreType.DMA((2,))],
        compiler_params=pltpu.CompilerParams(dimension_semantics=("arbitrary",),
                                             vmem_limit_bytes=VMEM_LIMIT),
        name="mlp",
    )(x1, mods, w1, b1, w2, b2, g2, be2)


def _state_cols(x_f, x_b):
    return jnp.concatenate([x_f, x_b, x_f, x_b], axis=-1)


def kernel(x_prompt, x_sample, state_s5, c, c_ctx, w_ada, b_ada, w_in, b_in, s5_lam_re, s5_lam_im, s5_log_dt, s5_b_re, s5_b_im, s5_c_re, s5_c_im, s5_d, w_glu, b_glu, w_proj_a, w_pool, pool_scale, w_proj_b, w_out, b_out, ln1_g, ln1_b, w_mlp1, b_mlp1, w_mlp2, b_mlp2, ln2_g, ln2_b):
    assert w_in.shape[0] == 1, "single-layer backbone"
    n_ctx = x_prompt.shape[0]
    n_lat = x_sample.shape[0]
    g, p, hh = N_GROUPS, S5_STATE, S5_GROUP
    paths, n_blocks = _paths([(x_prompt.shape, x_prompt.shape[1]), (x_sample.shape, GRID_W)])

    n_vec = 1 + n_lat
    n_rows = -(-n_vec // 8) * 8
    cvec = jnp.concatenate([c_ctx[None, :], c, jnp.zeros((n_rows - n_vec, D_MODEL), F32)], axis=0)
    mods = _mods(cvec, w_ada[0], b_ada[0][None, :])
    mods = jnp.concatenate([jnp.broadcast_to(mods[:, 0:1], (N_MOD, n_ctx, D_MODEL)),
                            mods[:, 1:n_vec]], axis=1)

    dirs = lambda x: _state_cols(x[0], x[1])
    rows = jnp.stack(
        [dirs(s5_lam_re[0]), dirs(s5_lam_im[0]),
         dirs(jnp.broadcast_to(s5_log_dt[0][:, :, None], (2, g, p))),
         jnp.tile(s5_d[0, 0].reshape(g, hh), (1, CHUNK)),
         jnp.tile(s5_d[0, 1].reshape(g, hh), (1, CHUNK))], axis=1)
    mats = jnp.stack(
        [dirs(s5_b_re[0].transpose(0, 1, 3, 2)), dirs(s5_b_im[0].transpose(0, 1, 3, 2)),
         dirs(s5_c_re[0]), dirs(s5_c_im[0])], axis=1)
    mt_mat, wb_mat, wct_mat, lam16, w_in_b = _s5_prep(rows, mats, [w_in[0]])

    xt, ua, ub, sg = _in_proj((x_prompt, x_sample), paths, n_blocks, mods, w_in_b,
                              b_in[0][None, :])

    st = state_s5[:, 0].astype(F32)
    s0 = jnp.concatenate([st[:, 0], st[:, 1]], axis=-1).transpose(2, 1, 0, 3)
    v, fin, w_glu_b, w_pa_b, w_pool_b, w_pb_b, w_out_b, w1_b, w2_b = _s5_core(
        ua, paths, mt_mat, wb_mat, wct_mat, lam16, s0,
        [w_glu[0], w_proj_a[0], w_pool[0].reshape(D_POOL, POOL_GROUP), w_proj_b[0], w_out[0],
         w_mlp1[0], w_mlp2[0]])
    new_state = fin.reshape(n_ctx, 1, 2, 2, g, p)

    x1 = _mix_out(xt, v, ub, sg, paths, n_blocks, mods,
                  w_glu_b, b_glu[0][None, :], w_pa_b, w_pool_b.reshape(w_pool.shape[1:]),
                  pool_scale[0][None, :], w_pb_b, w_out_b, b_out[0][None, :],
                  ln1_g[0][None, :], ln1_b[0][None, :])
    y_p, y_s = _mlp(x1, paths, n_blocks, mods, w1_b, b_mlp1[0][None, :], w2_b,
                    b_mlp2[0][None, :], ln2_g[0][None, :], ln2_b[0][None, :])
    return (y_p, y_s, new_state)
```

```python
import functools
from typing import NamedTuple

import jax
import jax.numpy as jnp
from jax import lax
from jax.experimental import pallas as pl
from jax.experimental.pallas import tpu as pltpu

F32 = jnp.float32
BF16 = jnp.bfloat16

D_MODEL = 1024
S5_GROUP = 16
N_GROUPS = D_MODEL // S5_GROUP
S5_STATE = 64
D_POOL = D_MODEL // 2
POOL_WINDOWS = (2, 4, 8, 16)
POOL_GROUP = D_POOL // len(POOL_WINDOWS)
D_IN = D_MODEL + D_POOL + 2 * D_MODEL
D_REST = D_IN - D_MODEL
D_FF = 4 * D_MODEL
N_MOD = 6
GRID_W = 64
DEEPNORM_ALPHA = 2.0 ** 0.25
LN_EPS = 1e-6

CHUNK = 16
CW = CHUNK * S5_GROUP
SW = 4 * S5_STATE
LANES = 128
N_STRIPS = D_MODEL // LANES
GROUPS_PER_STRIP = LANES // S5_GROUP
GROUP_UNROLL = 8
N_POW = 32
SEQ_TILE = 8
TIME_BLOCK = 128
TOK_BLOCK = SEQ_TILE * TIME_BLOCK
HALO = max(POOL_WINDOWS) // 2
IN_SLICES = 4
MLP_SLICES = 4
MIX_SLICES = 4
VMEM_LIMIT = 56 * 1024 * 1024


def _time_major(x3):
    return jnp.swapaxes(x3, 0, 1).reshape(x3.shape[1] * SEQ_TILE, x3.shape[-1])


def _seq_major(x2):
    return jnp.swapaxes(x2.reshape(x2.shape[0] // SEQ_TILE, SEQ_TILE, x2.shape[-1]), 0, 1)


def _modulate(a, mul, add=None):
    a3 = a.reshape(a.shape[0] // SEQ_TILE, SEQ_TILE, a.shape[-1]) * mul[None]
    if add is not None:
        a3 = a3 + add[None]
    return a3.reshape(a.shape)


def _layer_norm(x):
    mu = jnp.mean(x, axis=-1, keepdims=True)
    xc = x - mu
    var = jnp.mean(xc * xc, axis=-1, keepdims=True)
    return xc * lax.rsqrt(var + LN_EPS)


def _sigmoid(x):
    return 0.5 * jnp.tanh(0.5 * x) + 0.5


def _dot(a, b):
    return jnp.dot(a, b, preferred_element_type=F32)


def _dot_nt(a, b, precision=None):
    return lax.dot_general(a, b, (((1,), (1,)), ((), ())), precision=precision,
                           preferred_element_type=F32)


def _mods_kernel(c_ref, cctx_ref, w_ref, b_ref, o_ref):
    cvec = jnp.concatenate([c_ref[...], jnp.broadcast_to(cctx_ref[...], (SEQ_TILE, D_MODEL))],
                           axis=0)
    o_ref[0] = jnp.dot(jax.nn.silu(cvec), w_ref[...], precision=lax.Precision.HIGHEST,
                       preferred_element_type=F32) + b_ref[...]


def _mods(c, c_ctx, w_ada, b_ada):
    n_lat = c.shape[0]
    assert n_lat % SEQ_TILE == 0
    rows = n_lat + SEQ_TILE
    n_out = w_ada.shape[1]
    return pl.pallas_call(
        _mods_kernel,
        grid=(n_out // D_MODEL,),
        in_specs=[
            pl.BlockSpec((n_lat, D_MODEL), lambda j: (0, 0)),
            pl.BlockSpec((1, D_MODEL), lambda j: (0, 0)),
            pl.BlockSpec((D_MODEL, D_MODEL), lambda j: (0, j)),
            pl.BlockSpec((1, D_MODEL), lambda j: (0, j)),
        ],
        out_specs=pl.BlockSpec((1, rows, D_MODEL), lambda j: (j, 0, 0)),
        out_shape=jax.ShapeDtypeStruct((n_out // D_MODEL, rows, D_MODEL), F32),
        name="mods",
    )(c, c_ctx, w_ada, b_ada)


def _cast_specs(weights, n_steps, step_index):
    specs, shapes = [], []
    for w in weights:
        rows = w.shape[0] // n_steps
        assert rows * n_steps == w.shape[0] and rows % 16 == 0
        specs.append(pl.BlockSpec((rows, w.shape[1]), lambda *g: (step_index(*g), 0)))
        shapes.append(jax.ShapeDtypeStruct(w.shape, BF16))
    return specs, shapes


def _cast_chunks(src_refs, dst_refs):
    for src, dst in zip(src_refs, dst_refs):
        dst[...] = src[...].astype(BF16)


def _s5_prep_kernel(*refs, n_cast):
    rows_ref, mats_ref = refs[:2]
    mt_ref, wb_ref, wct_ref, lam_ref = refs[2 + n_cast:6 + n_cast]
    _cast_chunks(refs[2:2 + n_cast], refs[6 + n_cast:6 + 2 * n_cast])
    _s5_prep_body(rows_ref, mats_ref, mt_ref, wb_ref, wct_ref, lam_ref, *refs[6 + 2 * n_cast:])


def _s5_prep_body(rows_ref, mats_ref, mt_ref, wb_ref, wct_ref, lam_ref,
                  pwr_ref, pwi_ref, ge_ref):
    pg = GROUPS_PER_STRIP
    lre = rows_ref[:, 0:1, :]
    lim = rows_ref[:, 1:2, :]
    dt = jnp.exp(rows_ref[:, 2:3, :])
    a = lre * dt
    b = lim * dt
    col = lax.broadcasted_iota(jnp.int32, (1, 1, SW), 2)
    is_im = col >= 2 * S5_STATE
    is_b = (col & (2 * S5_STATE - 1)) >= S5_STATE

    mag = jnp.exp(a)
    sq_r = mag * jnp.cos(b)
    sq_i = mag * jnp.sin(b)
    pwr_ref[:, 0:1, :] = jnp.ones((pg, 1, SW), F32)
    pwi_ref[:, 0:1, :] = jnp.zeros((pg, 1, SW), F32)
    m = 1
    while m < N_POW:
        lo_r = pwr_ref[:, 0:m, :]
        lo_i = pwi_ref[:, 0:m, :]
        pwr_ref[:, m:2 * m, :] = lo_r * sq_r - lo_i * sq_i
        pwi_ref[:, m:2 * m, :] = lo_r * sq_i + lo_i * sq_r
        sq_r, sq_i = sq_r * sq_r - sq_i * sq_i, 2.0 * sq_r * sq_i
        m *= 2

    def power(kf, kb):
        def row(ref, kk):
            return jnp.zeros((pg, 1, SW), F32) if kk is None else ref[:, kk:kk + 1, :]
        if kf == kb:
            return row(pwr_ref, kf), row(pwi_ref, kf)
        return (jnp.where(is_b, row(pwr_ref, kb), row(pwr_ref, kf)),
                jnp.where(is_b, row(pwi_ref, kb), row(pwi_ref, kf)))

    lbr = pwr_ref[:, 1:2, :]
    lbi = pwi_ref[:, 1:2, :]
    den = lre * lre + lim * lim
    nr = lbr - 1.0
    cr = (nr * lre + lbi * lim) / den
    ci = (lbi * lre - nr * lim) / den
    bre = mats_ref[:, 0]
    bim = mats_ref[:, 1]
    bbr = cr * bre - ci * bim
    bbi = cr * bim + ci * bre
    bx = jnp.where(is_im, bbi, bbr)
    by = jnp.where(is_im, bbr, -bbi)
    cre = mats_ref[:, 2]
    cim = mats_ref[:, 3]
    cx = jnp.where(is_im, -cim, cre)
    cy = jnp.where(is_im, -cre, -cim)

    for t in range(CHUNK):
        rows = slice(t * S5_GROUP, (t + 1) * S5_GROUP)
        pr, pi = power(CHUNK - 1 - t, t)
        wb_ref[:, rows, :] = (pr * bx + pi * by).astype(BF16)
        pr, pi = power(t + 1, CHUNK - t)
        wct_ref[:, rows, :] = (pr * cx + pi * cy).astype(BF16)

    for j in range(2 * CHUNK):
        rows = slice(j * S5_GROUP, (j + 1) * S5_GROUP)
        if j == 2 * CHUNK - 1:
            ge_ref[:, rows, :] = jnp.zeros((pg, S5_GROUP, SW), F32)
            continue
        pr, pi = power(j - (CHUNK - 1) if j >= CHUNK - 1 else None,
                       (CHUNK - 1) - j if j <= CHUNK - 1 else None)
        ge_ref[:, rows, :] = pr * cx + pi * cy

    dsum = rows_ref[:, 3:4, :] + rows_ref[:, 4:5, :]
    r16 = lax.broadcasted_iota(jnp.int32, (S5_GROUP, CW), 0)
    c16 = lax.broadcasted_iota(jnp.int32, (S5_GROUP, CW), 1)
    for gi in range(pg):
        e = _dot_nt(bx[gi], ge_ref[gi], precision=lax.Precision.HIGHEST)
        blocks = []
        for tau in range(CHUNK):
            start = (CHUNK - 1 - tau) * S5_GROUP
            blocks.append(e[:, start:start + CW]
                          + jnp.where(c16 == r16 + tau * S5_GROUP, dsum[gi], 0.0))
        mt_ref[gi] = jnp.concatenate(blocks, axis=0).T.astype(BF16)

    lam_ref[...] = jnp.concatenate(
        [pwr_ref[:, CHUNK:CHUNK + 1, :], pwi_ref[:, CHUNK:CHUNK + 1, :],
         jnp.zeros((pg, 6, SW), F32)], axis=1)


def _s5_prep(rows, mats, cast_weights):
    g = N_GROUPS
    pg = GROUPS_PER_STRIP
    n_steps = g // pg
    mat_out = pl.BlockSpec((pg, CW, SW), lambda i: (i, 0, 0))
    cast_specs, cast_shapes = _cast_specs(cast_weights, n_steps, lambda i: i)
    return pl.pallas_call(
        functools.partial(_s5_prep_kernel, n_cast=len(cast_weights)),
        grid=(n_steps,),
        in_specs=[pl.BlockSpec((pg,) + rows.shape[1:], lambda i: (i, 0, 0)),
                  pl.BlockSpec((pg,) + mats.shape[1:], lambda i: (i, 0, 0, 0))] + cast_specs,
        out_specs=[mat_out, mat_out, mat_out,
                   pl.BlockSpec((pg, 8, SW), lambda i: (i, 0, 0))] + cast_specs,
        out_shape=[jax.ShapeDtypeStruct((g, CW, CW), BF16),
                   jax.ShapeDtypeStruct((g, CW, SW), BF16),
                   jax.ShapeDtypeStruct((g, CW, SW), BF16),
                   jax.ShapeDtypeStruct((g, 8, SW), F32)] + cast_shapes,
        scratch_shapes=[pltpu.VMEM((pg, N_POW, SW), F32), pltpu.VMEM((pg, N_POW, SW), F32),
                        pltpu.VMEM((pg, 2 * CW, SW), F32)],
        name="s5_prep",
    )(rows, mats, *cast_weights)


class _Path(NamedTuple):
    n_seq: int
    seq_len: int
    pool_n: int
    first_block: int
    mod_block: int
    mod_per_seq: bool

    @property
    def time_blocks(self):
        return self.seq_len // TIME_BLOCK

    @property
    def n_blocks(self):
        return (self.n_seq // SEQ_TILE) * self.time_blocks


def _paths(specs):
    paths, first = [], 0
    for (n_seq, seq_len, _), pool_n, mod_block, mod_per_seq in specs:
        assert n_seq % SEQ_TILE == 0 and seq_len % TIME_BLOCK == 0
        sub_time = TIME_BLOCK // MIX_SLICES
        assert pool_n % sub_time == 0 and seq_len % pool_n == 0 and HALO <= sub_time
        assert pool_n & (pool_n - 1) == 0
        paths.append(_Path(n_seq, seq_len, pool_n, first, mod_block, mod_per_seq))
        first += paths[-1].n_blocks
    return tuple(paths), first


def _block_coords(i, paths):
    coords = []
    for k, p in enumerate(paths):
        end = p.first_block + p.n_blocks
        inside = (i >= p.first_block) & (i < end)
        j = jnp.clip(i - p.first_block, 0, p.n_blocks - 1)
        coords.append((inside, (j // p.time_blocks) * SEQ_TILE, (j % p.time_blocks) * TIME_BLOCK))
    return coords


def _mod_block(i, paths):
    blk = 0
    for p in paths:
        j = jnp.clip(i - p.first_block, 0, p.n_blocks - 1)
        seq_tile = j // p.time_blocks if p.mod_per_seq else 0
        blk = jnp.where(i >= p.first_block, p.mod_block + seq_tile, blk)
    return blk


def _row_spec(width, n_blocks, shift=0):
    return pl.BlockSpec((TOK_BLOCK, width),
                        lambda i: (jnp.clip(i + shift, 0, n_blocks - 1), 0))


def _block_copies(hbm_refs, buf, sem, i, slot, paths, to_hbm):
    for (inside, seq0, t0), hbm in zip(_block_coords(i, paths), hbm_refs):
        copies = []
        for s in range(SEQ_TILE):
            rows = hbm.at[seq0 + s, pl.ds(t0, TIME_BLOCK), :]
            tile_rows = buf.at[slot, :, s, :]
            src, dst = (tile_rows, rows) if to_hbm else (rows, tile_rows)
            copies.append(pltpu.make_async_copy(src, dst, sem.at[slot]))
        yield inside, copies


def _start_block(hbm_refs, buf, sem, i, slot, paths, to_hbm):
    for inside, copies in _block_copies(hbm_refs, buf, sem, i, slot, paths, to_hbm):
        @pl.when(inside)
        def _():
            for c in copies:
                c.start()


def _wait_block(hbm_refs, buf, sem, slot, paths, to_hbm):
    _, copies = next(_block_copies(hbm_refs, buf, sem, 0, slot, paths, to_hbm))
    for c in copies:
        c.wait()


def _in_proj_kernel(*refs, paths):
    n_paths = len(paths)
    x_hbm = refs[:n_paths]
    mod_ref, w_ref, b_ref, xt_ref, ua_ref, ub_ref, sg_ref, xbuf, sem = refs[n_paths:]
    i = pl.program_id(0)
    n = pl.num_programs(0)
    slot = i % 2

    @pl.when(i == 0)
    def _():
        _start_block(x_hbm, xbuf, sem, i, slot, paths, to_hbm=False)

    @pl.when(i + 1 < n)
    def _():
        _start_block(x_hbm, xbuf, sem, i + 1, 1 - slot, paths, to_hbm=False)

    _wait_block(x_hbm, xbuf, sem, slot, paths, to_hbm=False)
    sub_time = TIME_BLOCK // IN_SLICES
    for sub in range(IN_SLICES):
        times = slice(sub * sub_time, (sub + 1) * sub_time)
        rows = slice(sub * sub_time * SEQ_TILE, (sub + 1) * sub_time * SEQ_TILE)
        x = xbuf[slot, times].reshape(sub_time * SEQ_TILE, D_MODEL)
        xt_ref[rows, :] = x
        h = _modulate(_layer_norm(x), 1.0 + mod_ref[1], mod_ref[0])
        z = _dot(h.astype(BF16), w_ref[...]) + b_ref[...]
        ua_ref[times] = z[:, :D_MODEL].reshape(sub_time, SEQ_TILE, D_MODEL)
        ub_ref[rows, :] = z[:, D_MODEL:D_MODEL + D_POOL].astype(BF16)
        sg_ref[rows, :] = _sigmoid(z[:, D_MODEL + D_POOL:]).astype(BF16)


def _in_proj(xs, paths, n_blocks, mods, w_in, b_in):
    n_tok = n_blocks * TOK_BLOCK
    const = lambda i: (0, 0)
    once = pl.Buffered(1)
    return pl.pallas_call(
        functools.partial(_in_proj_kernel, paths=paths),
        grid=(n_blocks,),
        in_specs=[pl.BlockSpec(memory_space=pl.ANY)] * len(xs) + [
            pl.BlockSpec((N_MOD, SEQ_TILE, D_MODEL), lambda i: (0, _mod_block(i, paths), 0)),
            pl.BlockSpec((D_MODEL, D_IN), const, pipeline_mode=once),
            pl.BlockSpec((1, D_IN), const),
        ],
        out_specs=[_row_spec(D_MODEL, n_blocks),
                   pl.BlockSpec((TIME_BLOCK, SEQ_TILE, D_MODEL), lambda i: (i, 0, 0)),
                   _row_spec(D_POOL, n_blocks), _row_spec(2 * D_MODEL, n_blocks)],
        out_shape=[jax.ShapeDtypeStruct((n_tok, D_MODEL), F32),
                   jax.ShapeDtypeStruct((n_blocks * TIME_BLOCK, SEQ_TILE, D_MODEL), F32),
                   jax.ShapeDtypeStruct((n_tok, D_POOL), BF16),
                   jax.ShapeDtypeStruct((n_tok, 2 * D_MODEL), BF16)],
        scratch_shapes=[pltpu.VMEM((2, TIME_BLOCK, SEQ_TILE, D_MODEL), F32),
                        pltpu.SemaphoreType.DMA((2,))],
        compiler_params=pltpu.CompilerParams(dimension_semantics=("arbitrary",),
                                             vmem_limit_bytes=VMEM_LIMIT),
        name="in_proj",
    )(*xs, mods, w_in, b_in)


def _s5_core_kernel(*refs, paths, n_cast):
    ua_ref, mt_ref, wb_ref, wct_ref, lam_ref, s0_ref = refs[:6]
    v_ref, fin_ref = refs[6 + n_cast:8 + n_cast]
    _cast_chunks(refs[6:6 + n_cast], refs[8 + n_cast:8 + 2 * n_cast])
    xt_ref, yt_ref, sloc_ref, sprf_ref, sprb_ref, fsc_ref = refs[8 + 2 * n_cast:]
    for k, p in enumerate(paths):
        @pl.when(pl.program_id(1) == k)
        def _():
            _s5_strip(ua_ref, mt_ref, wb_ref, wct_ref, lam_ref, s0_ref, v_ref, fin_ref,
                      xt_ref, yt_ref, sloc_ref, sprf_ref, sprb_ref, fsc_ref,
                      n_seq=p.n_seq, seq_len=p.seq_len, has_init=k > 0)


def _s5_strip(ua_ref, mt_ref, wb_ref, wct_ref, lam_ref, s0_ref, v_ref, fin_ref,
              xt_ref, yt_ref, sloc_ref, sprf_ref, sprb_ref, fsc_ref, *, n_seq, seq_len, has_init):
    n_chunk = seq_len // CHUNK
    n_sb = n_seq // SEQ_TILE
    half = 2 * S5_STATE

    for tau in range(CHUNK):
        x_tau = jnp.concatenate([ua_ref[sb * seq_len + c * CHUNK + tau]
                                 for c in range(n_chunk) for sb in range(n_sb)], axis=0)
        xt = x_tau.astype(BF16).T
        for gi in range(GROUPS_PER_STRIP):
            xt_ref[gi, tau * S5_GROUP:(tau + 1) * S5_GROUP, :] = xt[gi * S5_GROUP:(gi + 1) * S5_GROUP, :]

    is_f = lax.broadcasted_iota(jnp.int32, (1, half), 1) < S5_STATE
    col = lax.broadcasted_iota(jnp.int32, (1, SW), 1)
    col_is_f = (col & (half - 1)) < S5_STATE

    def one_group(gi, slot):
        xg = xt_ref[gi]
        u = xg.T
        sloc_ref[slot] = _dot(u, wb_ref[gi])
        ar = lam_ref[gi, 0:1, 0:half]
        ai = lam_ref[gi, 1:2, 0:half]
        if has_init:
            s_re = s0_ref[gi, 0]
            s_im = s0_ref[gi, 1]
        else:
            s_re = jnp.zeros((n_seq, half), F32)
            s_im = jnp.zeros((n_seq, half), F32)
        for c in range(n_chunk):
            rf = pl.ds(c * n_seq, n_seq)
            rb = pl.ds((n_chunk - 1 - c) * n_seq, n_seq)
            sprf_ref[slot, rf, 0:half] = s_re
            sprf_ref[slot, rf, half:SW] = s_im
            sprb_ref[slot, rb, 0:half] = s_re
            sprb_ref[slot, rb, half:SW] = s_im
            l_re = jnp.where(is_f, sloc_ref[slot, rf, 0:half], sloc_ref[slot, rb, 0:half])
            l_im = jnp.where(is_f, sloc_ref[slot, rf, half:SW], sloc_ref[slot, rb, half:SW])
            s_re, s_im = (ar * s_re - ai * s_im + l_re,
                          ar * s_im + ai * s_re + l_im)
        if not has_init:
            fsc_ref[gi, 0] = s_re
            fsc_ref[gi, 1] = s_im
        sprev = jnp.where(col_is_f, sprf_ref[slot], sprb_ref[slot]).astype(BF16)
        yt = _dot(mt_ref[gi], xg) + _dot_nt(wct_ref[gi], sprev)
        yt_ref[gi] = jax.nn.gelu(yt)

    def group_body(i, carry):
        for slot in range(GROUP_UNROLL):
            one_group(i * GROUP_UNROLL + slot, slot)
        return carry

    lax.fori_loop(0, GROUPS_PER_STRIP // GROUP_UNROLL, group_body, 0)

    if not has_init:
        for part in range(2):
            fs = jnp.swapaxes(fsc_ref[:, part], 0, 1)
            fin_ref[:, part] = fs[:, :, 0:S5_STATE]
            fin_ref[:, 2 + part] = fs[:, :, S5_STATE:half]

    for t in range(CHUNK):
        vt = yt_ref[:, t * S5_GROUP:(t + 1) * S5_GROUP, :].reshape(LANES, n_seq * n_chunk)
        v_t = vt.T
        for c in range(n_chunk):
            for sb in range(n_sb):
                r0 = (c * n_sb + sb) * SEQ_TILE
                v_ref[sb * seq_len + c * CHUNK + t] = v_t[r0:r0 + SEQ_TILE, :]


def _s5_core(ua, paths, m, wb, wct, lam, s0, cast_weights):
    ctx, lat = paths
    n_steps = N_STRIPS * len(paths)
    cast_specs, cast_shapes = _cast_specs(cast_weights, n_steps, lambda j, k: j * len(paths) + k)
    tiles = ctx.n_blocks * TIME_BLOCK
    rows = tiles * SEQ_TILE // CHUNK
    assert lat.n_blocks * TIME_BLOCK == tiles and s0.shape[2] == lat.n_seq
    gps = GROUPS_PER_STRIP
    strip = pl.BlockSpec((tiles, SEQ_TILE, LANES), lambda j, k: (k, 0, j))
    mat = pl.BlockSpec((gps, CW, SW), lambda j, k: (j, 0, 0))
    return pl.pallas_call(
        functools.partial(_s5_core_kernel, paths=paths, n_cast=len(cast_weights)),
        grid=(N_STRIPS, len(paths)),
        in_specs=[strip, mat, mat, mat, pl.BlockSpec((gps, 8, SW), lambda j, k: (j, 0, 0)),
                  pl.BlockSpec((gps, 2, lat.n_seq, 2 * S5_STATE), lambda j, k: (j, 0, 0, 0))
                  ] + cast_specs,
        out_specs=[strip,
                   pl.BlockSpec((ctx.n_seq, 4, gps, S5_STATE), lambda j, k: (0, 0, j, 0))
                   ] + cast_specs,
        out_shape=[jax.ShapeDtypeStruct(ua.shape, F32),
                   jax.ShapeDtypeStruct((ctx.n_seq, 4, N_GROUPS, S5_STATE), F32)] + cast_shapes,
        scratch_shapes=[pltpu.VMEM((gps, CW, rows), BF16), pltpu.VMEM((gps, CW, rows), F32),
                        pltpu.VMEM((GROUP_UNROLL, rows, SW), F32),
                        pltpu.VMEM((GROUP_UNROLL, rows, SW), F32),
                        pltpu.VMEM((GROUP_UNROLL, rows, SW), F32),
                        pltpu.VMEM((gps, 2, ctx.n_seq, 2 * S5_STATE), F32)],
        compiler_params=pltpu.CompilerParams(dimension_semantics=("arbitrary", "arbitrary"),
                                             vmem_limit_bytes=VMEM_LIMIT),
        name="s5_core",
    )(ua, m, wb, wct, lam, s0, *cast_weights)


def _mix_out_kernel(xt_ref, v_ref, ubp_ref, ub_ref, ubn_ref, sg_ref, mod_ref, wglu_ref, bglu_ref,
                    wpa_ref, wpool_ref, pscale_ref, wpb_ref, wout_ref, bout_ref, g1_ref, b1_ref,
                    o_ref, *, paths):
    i = pl.program_id(0)
    t_block, run_mask = 0, 0
    for p in paths:
        j = jnp.clip(i - p.first_block, 0, p.n_blocks - 1)
        t_block = jnp.where(i >= p.first_block, (j % p.time_blocks) * TIME_BLOCK, t_block)
        run_mask = jnp.where(i >= p.first_block, p.pool_n - 1, run_mask)
    halo_rows = HALO * SEQ_TILE
    ext = jnp.concatenate([ubp_ref[...], ub_ref[...], ubn_ref[...]], axis=0).astype(F32)

    sub_rows = TOK_BLOCK // MIX_SLICES
    sub_time = TIME_BLOCK // MIX_SLICES
    slices = [slice(sub * sub_rows, (sub + 1) * sub_rows) for sub in range(MIX_SLICES)]

    def s5_branch(sub):
        tiles = sub_rows // SEQ_TILE
        v = v_ref[sub * tiles:(sub + 1) * tiles].reshape(sub_rows, D_MODEL)
        glu = v * _sigmoid(_dot(v.astype(BF16), wglu_ref[...]) + bglu_ref[...])
        return _dot(glu.astype(BF16), wpa_ref[...])

    def pool_branch(sub):
        pos = (t_block + sub * sub_time) & run_mask
        first = pos == 0
        last = pos + sub_time == run_mask + 1
        base = sub * sub_rows
        win = jnp.concatenate(
            [jnp.where(first, 0.0, ext[base:base + halo_rows]),
             ext[base + halo_rows:base + halo_rows + sub_rows],
             jnp.where(last, 0.0, ext[base + halo_rows + sub_rows:base + 2 * halo_rows + sub_rows])],
            axis=0)
        t_idx = lax.broadcasted_iota(jnp.int32, (sub_rows, POOL_GROUP), 0) // SEQ_TILE
        pooled = []
        for gi, w in enumerate(POOL_WINDOWS):
            cols = slice(gi * POOL_GROUP, (gi + 1) * POOL_GROUP)
            acc = jnp.zeros((sub_rows, POOL_GROUP), F32)
            for k in range(-(w // 2), w - w // 2):
                r0 = halo_rows + k * SEQ_TILE
                acc = acc + win[r0:r0 + sub_rows, cols]
            lo = jnp.where(first, jnp.maximum(t_idx - w // 2, 0), t_idx - w // 2)
            hi = jnp.where(last, jnp.minimum(t_idx - w // 2 + w, sub_time), t_idx - w // 2 + w)
            p = acc / (hi - lo).astype(F32) - win[halo_rows:halo_rows + sub_rows, cols]
            pooled.append(_dot(p.astype(BF16), wpool_ref[gi]))
        pm = jnp.concatenate(pooled, axis=1) * pscale_ref[...]
        return _dot(pm.astype(BF16), wpb_ref[...])

    ya = [s5_branch(sub) for sub in range(MIX_SLICES)]
    yb = [pool_branch(sub) for sub in range(MIX_SLICES)]
    tm = []
    for sub, rows in enumerate(slices):
        merged = (sg_ref[rows, 0:D_MODEL].astype(F32) * ya[sub]
                  + sg_ref[rows, D_MODEL:2 * D_MODEL].astype(F32) * yb[sub])
        tm.append(_dot(merged.astype(BF16), wout_ref[...]) + bout_ref[...])
    for sub, rows in enumerate(slices):
        y = DEEPNORM_ALPHA * xt_ref[rows, :] + _modulate(tm[sub], mod_ref[2])
        o_ref[rows, :] = _layer_norm(y) * g1_ref[...] + b1_ref[...]


def _mix_out(xt, v, ub, sg, paths, n_blocks, mods, wglu, bglu, wpa, wpool, pscale, wpb, wout,
             bout, g1, b1):
    const2 = lambda i: (0, 0)
    once = pl.Buffered(1)
    vec = pl.BlockSpec((1, D_MODEL), const2)
    sq = pl.BlockSpec((D_MODEL, D_MODEL), const2, pipeline_mode=once)
    rows = functools.partial(_row_spec, n_blocks=n_blocks)
    halo_rows = HALO * SEQ_TILE
    halos_per_block = TOK_BLOCK // halo_rows
    n_halo_blocks = n_blocks * halos_per_block
    halo = lambda index: pl.BlockSpec((halo_rows, D_POOL), lambda i: (index(i), 0))
    return pl.pallas_call(
        functools.partial(_mix_out_kernel, paths=paths),
        grid=(n_blocks,),
        in_specs=[
            rows(D_MODEL),
            pl.BlockSpec((TIME_BLOCK, SEQ_TILE, D_MODEL), lambda i: (i, 0, 0)),
            halo(lambda i: jnp.maximum(i * halos_per_block - 1, 0)), rows(D_POOL),
            halo(lambda i: jnp.minimum((i + 1) * halos_per_block, n_halo_blocks - 1)),
            rows(2 * D_MODEL),
            pl.BlockSpec((N_MOD, SEQ_TILE, D_MODEL), lambda i: (0, _mod_block(i, paths), 0)),
            sq, vec, sq,
            pl.BlockSpec((len(POOL_WINDOWS), POOL_GROUP, POOL_GROUP), lambda i: (0, 0, 0)),
            pl.BlockSpec((1, D_POOL), const2),
            pl.BlockSpec((D_POOL, D_MODEL), const2, pipeline_mode=once),
            sq, vec, vec, vec,
        ],
        out_specs=rows(D_MODEL),
        out_shape=jax.ShapeDtypeStruct((n_blocks * TOK_BLOCK, D_MODEL), F32),
        compiler_params=pltpu.CompilerParams(vmem_limit_bytes=VMEM_LIMIT),
        name="mix_out",
    )(xt, v, ub, ub, ub, sg, mods, wglu, bglu, wpa, wpool, pscale, wpb, wout, bout, g1, b1)


def _mlp_kernel(*refs, paths):
    n_paths = len(paths)
    x_ref, mod_ref, w1_ref, b1_ref, w2_ref, b2_ref, g2_ref, be2_ref = refs[:8]
    y_hbm = refs[8:8 + n_paths]
    obuf, sem = refs[8 + n_paths:]
    i = pl.program_id(0)
    n = pl.num_programs(0)
    slot = i % 2

    @pl.when(i >= 2)
    def _():
        _wait_block(y_hbm, obuf, sem, slot, paths, to_hbm=True)

    sub_time = TIME_BLOCK // MLP_SLICES
    for sub in range(MLP_SLICES):
        x = x_ref[sub * sub_time * SEQ_TILE:(sub + 1) * sub_time * SEQ_TILE, :]
        h = _modulate(_layer_norm(x), 1.0 + mod_ref[4], mod_ref[3]).astype(BF16)
        f = jnp.zeros(x.shape, F32)
        for k in range(D_FF // D_MODEL):
            cols = slice(k * D_MODEL, (k + 1) * D_MODEL)
            a = jnp.square(jax.nn.relu(_dot(h, w1_ref[:, cols]) + b1_ref[:, cols]))
            f = f + _dot(a.astype(BF16), w2_ref[cols, :])
        y = DEEPNORM_ALPHA * x + _modulate(f + b2_ref[...], mod_ref[5])
        obuf[slot, sub * sub_time:(sub + 1) * sub_time] = (
            _layer_norm(y) * g2_ref[...] + be2_ref[...]).reshape(sub_time, SEQ_TILE, D_MODEL)

    _start_block(y_hbm, obuf, sem, i, slot, paths, to_hbm=True)

    @pl.when(i == n - 1)
    def _():
        @pl.when(n >= 2)
        def _():
            _wait_block(y_hbm, obuf, sem, 1 - slot, paths, to_hbm=True)
        _wait_block(y_hbm, obuf, sem, slot, paths, to_hbm=True)


def _mlp(x1, paths, n_blocks, mods, w1, b1, w2, b2, g2, be2):
    const2 = lambda i: (0, 0)
    vec = pl.BlockSpec((1, D_MODEL), const2)
    once = pl.Buffered(1)
    return pl.pallas_call(
        functools.partial(_mlp_kernel, paths=paths),
        grid=(n_blocks,),
        in_specs=[
            _row_spec(D_MODEL, n_blocks),
            pl.BlockSpec((N_MOD, SEQ_TILE, D_MODEL), lambda i: (0, _mod_block(i, paths), 0)),
            pl.BlockSpec((D_MODEL, D_FF), const2, pipeline_mode=once),
            pl.BlockSpec((1, D_FF), const2),
            pl.BlockSpec((D_FF, D_MODEL), const2, pipeline_mode=once),
            vec, vec, vec,
        ],
        out_specs=[pl.BlockSpec(memory_space=pl.ANY)] * len(paths),
        out_shape=[jax.ShapeDtypeStruct((p.n_seq, p.seq_len, D_MODEL), F32) for p in paths],
        scratch_shapes=[pltpu.VMEM((2, TIME_BLOCK, SEQ_TILE, D_MODEL), F32),
                        pltpu.SemaphoreType.DMA((2,))],
        compiler_params=pltpu.CompilerParams(dimension_semantics=("arbitrary",),
                                             vmem_limit_bytes=VMEM_LIMIT),
        name="mlp",
    )(x1, mods, w1, b1, w2, b2, g2, be2)


def _state_cols(x_f, x_b):
    return jnp.concatenate([x_f, x_b, x_f, x_b], axis=-1)


def kernel(x_prompt, x_sample, state_s5, c, c_ctx, w_ada, b_ada, w_in, b_in, s5_lam_re, s5_lam_im, s5_log_dt, s5_b_re, s5_b_im, s5_c_re, s5_c_im, s5_d, w_glu, b_glu, w_proj_a, w_pool, pool_scale, w_proj_b, w_out, b_out, ln1_g, ln1_b, w_mlp1, b_mlp1, w_mlp2, b_mlp2, ln2_g, ln2_b):
    assert w_in.shape[0] == 1, "single-layer backbone"
    n_ctx = x_prompt.shape[0]
    n_lat = x_sample.shape[0]
    g, p, hh = N_GROUPS, S5_STATE, S5_GROUP
    paths, n_blocks = _paths([(x_prompt.shape, x_prompt.shape[1], n_lat // SEQ_TILE, False),
                              (x_sample.shape, GRID_W, 0, True)])
    mods = _mods(c, c_ctx[None, :], w_ada[0], b_ada[0][None, :])

    dirs = lambda x: _state_cols(x[0], x[1])
    rows = jnp.stack(
        [dirs(s5_lam_re[0]), dirs(s5_lam_im[0]),
         dirs(jnp.broadcast_to(s5_log_dt[0][:, :, None], (2, g, p))),
         jnp.tile(s5_d[0, 0].reshape(g, hh), (1, CHUNK)),
         jnp.tile(s5_d[0, 1].reshape(g, hh), (1, CHUNK))], axis=1)
    mats = jnp.stack(
        [dirs(s5_b_re[0].transpose(0, 1, 3, 2)), dirs(s5_b_im[0].transpose(0, 1, 3, 2)),
         dirs(s5_c_re[0]), dirs(s5_c_im[0])], axis=1)
    mt_mat, wb_mat, wct_mat, lam16, w_in_b = _s5_prep(rows, mats, [w_in[0]])

    xt, ua, ub, sg = _in_proj((x_prompt, x_sample), paths, n_blocks, mods, w_in_b,
                              b_in[0][None, :])

    st = state_s5[:, 0].astype(F32)
    s0 = jnp.concatenate([st[:, 0], st[:, 1]], axis=-1).transpose(2, 1, 0, 3)
    v, fin, w_glu_b, w_pa_b, w_pool_b, w_pb_b, w_out_b, w1_b, w2_b = _s5_core(
        ua, paths, mt_mat, wb_mat, wct_mat, lam16, s0,
        [w_glu[0], w_proj_a[0], w_pool[0].reshape(D_POOL, POOL_GROUP), w_proj_b[0], w_out[0],
         w_mlp1[0], w_mlp2[0]])
    new_state = fin.reshape(n_ctx, 1, 2, 2, g, p)

    x1 = _mix_out(xt, v, ub, sg, paths, n_blocks, mods,
                  w_glu_b, b_glu[0][None, :], w_pa_b, w_pool_b.reshape(w_pool.shape[1:]),
                  pool_scale[0][None, :], w_pb_b, w_out_b, b_out[0][None, :],
                  ln1_g[0][None, :], ln1_b[0][None, :])
    y_p, y_s = _mlp(x1, paths, n_blocks, mods, w1_b, b_mlp1[0][None, :], w2_b,
                    b_mlp2[0][None, :], ln2_g[0][None, :], ln2_b[0][None, :])
    return (y_p, y_s, new_state)
```

```python
import functools
from typing import NamedTuple

import jax
import jax.numpy as jnp
from jax import lax
from jax.experimental import pallas as pl
from jax.experimental.pallas import tpu as pltpu

F32 = jnp.float32
BF16 = jnp.bfloat16

D_MODEL = 1024
S5_GROUP = 16
N_GROUPS = D_MODEL // S5_GROUP
S5_STATE = 64
D_POOL = D_MODEL // 2
POOL_WINDOWS = (2, 4, 8, 16)
POOL_GROUP = D_POOL // len(POOL_WINDOWS)
D_IN = D_MODEL + D_POOL + 2 * D_MODEL
D_REST = D_IN - D_MODEL
D_FF = 4 * D_MODEL
N_MOD = 6
GRID_W = 64
DEEPNORM_ALPHA = 2.0 ** 0.25
LN_EPS = 1e-6

CHUNK = 16
CW = CHUNK * S5_GROUP
SW = 4 * S5_STATE
LANES = 128
N_STRIPS = D_MODEL // LANES
GROUPS_PER_STRIP = LANES // S5_GROUP
GROUP_UNROLL = 8
N_POW = 32
SEQ_TILE = 8
TIME_BLOCK = 128
TOK_BLOCK = SEQ_TILE * TIME_BLOCK
HALO = max(POOL_WINDOWS) // 2
IN_SLICES = 4
MLP_SLICES = 4
MIX_SLICES = 4
VMEM_LIMIT = 56 * 1024 * 1024


def _time_major(x3):
    return jnp.swapaxes(x3, 0, 1).reshape(x3.shape[1] * SEQ_TILE, x3.shape[-1])


def _seq_major(x2):
    return jnp.swapaxes(x2.reshape(x2.shape[0] // SEQ_TILE, SEQ_TILE, x2.shape[-1]), 0, 1)


def _modulate(a, mul, add=None):
    a3 = a.reshape(a.shape[0] // SEQ_TILE, SEQ_TILE, a.shape[-1]) * mul[None]
    if add is not None:
        a3 = a3 + add[None]
    return a3.reshape(a.shape)


def _layer_norm(x):
    mu = jnp.mean(x, axis=-1, keepdims=True)
    xc = x - mu
    var = jnp.mean(xc * xc, axis=-1, keepdims=True)
    return xc * lax.rsqrt(var + LN_EPS)


def _sigmoid(x):
    return 0.5 * jnp.tanh(0.5 * x) + 0.5


def _dot(a, b):
    return jnp.dot(a, b, preferred_element_type=F32)


def _dot_nt(a, b, precision=None):
    return lax.dot_general(a, b, (((1,), (1,)), ((), ())), precision=precision,
                           preferred_element_type=F32)


def _mods_kernel(c_ref, cctx_ref, w_ref, b_ref, o_ref):
    cvec = jnp.concatenate([c_ref[...], jnp.broadcast_to(cctx_ref[...], (SEQ_TILE, D_MODEL))],
                           axis=0)
    o_ref[0] = jnp.dot(jax.nn.silu(cvec), w_ref[...], precision=lax.Precision.HIGHEST,
                       preferred_element_type=F32) + b_ref[...]


def _mods(c, c_ctx, w_ada, b_ada):
    n_lat = c.shape[0]
    assert n_lat % SEQ_TILE == 0
    rows = n_lat + SEQ_TILE
    n_out = w_ada.shape[1]
    return pl.pallas_call(
        _mods_kernel,
        grid=(n_out // D_MODEL,),
        in_specs=[
            pl.BlockSpec((n_lat, D_MODEL), lambda j: (0, 0)),
            pl.BlockSpec((1, D_MODEL), lambda j: (0, 0)),
            pl.BlockSpec((D_MODEL, D_MODEL), lambda j: (0, j)),
            pl.BlockSpec((1, D_MODEL), lambda j: (0, j)),
        ],
        out_specs=pl.BlockSpec((1, rows, D_MODEL), lambda j: (j, 0, 0)),
        out_shape=jax.ShapeDtypeStruct((n_out // D_MODEL, rows, D_MODEL), F32),
        name="mods",
    )(c, c_ctx, w_ada, b_ada)


def _cast_specs(weights, n_steps, step_index):
    specs, shapes = [], []
    for w in weights:
        rows = w.shape[0] // n_steps
        assert rows * n_steps == w.shape[0] and rows % 16 == 0
        specs.append(pl.BlockSpec((rows, w.shape[1]), lambda *g: (step_index(*g), 0)))
        shapes.append(jax.ShapeDtypeStruct(w.shape, BF16))
    return specs, shapes


def _cast_chunks(src_refs, dst_refs):
    for src, dst in zip(src_refs, dst_refs):
        dst[...] = src[...].astype(BF16)


def _s5_prep_kernel(*refs, n_cast):
    rows_ref, mats_ref = refs[:2]
    mt_ref, wb_ref, wct_ref, lam_ref = refs[2 + n_cast:6 + n_cast]
    _cast_chunks(refs[2:2 + n_cast], refs[6 + n_cast:6 + 2 * n_cast])
    _s5_prep_body(rows_ref, mats_ref, mt_ref, wb_ref, wct_ref, lam_ref, *refs[6 + 2 * n_cast:])


def _s5_prep_body(rows_ref, mats_ref, mt_ref, wb_ref, wct_ref, lam_ref,
                  pwr_ref, pwi_ref, ge_ref):
    pg = GROUPS_PER_STRIP
    lre = rows_ref[:, 0:1, :]
    lim = rows_ref[:, 1:2, :]
    dt = jnp.exp(rows_ref[:, 2:3, :])
    a = lre * dt
    b = lim * dt
    col = lax.broadcasted_iota(jnp.int32, (1, 1, SW), 2)
    is_im = col >= 2 * S5_STATE
    is_b = (col & (2 * S5_STATE - 1)) >= S5_STATE

    mag = jnp.exp(a)
    sq_r = mag * jnp.cos(b)
    sq_i = mag * jnp.sin(b)
    pwr_ref[:, 0:1, :] = jnp.ones((pg, 1, SW), F32)
    pwi_ref[:, 0:1, :] = jnp.zeros((pg, 1, SW), F32)
    m = 1
    while m < N_POW:
        lo_r = pwr_ref[:, 0:m, :]
        lo_i = pwi_ref[:, 0:m, :]
        pwr_ref[:, m:2 * m, :] = lo_r * sq_r - lo_i * sq_i
        pwi_ref[:, m:2 * m, :] = lo_r * sq_i + lo_i * sq_r
        sq_r, sq_i = sq_r * sq_r - sq_i * sq_i, 2.0 * sq_r * sq_i
        m *= 2

    def power(kf, kb):
        def row(ref, kk):
            return jnp.zeros((pg, 1, SW), F32) if kk is None else ref[:, kk:kk + 1, :]
        if kf == kb:
            return row(pwr_ref, kf), row(pwi_ref, kf)
        return (jnp.where(is_b, row(pwr_ref, kb), row(pwr_ref, kf)),
                jnp.where(is_b, row(pwi_ref, kb), row(pwi_ref, kf)))

    lbr = pwr_ref[:, 1:2, :]
    lbi = pwi_ref[:, 1:2, :]
    den = lre * lre + lim * lim
    nr = lbr - 1.0
    cr = (nr * lre + lbi * lim) / den
    ci = (lbi * lre - nr * lim) / den
    bre = mats_ref[:, 0]
    bim = mats_ref[:, 1]
    bbr = cr * bre - ci * bim
    bbi = cr * bim + ci * bre
    bx = jnp.where(is_im, bbi, bbr)
    by = jnp.where(is_im, bbr, -bbi)
    cre = mats_ref[:, 2]
    cim = mats_ref[:, 3]
    cx = jnp.where(is_im, -cim, cre)
    cy = jnp.where(is_im, -cre, -cim)

    for t in range(CHUNK):
        rows = slice(t * S5_GROUP, (t + 1) * S5_GROUP)
        pr, pi = power(CHUNK - 1 - t, t)
        wb_ref[:, rows, :] = (pr * bx + pi * by).astype(BF16)
        pr, pi = power(t + 1, CHUNK - t)
        wct_ref[:, rows, :] = (pr * cx + pi * cy).astype(BF16)

    for j in range(2 * CHUNK):
        rows = slice(j * S5_GROUP, (j + 1) * S5_GROUP)
        if j == 2 * CHUNK - 1:
            ge_ref[:, rows, :] = jnp.zeros((pg, S5_GROUP, SW), F32)
            continue
        pr, pi = power(j - (CHUNK - 1) if j >= CHUNK - 1 else None,
                       (CHUNK - 1) - j if j <= CHUNK - 1 else None)
        ge_ref[:, rows, :] = pr * cx + pi * cy

    dsum = rows_ref[:, 3:4, :] + rows_ref[:, 4:5, :]
    r16 = lax.broadcasted_iota(jnp.int32, (S5_GROUP, CW), 0)
    c16 = lax.broadcasted_iota(jnp.int32, (S5_GROUP, CW), 1)
    for gi in range(pg):
        e = _dot_nt(bx[gi], ge_ref[gi], precision=lax.Precision.HIGHEST)
        blocks = []
        for tau in range(CHUNK):
            start = (CHUNK - 1 - tau) * S5_GROUP
            blocks.append(e[:, start:start + CW]
                          + jnp.where(c16 == r16 + tau * S5_GROUP, dsum[gi], 0.0))
        mt_ref[gi] = jnp.concatenate(blocks, axis=0).T.astype(BF16)

    lam_ref[...] = jnp.concatenate(
        [pwr_ref[:, CHUNK:CHUNK + 1, :], pwi_ref[:, CHUNK:CHUNK + 1, :],
         jnp.zeros((pg, 6, SW), F32)], axis=1)


def _s5_prep(rows, mats, cast_weights):
    g = N_GROUPS
    pg = GROUPS_PER_STRIP
    n_steps = g // pg
    mat_out = pl.BlockSpec((pg, CW, SW), lambda i: (i, 0, 0))
    cast_specs, cast_shapes = _cast_specs(cast_weights, n_steps, lambda i: i)
    return pl.pallas_call(
        functools.partial(_s5_prep_kernel, n_cast=len(cast_weights)),
        grid=(n_steps,),
        in_specs=[pl.BlockSpec((pg,) + rows.shape[1:], lambda i: (i, 0, 0)),
                  pl.BlockSpec((pg,) + mats.shape[1:], lambda i: (i, 0, 0, 0))] + cast_specs,
        out_specs=[mat_out, mat_out, mat_out,
                   pl.BlockSpec((pg, 8, SW), lambda i: (i, 0, 0))] + cast_specs,
        out_shape=[jax.ShapeDtypeStruct((g, CW, CW), BF16),
                   jax.ShapeDtypeStruct((g, CW, SW), BF16),
                   jax.ShapeDtypeStruct((g, CW, SW), BF16),
                   jax.ShapeDtypeStruct((g, 8, SW), F32)] + cast_shapes,
        scratch_shapes=[pltpu.VMEM((pg, N_POW, SW), F32), pltpu.VMEM((pg, N_POW, SW), F32),
                        pltpu.VMEM((pg, 2 * CW, SW), F32)],
        name="s5_prep",
    )(rows, mats, *cast_weights)


class _Path(NamedTuple):
    n_seq: int
    seq_len: int
    pool_n: int
    first_block: int
    mod_block: int
    mod_per_seq: bool

    @property
    def time_blocks(self):
        return self.seq_len // TIME_BLOCK

    @property
    def n_blocks(self):
        return (self.n_seq // SEQ_TILE) * self.time_blocks


def _paths(specs):
    paths, first = [], 0
    for (n_seq, seq_len, _), pool_n, mod_block, mod_per_seq in specs:
        assert n_seq % SEQ_TILE == 0 and seq_len % TIME_BLOCK == 0
        sub_time = TIME_BLOCK // MIX_SLICES
        assert pool_n % sub_time == 0 and seq_len % pool_n == 0 and HALO <= sub_time
        assert pool_n & (pool_n - 1) == 0
        paths.append(_Path(n_seq, seq_len, pool_n, first, mod_block, mod_per_seq))
        first += paths[-1].n_blocks
    return tuple(paths), first


def _block_coords(i, paths):
    coords = []
    for k, p in enumerate(paths):
        end = p.first_block + p.n_blocks
        inside = (i >= p.first_block) & (i < end)
        j = jnp.clip(i - p.first_block, 0, p.n_blocks - 1)
        coords.append((inside, (j // p.time_blocks) * SEQ_TILE, (j % p.time_blocks) * TIME_BLOCK))
    return coords


def _mod_block(i, paths):
    blk = 0
    for p in paths:
        j = jnp.clip(i - p.first_block, 0, p.n_blocks - 1)
        seq_tile = j // p.time_blocks if p.mod_per_seq else 0
        blk = jnp.where(i >= p.first_block, p.mod_block + seq_tile, blk)
    return blk


def _row_spec(width, n_blocks, shift=0):
    return pl.BlockSpec((TOK_BLOCK, width),
                        lambda i: (jnp.clip(i + shift, 0, n_blocks - 1), 0))


def _block_copies(hbm_refs, buf, sem, i, slot, paths, to_hbm):
    for (inside, seq0, t0), hbm in zip(_block_coords(i, paths), hbm_refs):
        copies = []
        for s in range(SEQ_TILE):
            rows = hbm.at[seq0 + s, pl.ds(t0, TIME_BLOCK), :]
            tile_rows = buf.at[slot, :, s, :]
            src, dst = (tile_rows, rows) if to_hbm else (rows, tile_rows)
            copies.append(pltpu.make_async_copy(src, dst, sem.at[slot]))
        yield inside, copies


def _start_block(hbm_refs, buf, sem, i, slot, paths, to_hbm):
    for inside, copies in _block_copies(hbm_refs, buf, sem, i, slot, paths, to_hbm):
        @pl.when(inside)
        def _():
            for c in copies:
                c.start()


def _wait_block(hbm_refs, buf, sem, slot, paths, to_hbm):
    _, copies = next(_block_copies(hbm_refs, buf, sem, 0, slot, paths, to_hbm))
    for c in copies:
        c.wait()


def _fetch_x_block(x_hbm, xbuf, sem, paths):
    i = pl.program_id(0)
    slot = i % 2

    @pl.when(i == 0)
    def _():
        _start_block(x_hbm, xbuf, sem, i, slot, paths, to_hbm=False)

    @pl.when(i + 1 < pl.num_programs(0))
    def _():
        _start_block(x_hbm, xbuf, sem, i + 1, 1 - slot, paths, to_hbm=False)

    _wait_block(x_hbm, xbuf, sem, slot, paths, to_hbm=False)
    return slot


def _in_proj_kernel(*refs, paths):
    n_paths = len(paths)
    x_hbm = refs[:n_paths]
    mod_ref, w_ref, b_ref, ua_ref, ub_ref, sg_ref, xbuf, sem = refs[n_paths:]
    slot = _fetch_x_block(x_hbm, xbuf, sem, paths)
    sub_time = TIME_BLOCK // IN_SLICES
    for sub in range(IN_SLICES):
        times = slice(sub * sub_time, (sub + 1) * sub_time)
        rows = slice(sub * sub_time * SEQ_TILE, (sub + 1) * sub_time * SEQ_TILE)
        x = xbuf[slot, times].reshape(sub_time * SEQ_TILE, D_MODEL)
        h = _modulate(_layer_norm(x), 1.0 + mod_ref[1], mod_ref[0])
        z = _dot(h.astype(BF16), w_ref[...]) + b_ref[...]
        ua_ref[times] = z[:, :D_MODEL].reshape(sub_time, SEQ_TILE, D_MODEL)
        ub_ref[rows, :] = z[:, D_MODEL:D_MODEL + D_POOL].astype(BF16)
        sg_ref[rows, :] = _sigmoid(z[:, D_MODEL + D_POOL:]).astype(BF16)


def _in_proj(xs, paths, n_blocks, mods, w_in, b_in):
    n_tok = n_blocks * TOK_BLOCK
    const = lambda i: (0, 0)
    once = pl.Buffered(1)
    return pl.pallas_call(
        functools.partial(_in_proj_kernel, paths=paths),
        grid=(n_blocks,),
        in_specs=[pl.BlockSpec(memory_space=pl.ANY)] * len(xs) + [
            pl.BlockSpec((N_MOD, SEQ_TILE, D_MODEL), lambda i: (0, _mod_block(i, paths), 0)),
            pl.BlockSpec((D_MODEL, D_IN), const, pipeline_mode=once),
            pl.BlockSpec((1, D_IN), const),
        ],
        out_specs=[pl.BlockSpec((TIME_BLOCK, SEQ_TILE, D_MODEL), lambda i: (i, 0, 0)),
                   _row_spec(D_POOL, n_blocks), _row_spec(2 * D_MODEL, n_blocks)],
        out_shape=[jax.ShapeDtypeStruct((n_blocks * TIME_BLOCK, SEQ_TILE, D_MODEL), F32),
                   jax.ShapeDtypeStruct((n_tok, D_POOL), BF16),
                   jax.ShapeDtypeStruct((n_tok, 2 * D_MODEL), BF16)],
        scratch_shapes=[pltpu.VMEM((2, TIME_BLOCK, SEQ_TILE, D_MODEL), F32),
                        pltpu.SemaphoreType.DMA((2,))],
        compiler_params=pltpu.CompilerParams(dimension_semantics=("arbitrary",),
                                             vmem_limit_bytes=VMEM_LIMIT),
        name="in_proj",
    )(*xs, mods, w_in, b_in)


def _s5_core_kernel(*refs, paths, n_cast):
    ua_ref, mt_ref, wb_ref, wct_ref, lam_ref, s0_ref = refs[:6]
    v_ref, fin_ref = refs[6 + n_cast:8 + n_cast]
    _cast_chunks(refs[6:6 + n_cast], refs[8 + n_cast:8 + 2 * n_cast])
    xt_ref, yt_ref, sloc_ref, sprf_ref, sprb_ref, fsc_ref = refs[8 + 2 * n_cast:]
    for k, p in enumerate(paths):
        @pl.when(pl.program_id(1) == k)
        def _():
            _s5_strip(ua_ref, mt_ref, wb_ref, wct_ref, lam_ref, s0_ref, v_ref, fin_ref,
                      xt_ref, yt_ref, sloc_ref, sprf_ref, sprb_ref, fsc_ref,
                      n_seq=p.n_seq, seq_len=p.seq_len, has_init=k > 0)


def _s5_strip(ua_ref, mt_ref, wb_ref, wct_ref, lam_ref, s0_ref, v_ref, fin_ref,
              xt_ref, yt_ref, sloc_ref, sprf_ref, sprb_ref, fsc_ref, *, n_seq, seq_len, has_init):
    n_chunk = seq_len // CHUNK
    n_sb = n_seq // SEQ_TILE
    half = 2 * S5_STATE

    for tau in range(CHUNK):
        x_tau = jnp.concatenate([ua_ref[sb * seq_len + c * CHUNK + tau]
                                 for c in range(n_chunk) for sb in range(n_sb)], axis=0)
        xt = x_tau.astype(BF16).T
        for gi in range(GROUPS_PER_STRIP):
            xt_ref[gi, tau * S5_GROUP:(tau + 1) * S5_GROUP, :] = xt[gi * S5_GROUP:(gi + 1) * S5_GROUP, :]

    is_f = lax.broadcasted_iota(jnp.int32, (1, half), 1) < S5_STATE
    col = lax.broadcasted_iota(jnp.int32, (1, SW), 1)
    col_is_f = (col & (half - 1)) < S5_STATE

    def one_group(gi, slot):
        xg = xt_ref[gi]
        u = xg.T
        sloc_ref[slot] = _dot(u, wb_ref[gi])
        ar = lam_ref[gi, 0:1, 0:half]
        ai = lam_ref[gi, 1:2, 0:half]
        if has_init:
            s_re = s0_ref[gi, 0]
            s_im = s0_ref[gi, 1]
        else:
            s_re = jnp.zeros((n_seq, half), F32)
            s_im = jnp.zeros((n_seq, half), F32)
        for c in range(n_chunk):
            rf = pl.ds(c * n_seq, n_seq)
            rb = pl.ds((n_chunk - 1 - c) * n_seq, n_seq)
            sprf_ref[slot, rf, 0:half] = s_re
            sprf_ref[slot, rf, half:SW] = s_im
            sprb_ref[slot, rb, 0:half] = s_re
            sprb_ref[slot, rb, half:SW] = s_im
            l_re = jnp.where(is_f, sloc_ref[slot, rf, 0:half], sloc_ref[slot, rb, 0:half])
            l_im = jnp.where(is_f, sloc_ref[slot, rf, half:SW], sloc_ref[slot, rb, half:SW])
            s_re, s_im = (ar * s_re - ai * s_im + l_re,
                          ar * s_im + ai * s_re + l_im)
        if not has_init:
            fsc_ref[gi, 0] = s_re
            fsc_ref[gi, 1] = s_im
        sprev = jnp.where(col_is_f, sprf_ref[slot], sprb_ref[slot]).astype(BF16)
        yt = _dot(mt_ref[gi], xg) + _dot_nt(wct_ref[gi], sprev)
        yt_ref[gi] = jax.nn.gelu(yt)

    def group_body(i, carry):
        for slot in range(GROUP_UNROLL):
            one_group(i * GROUP_UNROLL + slot, slot)
        return carry

    lax.fori_loop(0, GROUPS_PER_STRIP // GROUP_UNROLL, group_body, 0)

    if not has_init:
        for part in range(2):
            fs = jnp.swapaxes(fsc_ref[:, part], 0, 1)
            fin_ref[:, part] = fs[:, :, 0:S5_STATE]
            fin_ref[:, 2 + part] = fs[:, :, S5_STATE:half]

    for t in range(CHUNK):
        vt = yt_ref[:, t * S5_GROUP:(t + 1) * S5_GROUP, :].reshape(LANES, n_seq * n_chunk)
        v_t = vt.T
        for c in range(n_chunk):
            for sb in range(n_sb):
                r0 = (c * n_sb + sb) * SEQ_TILE
                v_ref[sb * seq_len + c * CHUNK + t] = v_t[r0:r0 + SEQ_TILE, :]


def _s5_core(ua, paths, m, wb, wct, lam, s0, cast_weights):
    ctx, lat = paths
    n_steps = N_STRIPS * len(paths)
    cast_specs, cast_shapes = _cast_specs(cast_weights, n_steps, lambda j, k: j * len(paths) + k)
    tiles = ctx.n_blocks * TIME_BLOCK
    rows = tiles * SEQ_TILE // CHUNK
    assert lat.n_blocks * TIME_BLOCK == tiles and s0.shape[2] == lat.n_seq
    gps = GROUPS_PER_STRIP
    strip = pl.BlockSpec((tiles, SEQ_TILE, LANES), lambda j, k: (k, 0, j))
    mat = pl.BlockSpec((gps, CW, SW), lambda j, k: (j, 0, 0))
    return pl.pallas_call(
        functools.partial(_s5_core_kernel, paths=paths, n_cast=len(cast_weights)),
        grid=(N_STRIPS, len(paths)),
        in_specs=[strip, mat, mat, mat, pl.BlockSpec((gps, 8, SW), lambda j, k: (j, 0, 0)),
                  pl.BlockSpec((gps, 2, lat.n_seq, 2 * S5_STATE), lambda j, k: (j, 0, 0, 0))
                  ] + cast_specs,
        out_specs=[strip,
                   pl.BlockSpec((ctx.n_seq, 4, gps, S5_STATE), lambda j, k: (0, 0, j, 0))
                   ] + cast_specs,
        out_shape=[jax.ShapeDtypeStruct(ua.shape, F32),
                   jax.ShapeDtypeStruct((ctx.n_seq, 4, N_GROUPS, S5_STATE), F32)] + cast_shapes,
        scratch_shapes=[pltpu.VMEM((gps, CW, rows), BF16), pltpu.VMEM((gps, CW, rows), F32),
                        pltpu.VMEM((GROUP_UNROLL, rows, SW), F32),
                        pltpu.VMEM((GROUP_UNROLL, rows, SW), F32),
                        pltpu.VMEM((GROUP_UNROLL, rows, SW), F32),
                        pltpu.VMEM((gps, 2, ctx.n_seq, 2 * S5_STATE), F32)],
        compiler_params=pltpu.CompilerParams(dimension_semantics=("arbitrary", "arbitrary"),
                                             vmem_limit_bytes=VMEM_LIMIT),
        name="s5_core",
    )(ua, m, wb, wct, lam, s0, *cast_weights)


def _mix_out_kernel(*refs, paths):
    n_paths = len(paths)
    x_hbm = refs[:n_paths]
    (v_ref, ubp_ref, ub_ref, ubn_ref, sg_ref, mod_ref, wglu_ref, bglu_ref, wpa_ref, wpool_ref,
     pscale_ref, wpb_ref, wout_ref, bout_ref, g1_ref, b1_ref, o_ref, xbuf, sem) = refs[n_paths:]
    x_slot = _fetch_x_block(x_hbm, xbuf, sem, paths)
    i = pl.program_id(0)
    t_block, run_mask = 0, 0
    for p in paths:
        j = jnp.clip(i - p.first_block, 0, p.n_blocks - 1)
        t_block = jnp.where(i >= p.first_block, (j % p.time_blocks) * TIME_BLOCK, t_block)
        run_mask = jnp.where(i >= p.first_block, p.pool_n - 1, run_mask)
    halo_rows = HALO * SEQ_TILE
    ext = jnp.concatenate([ubp_ref[...], ub_ref[...], ubn_ref[...]], axis=0).astype(F32)

    sub_rows = TOK_BLOCK // MIX_SLICES
    sub_time = TIME_BLOCK // MIX_SLICES
    slices = [slice(sub * sub_rows, (sub + 1) * sub_rows) for sub in range(MIX_SLICES)]

    def s5_branch(sub):
        tiles = sub_rows // SEQ_TILE
        v = v_ref[sub * tiles:(sub + 1) * tiles].reshape(sub_rows, D_MODEL)
        glu = v * _sigmoid(_dot(v.astype(BF16), wglu_ref[...]) + bglu_ref[...])
        return _dot(glu.astype(BF16), wpa_ref[...])

    def pool_branch(sub):
        pos = (t_block + sub * sub_time) & run_mask
        first = pos == 0
        last = pos + sub_time == run_mask + 1
        base = sub * sub_rows
        win = jnp.concatenate(
            [jnp.where(first, 0.0, ext[base:base + halo_rows]),
             ext[base + halo_rows:base + halo_rows + sub_rows],
             jnp.where(last, 0.0, ext[base + halo_rows + sub_rows:base + 2 * halo_rows + sub_rows])],
            axis=0)
        t_idx = lax.broadcasted_iota(jnp.int32, (sub_rows, POOL_GROUP), 0) // SEQ_TILE
        pooled = []
        for gi, w in enumerate(POOL_WINDOWS):
            cols = slice(gi * POOL_GROUP, (gi + 1) * POOL_GROUP)
            acc = jnp.zeros((sub_rows, POOL_GROUP), F32)
            for k in range(-(w // 2), w - w // 2):
                r0 = halo_rows + k * SEQ_TILE
                acc = acc + win[r0:r0 + sub_rows, cols]
            lo = jnp.where(first, jnp.maximum(t_idx - w // 2, 0), t_idx - w // 2)
            hi = jnp.where(last, jnp.minimum(t_idx - w // 2 + w, sub_time), t_idx - w // 2 + w)
            p = acc / (hi - lo).astype(F32) - win[halo_rows:halo_rows + sub_rows, cols]
            pooled.append(_dot(p.astype(BF16), wpool_ref[gi]))
        pm = jnp.concatenate(pooled, axis=1) * pscale_ref[...]
        return _dot(pm.astype(BF16), wpb_ref[...])

    ya = [s5_branch(sub) for sub in range(MIX_SLICES)]
    yb = [pool_branch(sub) for sub in range(MIX_SLICES)]
    tm = []
    for sub, rows in enumerate(slices):
        merged = (sg_ref[rows, 0:D_MODEL].astype(F32) * ya[sub]
                  + sg_ref[rows, D_MODEL:2 * D_MODEL].astype(F32) * yb[sub])
        tm.append(_dot(merged.astype(BF16), wout_ref[...]) + bout_ref[...])
    for sub, rows in enumerate(slices):
        x = xbuf[x_slot, sub * sub_time:(sub + 1) * sub_time].reshape(sub_rows, D_MODEL)
        y = DEEPNORM_ALPHA * x + _modulate(tm[sub], mod_ref[2])
        o_ref[rows, :] = _layer_norm(y) * g1_ref[...] + b1_ref[...]


def _mix_out(xs, v, ub, sg, paths, n_blocks, mods, wglu, bglu, wpa, wpool, pscale, wpb, wout,
             bout, g1, b1):
    const2 = lambda i: (0, 0)
    once = pl.Buffered(1)
    vec = pl.BlockSpec((1, D_MODEL), const2)
    sq = pl.BlockSpec((D_MODEL, D_MODEL), const2, pipeline_mode=once)
    rows = functools.partial(_row_spec, n_blocks=n_blocks)
    halo_rows = HALO * SEQ_TILE
    halos_per_block = TOK_BLOCK // halo_rows
    n_halo_blocks = n_blocks * halos_per_block
    halo = lambda index: pl.BlockSpec((halo_rows, D_POOL), lambda i: (index(i), 0))
    return pl.pallas_call(
        functools.partial(_mix_out_kernel, paths=paths),
        grid=(n_blocks,),
        in_specs=[pl.BlockSpec(memory_space=pl.ANY)] * len(xs) + [
            pl.BlockSpec((TIME_BLOCK, SEQ_TILE, D_MODEL), lambda i: (i, 0, 0)),
            halo(lambda i: jnp.maximum(i * halos_per_block - 1, 0)), rows(D_POOL),
            halo(lambda i: jnp.minimum((i + 1) * halos_per_block, n_halo_blocks - 1)),
            rows(2 * D_MODEL),
            pl.BlockSpec((N_MOD, SEQ_TILE, D_MODEL), lambda i: (0, _mod_block(i, paths), 0)),
            sq, vec, sq,
            pl.BlockSpec((len(POOL_WINDOWS), POOL_GROUP, POOL_GROUP), lambda i: (0, 0, 0)),
            pl.BlockSpec((1, D_POOL), const2),
            pl.BlockSpec((D_POOL, D_MODEL), const2, pipeline_mode=once),
            sq, vec, vec, vec,
        ],
        out_specs=rows(D_MODEL),
        out_shape=jax.ShapeDtypeStruct((n_blocks * TOK_BLOCK, D_MODEL), F32),
        scratch_shapes=[pltpu.VMEM((2, TIME_BLOCK, SEQ_TILE, D_MODEL), F32),
                        pltpu.SemaphoreType.DMA((2,))],
        compiler_params=pltpu.CompilerParams(dimension_semantics=("arbitrary",),
                                             vmem_limit_bytes=VMEM_LIMIT),
        name="mix_out",
    )(*xs, v, ub, ub, ub, sg, mods, wglu, bglu, wpa, wpool, pscale, wpb, wout, bout, g1, b1)


def _mlp_kernel(*refs, paths):
    n_paths = len(paths)
    x_ref, mod_ref, w1_ref, b1_ref, w2_ref, b2_ref, g2_ref, be2_ref = refs[:8]
    y_hbm = refs[8:8 + n_paths]
    obuf, sem = refs[8 + n_paths:]
    i = pl.program_id(0)
    n = pl.num_programs(0)
    slot = i % 2

    @pl.when(i >= 2)
    def _():
        _wait_block(y_hbm, obuf, sem, slot, paths, to_hbm=True)

    sub_time = TIME_BLOCK // MLP_SLICES
    for sub in range(MLP_SLICES):
        x = x_ref[sub * sub_time * SEQ_TILE:(sub + 1) * sub_time * SEQ_TILE, :]
        h = _modulate(_layer_norm(x), 1.0 + mod_ref[4], mod_ref[3]).astype(BF16)
        f = jnp.zeros(x.shape, F32)
        for k in range(D_FF // D_MODEL):
            cols = slice(k * D_MODEL, (k + 1) * D_MODEL)
            a = jnp.square(jax.nn.relu(_dot(h, w1_ref[:, cols]) + b1_ref[:, cols]))
            f = f + _dot(a.astype(BF16), w2_ref[cols, :])
        y = DEEPNORM_ALPHA * x + _modulate(f + b2_ref[...], mod_ref[5])
        obuf[slot, sub * sub_time:(sub + 1) * sub_time] = (
            _layer_norm(y) * g2_ref[...] + be2_ref[...]).reshape(sub_time, SEQ_TILE, D_MODEL)

    _start_block(y_hbm, obuf, sem, i, slot, paths, to_hbm=True)

    @pl.when(i == n - 1)
    def _():
        @pl.when(n >= 2)
        def _():
            _wait_block(y_hbm, obuf, sem, 1 - slot, paths, to_hbm=True)
        _wait_block(y_hbm, obuf, sem, slot, paths, to_hbm=True)


def _mlp(x1, paths, n_blocks, mods, w1, b1, w2, b2, g2, be2):
    const2 = lambda i: (0, 0)
    vec = pl.BlockSpec((1, D_MODEL), const2)
    once = pl.Buffered(1)
    return pl.pallas_call(
        functools.partial(_mlp_kernel, paths=paths),
        grid=(n_blocks,),
        in_specs=[
            _row_spec(D_MODEL, n_blocks),
            pl.BlockSpec((N_MOD, SEQ_TILE, D_MODEL), lambda i: (0, _mod_block(i, paths), 0)),
            pl.BlockSpec((D_MODEL, D_FF), const2, pipeline_mode=once),
            pl.BlockSpec((1, D_FF), const2),
            pl.BlockSpec((D_FF, D_MODEL), const2, pipeline_mode=once),
            vec, vec, vec,
        ],
        out_specs=[pl.BlockSpec(memory_space=pl.ANY)] * len(paths),
        out_shape=[jax.ShapeDtypeStruct((p.n_seq, p.seq_len, D_MODEL), F32) for p in paths],
        scratch_shapes=[pltpu.VMEM((2, TIME_BLOCK, SEQ_TILE, D_MODEL), F32),
                        pltpu.SemaphoreType.DMA((2,))],
        compiler_params=pltpu.CompilerParams(dimension_semantics=("arbitrary",),
                                             vmem_limit_bytes=VMEM_LIMIT),
        name="mlp",
    )(x1, mods, w1, b1, w2, b2, g2, be2)


def _state_cols(x_f, x_b):
    return jnp.concatenate([x_f, x_b, x_f, x_b], axis=-1)


def kernel(x_prompt, x_sample, state_s5, c, c_ctx, w_ada, b_ada, w_in, b_in, s5_lam_re, s5_lam_im, s5_log_dt, s5_b_re, s5_b_im, s5_c_re, s5_c_im, s5_d, w_glu, b_glu, w_proj_a, w_pool, pool_scale, w_proj_b, w_out, b_out, ln1_g, ln1_b, w_mlp1, b_mlp1, w_mlp2, b_mlp2, ln2_g, ln2_b):
    assert w_in.shape[0] == 1, "single-layer backbone"
    n_ctx = x_prompt.shape[0]
    n_lat = x_sample.shape[0]
    g, p, hh = N_GROUPS, S5_STATE, S5_GROUP
    paths, n_blocks = _paths([(x_prompt.shape, x_prompt.shape[1], n_lat // SEQ_TILE, False),
                              (x_sample.shape, GRID_W, 0, True)])
    mods = _mods(c, c_ctx[None, :], w_ada[0], b_ada[0][None, :])

    dirs = lambda x: _state_cols(x[0], x[1])
    rows = jnp.stack(
        [dirs(s5_lam_re[0]), dirs(s5_lam_im[0]),
         dirs(jnp.broadcast_to(s5_log_dt[0][:, :, None], (2, g, p))),
         jnp.tile(s5_d[0, 0].reshape(g, hh), (1, CHUNK)),
         jnp.tile(s5_d[0, 1].reshape(g, hh), (1, CHUNK))], axis=1)
    mats = jnp.stack(
        [dirs(s5_b_re[0].transpose(0, 1, 3, 2)), dirs(s5_b_im[0].transpose(0, 1, 3, 2)),
         dirs(s5_c_re[0]), dirs(s5_c_im[0])], axis=1)
    mt_mat, wb_mat, wct_mat, lam16, w_in_b = _s5_prep(rows, mats, [w_in[0]])

    xs = (x_prompt, x_sample)
    ua, ub, sg = _in_proj(xs, paths, n_blocks, mods, w_in_b, b_in[0][None, :])

    st = state_s5[:, 0].astype(F32)
    s0 = jnp.concatenate([st[:, 0], st[:, 1]], axis=-1).transpose(2, 1, 0, 3)
    v, fin, w_glu_b, w_pa_b, w_pool_b, w_pb_b, w_out_b, w1_b, w2_b = _s5_core(
        ua, paths, mt_mat, wb_mat, wct_mat, lam16, s0,
        [w_glu[0], w_proj_a[0], w_pool[0].reshape(D_POOL, POOL_GROUP), w_proj_b[0], w_out[0],
         w_mlp1[0], w_mlp2[0]])
    new_state = fin.reshape(n_ctx, 1, 2, 2, g, p)

    x1 = _mix_out(xs, v, ub, sg, paths, n_blocks, mods,
                  w_glu_b, b_glu[0][None, :], w_pa_b, w_pool_b.reshape(w_pool.shape[1:]),
                  pool_scale[0][None, :], w_pb_b, w_out_b, b_out[0][None, :],
                  ln1_g[0][None, :], ln1_b[0][None, :])
    y_p, y_s = _mlp(x1, paths, n_blocks, mods, w1_b, b_mlp1[0][None, :], w2_b,
                    b_mlp2[0][None, :], ln2_g[0][None, :], ln2_b[0][None, :])
    return (y_p, y_s, new_state)
```

```python
import functools
from typing import NamedTuple

import jax
import jax.numpy as jnp
from jax import lax
from jax.experimental import pallas as pl
from jax.experimental.pallas import tpu as pltpu

F32 = jnp.float32
BF16 = jnp.bfloat16

D_MODEL = 1024
S5_GROUP = 16
N_GROUPS = D_MODEL // S5_GROUP
S5_STATE = 64
D_POOL = D_MODEL // 2
POOL_WINDOWS = (2, 4, 8, 16)
POOL_GROUP = D_POOL // len(POOL_WINDOWS)
D_IN = D_MODEL + D_POOL + 2 * D_MODEL
D_REST = D_IN - D_MODEL
D_FF = 4 * D_MODEL
N_MOD = 6
GRID_W = 64
DEEPNORM_ALPHA = 2.0 ** 0.25
LN_EPS = 1e-6

CHUNK = 16
CW = CHUNK * S5_GROUP
SW = 4 * S5_STATE
LANES = 128
N_STRIPS = D_MODEL // LANES
GROUPS_PER_STRIP = LANES // S5_GROUP
GROUP_UNROLL = 8
N_POW = 32
SEQ_TILE = 8
TIME_BLOCK = 128
TOK_BLOCK = SEQ_TILE * TIME_BLOCK
HALO = max(POOL_WINDOWS) // 2
IN_SLICES = 4
MLP_SLICES = 4
MIX_SLICES = 4
VMEM_LIMIT = 56 * 1024 * 1024


def _time_major(x3):
    return jnp.swapaxes(x3, 0, 1).reshape(x3.shape[1] * SEQ_TILE, x3.shape[-1])


def _seq_major(x2):
    return jnp.swapaxes(x2.reshape(x2.shape[0] // SEQ_TILE, SEQ_TILE, x2.shape[-1]), 0, 1)


def _modulate(a, mul, add=None):
    a3 = a.reshape(a.shape[0] // SEQ_TILE, SEQ_TILE, a.shape[-1]) * mul[None]
    if add is not None:
        a3 = a3 + add[None]
    return a3.reshape(a.shape)


def _layer_norm(x):
    mu = jnp.mean(x, axis=-1, keepdims=True)
    xc = x - mu
    var = jnp.mean(xc * xc, axis=-1, keepdims=True)
    return xc * lax.rsqrt(var + LN_EPS)


def _sigmoid(x):
    return 0.5 * jnp.tanh(0.5 * x) + 0.5


def _dot(a, b):
    return jnp.dot(a, b, preferred_element_type=F32)


def _dot_nt(a, b, precision=None):
    return lax.dot_general(a, b, (((1,), (1,)), ((), ())), precision=precision,
                           preferred_element_type=F32)


def _mods_kernel(c_ref, cctx_ref, w_ref, b_ref, o_ref):
    cvec = jnp.concatenate([c_ref[...], jnp.broadcast_to(cctx_ref[...], (SEQ_TILE, D_MODEL))],
                           axis=0)
    o_ref[0] = jnp.dot(jax.nn.silu(cvec), w_ref[...], precision=lax.Precision.HIGHEST,
                       preferred_element_type=F32) + b_ref[...]


def _mods(c, c_ctx, w_ada, b_ada):
    n_lat = c.shape[0]
    assert n_lat % SEQ_TILE == 0
    rows = n_lat + SEQ_TILE
    n_out = w_ada.shape[1]
    return pl.pallas_call(
        _mods_kernel,
        grid=(n_out // D_MODEL,),
        in_specs=[
            pl.BlockSpec((n_lat, D_MODEL), lambda j: (0, 0)),
            pl.BlockSpec((1, D_MODEL), lambda j: (0, 0)),
            pl.BlockSpec((D_MODEL, D_MODEL), lambda j: (0, j)),
            pl.BlockSpec((1, D_MODEL), lambda j: (0, j)),
        ],
        out_specs=pl.BlockSpec((1, rows, D_MODEL), lambda j: (j, 0, 0)),
        out_shape=jax.ShapeDtypeStruct((n_out // D_MODEL, rows, D_MODEL), F32),
        name="mods",
    )(c, c_ctx, w_ada, b_ada)


def _cast_specs(weights, n_steps, step_index):
    specs, shapes = [], []
    for w in weights:
        rows = w.shape[0] // n_steps
        assert rows * n_steps == w.shape[0] and rows % 16 == 0
        specs.append(pl.BlockSpec((rows, w.shape[1]), lambda *g: (step_index(*g), 0)))
        shapes.append(jax.ShapeDtypeStruct(w.shape, BF16))
    return specs, shapes


def _cast_chunks(src_refs, dst_refs):
    for src, dst in zip(src_refs, dst_refs):
        dst[...] = src[...].astype(BF16)


def _s5_prep_kernel(*refs, n_cast):
    rows_ref, mats_ref = refs[:2]
    mt_ref, wb_ref, wct_ref, lam_ref = refs[2 + n_cast:6 + n_cast]
    _cast_chunks(refs[2:2 + n_cast], refs[6 + n_cast:6 + 2 * n_cast])
    _s5_prep_body(rows_ref, mats_ref, mt_ref, wb_ref, wct_ref, lam_ref, *refs[6 + 2 * n_cast:])


def _s5_prep_body(rows_ref, mats_ref, mt_ref, wb_ref, wct_ref, lam_ref,
                  pwr_ref, pwi_ref, ge_ref):
    pg = GROUPS_PER_STRIP
    lre = rows_ref[:, 0:1, :]
    lim = rows_ref[:, 1:2, :]
    dt = jnp.exp(rows_ref[:, 2:3, :])
    a = lre * dt
    b = lim * dt
    col = lax.broadcasted_iota(jnp.int32, (1, 1, SW), 2)
    is_im = col >= 2 * S5_STATE
    is_b = (col & (2 * S5_STATE - 1)) >= S5_STATE

    mag = jnp.exp(a)
    sq_r = mag * jnp.cos(b)
    sq_i = mag * jnp.sin(b)
    pwr_ref[:, 0:1, :] = jnp.ones((pg, 1, SW), F32)
    pwi_ref[:, 0:1, :] = jnp.zeros((pg, 1, SW), F32)
    m = 1
    while m < N_POW:
        lo_r = pwr_ref[:, 0:m, :]
        lo_i = pwi_ref[:, 0:m, :]
        pwr_ref[:, m:2 * m, :] = lo_r * sq_r - lo_i * sq_i
        pwi_ref[:, m:2 * m, :] = lo_r * sq_i + lo_i * sq_r
        sq_r, sq_i = sq_r * sq_r - sq_i * sq_i, 2.0 * sq_r * sq_i
        m *= 2

    def power(kf, kb):
        def row(ref, kk):
            return jnp.zeros((pg, 1, SW), F32) if kk is None else ref[:, kk:kk + 1, :]
        if kf == kb:
            return row(pwr_ref, kf), row(pwi_ref, kf)
        return (jnp.where(is_b, row(pwr_ref, kb), row(pwr_ref, kf)),
                jnp.where(is_b, row(pwi_ref, kb), row(pwi_ref, kf)))

    lbr = pwr_ref[:, 1:2, :]
    lbi = pwi_ref[:, 1:2, :]
    den = lre * lre + lim * lim
    nr = lbr - 1.0
    cr = (nr * lre + lbi * lim) / den
    ci = (lbi * lre - nr * lim) / den
    bre = mats_ref[:, 0]
    bim = mats_ref[:, 1]
    bbr = cr * bre - ci * bim
    bbi = cr * bim + ci * bre
    bx = jnp.where(is_im, bbi, bbr)
    by = jnp.where(is_im, bbr, -bbi)
    cre = mats_ref[:, 2]
    cim = mats_ref[:, 3]
    cx = jnp.where(is_im, -cim, cre)
    cy = jnp.where(is_im, -cre, -cim)

    for t in range(CHUNK):
        rows = slice(t * S5_GROUP, (t + 1) * S5_GROUP)
        pr, pi = power(CHUNK - 1 - t, t)
        wb_ref[:, rows, :] = (pr * bx + pi * by).astype(BF16)
        pr, pi = power(t + 1, CHUNK - t)
        wct_ref[:, rows, :] = (pr * cx + pi * cy).astype(BF16)

    for j in range(2 * CHUNK):
        rows = slice(j * S5_GROUP, (j + 1) * S5_GROUP)
        if j == 2 * CHUNK - 1:
            ge_ref[:, rows, :] = jnp.zeros((pg, S5_GROUP, SW), BF16)
            continue
        pr, pi = power(j - (CHUNK - 1) if j >= CHUNK - 1 else None,
                       (CHUNK - 1) - j if j <= CHUNK - 1 else None)
        ge_ref[:, rows, :] = (pr * cx + pi * cy).astype(BF16)

    dsum = rows_ref[:, 3:4, :] + rows_ref[:, 4:5, :]
    r16 = lax.broadcasted_iota(jnp.int32, (S5_GROUP, CW), 0)
    c16 = lax.broadcasted_iota(jnp.int32, (S5_GROUP, CW), 1)
    for gi in range(pg):
        e = _dot_nt(bx[gi].astype(BF16), ge_ref[gi])
        blocks = []
        for tau in range(CHUNK):
            start = (CHUNK - 1 - tau) * S5_GROUP
            blocks.append(e[:, start:start + CW]
                          + jnp.where(c16 == r16 + tau * S5_GROUP, dsum[gi], 0.0))
        mt_ref[gi] = jnp.concatenate(blocks, axis=0).T.astype(BF16)

    lam_ref[...] = jnp.concatenate(
        [pwr_ref[:, CHUNK:CHUNK + 1, :], pwi_ref[:, CHUNK:CHUNK + 1, :],
         jnp.zeros((pg, 6, SW), F32)], axis=1)


def _s5_prep(rows, mats, cast_weights):
    g = N_GROUPS
    pg = GROUPS_PER_STRIP
    n_steps = g // pg
    mat_out = pl.BlockSpec((pg, CW, SW), lambda i: (i, 0, 0))
    cast_specs, cast_shapes = _cast_specs(cast_weights, n_steps, lambda i: i)
    return pl.pallas_call(
        functools.partial(_s5_prep_kernel, n_cast=len(cast_weights)),
        grid=(n_steps,),
        in_specs=[pl.BlockSpec((pg,) + rows.shape[1:], lambda i: (i, 0, 0)),
                  pl.BlockSpec((pg,) + mats.shape[1:], lambda i: (i, 0, 0, 0))] + cast_specs,
        out_specs=[mat_out, mat_out, mat_out,
                   pl.BlockSpec((pg, 8, SW), lambda i: (i, 0, 0))] + cast_specs,
        out_shape=[jax.ShapeDtypeStruct((g, CW, CW), BF16),
                   jax.ShapeDtypeStruct((g, CW, SW), BF16),
                   jax.ShapeDtypeStruct((g, CW, SW), BF16),
                   jax.ShapeDtypeStruct((g, 8, SW), F32)] + cast_shapes,
        scratch_shapes=[pltpu.VMEM((pg, N_POW, SW), F32), pltpu.VMEM((pg, N_POW, SW), F32),
                        pltpu.VMEM((pg, 2 * CW, SW), BF16)],
        name="s5_prep",
    )(rows, mats, *cast_weights)


class _Path(NamedTuple):
    n_seq: int
    seq_len: int
    pool_n: int
    first_block: int
    mod_block: int
    mod_per_seq: bool

    @property
    def time_blocks(self):
        return self.seq_len // TIME_BLOCK

    @property
    def n_blocks(self):
        return (self.n_seq // SEQ_TILE) * self.time_blocks


def _paths(specs):
    paths, first = [], 0
    for (n_seq, seq_len, _), pool_n, mod_block, mod_per_seq in specs:
        assert n_seq % SEQ_TILE == 0 and seq_len % TIME_BLOCK == 0
        sub_time = TIME_BLOCK // MIX_SLICES
        assert pool_n % sub_time == 0 and seq_len % pool_n == 0 and HALO <= sub_time
        assert pool_n & (pool_n - 1) == 0
        paths.append(_Path(n_seq, seq_len, pool_n, first, mod_block, mod_per_seq))
        first += paths[-1].n_blocks
    return tuple(paths), first


def _block_coords(i, paths):
    coords = []
    for k, p in enumerate(paths):
        end = p.first_block + p.n_blocks
        inside = (i >= p.first_block) & (i < end)
        j = jnp.clip(i - p.first_block, 0, p.n_blocks - 1)
        coords.append((inside, (j // p.time_blocks) * SEQ_TILE, (j % p.time_blocks) * TIME_BLOCK))
    return coords


def _mod_block(i, paths):
    blk = 0
    for p in paths:
        j = jnp.clip(i - p.first_block, 0, p.n_blocks - 1)
        seq_tile = j // p.time_blocks if p.mod_per_seq else 0
        blk = jnp.where(i >= p.first_block, p.mod_block + seq_tile, blk)
    return blk


def _row_spec(width, n_blocks, shift=0):
    return pl.BlockSpec((TOK_BLOCK, width),
                        lambda i: (jnp.clip(i + shift, 0, n_blocks - 1), 0))


def _block_copies(hbm_refs, buf, sem, i, slot, paths, to_hbm):
    for (inside, seq0, t0), hbm in zip(_block_coords(i, paths), hbm_refs):
        copies = []
        for s in range(SEQ_TILE):
            rows = hbm.at[seq0 + s, pl.ds(t0, TIME_BLOCK), :]
            tile_rows = buf.at[slot, :, s, :]
            src, dst = (tile_rows, rows) if to_hbm else (rows, tile_rows)
            copies.append(pltpu.make_async_copy(src, dst, sem.at[slot]))
        yield inside, copies


def _start_block(hbm_refs, buf, sem, i, slot, paths, to_hbm):
    for inside, copies in _block_copies(hbm_refs, buf, sem, i, slot, paths, to_hbm):
        @pl.when(inside)
        def _():
            for c in copies:
                c.start()


def _wait_block(hbm_refs, buf, sem, slot, paths, to_hbm):
    _, copies = next(_block_copies(hbm_refs, buf, sem, 0, slot, paths, to_hbm))
    for c in copies:
        c.wait()


def _fetch_x_block(x_hbm, xbuf, sem, paths):
    i = pl.program_id(0)
    slot = i % 2

    @pl.when(i == 0)
    def _():
        _start_block(x_hbm, xbuf, sem, i, slot, paths, to_hbm=False)

    @pl.when(i + 1 < pl.num_programs(0))
    def _():
        _start_block(x_hbm, xbuf, sem, i + 1, 1 - slot, paths, to_hbm=False)

    _wait_block(x_hbm, xbuf, sem, slot, paths, to_hbm=False)
    return slot


def _in_proj_kernel(*refs, paths):
    n_paths = len(paths)
    x_hbm = refs[:n_paths]
    mod_ref, w_ref, b_ref, ua_ref, ub_ref, sg_ref, xbuf, sem = refs[n_paths:]
    slot = _fetch_x_block(x_hbm, xbuf, sem, paths)
    sub_time = TIME_BLOCK // IN_SLICES
    for sub in range(IN_SLICES):
        times = slice(sub * sub_time, (sub + 1) * sub_time)
        rows = slice(sub * sub_time * SEQ_TILE, (sub + 1) * sub_time * SEQ_TILE)
        x = xbuf[slot, times].reshape(sub_time * SEQ_TILE, D_MODEL)
        h = _modulate(_layer_norm(x), 1.0 + mod_ref[1], mod_ref[0])
        z = _dot(h.astype(BF16), w_ref[...]) + b_ref[...]
        ua_ref[times] = z[:, :D_MODEL].reshape(sub_time, SEQ_TILE, D_MODEL)
        ub_ref[rows, :] = z[:, D_MODEL:D_MODEL + D_POOL].astype(BF16)
        sg_ref[rows, :] = _sigmoid(z[:, D_MODEL + D_POOL:]).astype(BF16)


def _in_proj(xs, paths, n_blocks, mods, w_in, b_in):
    n_tok = n_blocks * TOK_BLOCK
    const = lambda i: (0, 0)
    once = pl.Buffered(1)
    return pl.pallas_call(
        functools.partial(_in_proj_kernel, paths=paths),
        grid=(n_blocks,),
        in_specs=[pl.BlockSpec(memory_space=pl.ANY)] * len(xs) + [
            pl.BlockSpec((N_MOD, SEQ_TILE, D_MODEL), lambda i: (0, _mod_block(i, paths), 0)),
            pl.BlockSpec((D_MODEL, D_IN), const, pipeline_mode=once),
            pl.BlockSpec((1, D_IN), const),
        ],
        out_specs=[pl.BlockSpec((TIME_BLOCK, SEQ_TILE, D_MODEL), lambda i: (i, 0, 0)),
                   _row_spec(D_POOL, n_blocks), _row_spec(2 * D_MODEL, n_blocks)],
        out_shape=[jax.ShapeDtypeStruct((n_blocks * TIME_BLOCK, SEQ_TILE, D_MODEL), F32),
                   jax.ShapeDtypeStruct((n_tok, D_POOL), BF16),
                   jax.ShapeDtypeStruct((n_tok, 2 * D_MODEL), BF16)],
        scratch_shapes=[pltpu.VMEM((2, TIME_BLOCK, SEQ_TILE, D_MODEL), F32),
                        pltpu.SemaphoreType.DMA((2,))],
        compiler_params=pltpu.CompilerParams(dimension_semantics=("arbitrary",),
                                             vmem_limit_bytes=VMEM_LIMIT),
        name="in_proj",
    )(*xs, mods, w_in, b_in)


def _s5_core_kernel(*refs, paths, n_cast):
    ua_ref, mt_ref, wb_ref, wct_ref, lam_ref, s0_ref = refs[:6]
    v_ref, fin_ref = refs[6 + n_cast:8 + n_cast]
    _cast_chunks(refs[6:6 + n_cast], refs[8 + n_cast:8 + 2 * n_cast])
    xt_ref, yt_ref, sloc_ref, sprf_ref, sprb_ref, fsc_ref = refs[8 + 2 * n_cast:]
    for k, p in enumerate(paths):
        @pl.when(pl.program_id(1) == k)
        def _():
            _s5_strip(ua_ref, mt_ref, wb_ref, wct_ref, lam_ref, s0_ref, v_ref, fin_ref,
                      xt_ref, yt_ref, sloc_ref, sprf_ref, sprb_ref, fsc_ref,
                      n_seq=p.n_seq, seq_len=p.seq_len, has_init=k > 0)


def _s5_strip(ua_ref, mt_ref, wb_ref, wct_ref, lam_ref, s0_ref, v_ref, fin_ref,
              xt_ref, yt_ref, sloc_ref, sprf_ref, sprb_ref, fsc_ref, *, n_seq, seq_len, has_init):
    n_chunk = seq_len // CHUNK
    n_sb = n_seq // SEQ_TILE
    half = 2 * S5_STATE

    for tau in range(CHUNK):
        x_tau = jnp.concatenate([ua_ref[sb * seq_len + c * CHUNK + tau]
                                 for c in range(n_chunk) for sb in range(n_sb)], axis=0)
        xt = x_tau.astype(BF16).T
        for gi in range(GROUPS_PER_STRIP):
            xt_ref[gi, tau * S5_GROUP:(tau + 1) * S5_GROUP, :] = xt[gi * S5_GROUP:(gi + 1) * S5_GROUP, :]

    is_f = lax.broadcasted_iota(jnp.int32, (1, half), 1) < S5_STATE
    col = lax.broadcasted_iota(jnp.int32, (1, SW), 1)
    col_is_f = (col & (half - 1)) < S5_STATE

    def one_group(gi, slot):
        xg = xt_ref[gi]
        u = xg.T
        sloc_ref[slot] = _dot(u, wb_ref[gi])
        ar = lam_ref[gi, 0:1, 0:half]
        ai = lam_ref[gi, 1:2, 0:half]
        if has_init:
            s_re = s0_ref[gi, 0]
            s_im = s0_ref[gi, 1]
        else:
            s_re = jnp.zeros((n_seq, half), F32)
            s_im = jnp.zeros((n_seq, half), F32)
        for c in range(n_chunk):
            rf = pl.ds(c * n_seq, n_seq)
            rb = pl.ds((n_chunk - 1 - c) * n_seq, n_seq)
            sprf_ref[slot, rf, 0:half] = s_re
            sprf_ref[slot, rf, half:SW] = s_im
            sprb_ref[slot, rb, 0:half] = s_re
            sprb_ref[slot, rb, half:SW] = s_im
            l_re = jnp.where(is_f, sloc_ref[slot, rf, 0:half], sloc_ref[slot, rb, 0:half])
            l_im = jnp.where(is_f, sloc_ref[slot, rf, half:SW], sloc_ref[slot, rb, half:SW])
            s_re, s_im = (ar * s_re - ai * s_im + l_re,
                          ar * s_im + ai * s_re + l_im)
        if not has_init:
            fsc_ref[gi, 0] = s_re
            fsc_ref[gi, 1] = s_im
        sprev = jnp.where(col_is_f, sprf_ref[slot], sprb_ref[slot]).astype(BF16)
        yt = _dot(mt_ref[gi], xg) + _dot_nt(wct_ref[gi], sprev)
        yt_ref[gi] = jax.nn.gelu(yt)

    def group_body(i, carry):
        for slot in range(GROUP_UNROLL):
            one_group(i * GROUP_UNROLL + slot, slot)
        return carry

    lax.fori_loop(0, GROUPS_PER_STRIP // GROUP_UNROLL, group_body, 0)

    if not has_init:
        for part in range(2):
            fs = jnp.swapaxes(fsc_ref[:, part], 0, 1)
            fin_ref[:, part] = fs[:, :, 0:S5_STATE]
            fin_ref[:, 2 + part] = fs[:, :, S5_STATE:half]

    for t in range(CHUNK):
        vt = yt_ref[:, t * S5_GROUP:(t + 1) * S5_GROUP, :].reshape(LANES, n_seq * n_chunk)
        v_t = vt.T
        for c in range(n_chunk):
            for sb in range(n_sb):
                r0 = (c * n_sb + sb) * SEQ_TILE
                v_ref[sb * seq_len + c * CHUNK + t] = v_t[r0:r0 + SEQ_TILE, :]


def _s5_core(ua, paths, m, wb, wct, lam, s0, cast_weights):
    ctx, lat = paths
    n_steps = N_STRIPS * len(paths)
    cast_specs, cast_shapes = _cast_specs(cast_weights, n_steps, lambda j, k: j * len(paths) + k)
    tiles = ctx.n_blocks * TIME_BLOCK
    rows = tiles * SEQ_TILE // CHUNK
    assert lat.n_blocks * TIME_BLOCK == tiles and s0.shape[2] == lat.n_seq
    gps = GROUPS_PER_STRIP
    strip = pl.BlockSpec((tiles, SEQ_TILE, LANES), lambda j, k: (k, 0, j))
    mat = pl.BlockSpec((gps, CW, SW), lambda j, k: (j, 0, 0))
    return pl.pallas_call(
        functools.partial(_s5_core_kernel, paths=paths, n_cast=len(cast_weights)),
        grid=(N_STRIPS, len(paths)),
        in_specs=[strip, mat, mat, mat, pl.BlockSpec((gps, 8, SW), lambda j, k: (j, 0, 0)),
                  pl.BlockSpec((gps, 2, lat.n_seq, 2 * S5_STATE), lambda j, k: (j, 0, 0, 0))
                  ] + cast_specs,
        out_specs=[strip,
                   pl.BlockSpec((ctx.n_seq, 4, gps, S5_STATE), lambda j, k: (0, 0, j, 0))
                   ] + cast_specs,
        out_shape=[jax.ShapeDtypeStruct(ua.shape, F32),
                   jax.ShapeDtypeStruct((ctx.n_seq, 4, N_GROUPS, S5_STATE), F32)] + cast_shapes,
        scratch_shapes=[pltpu.VMEM((gps, CW, rows), BF16), pltpu.VMEM((gps, CW, rows), F32),
                        pltpu.VMEM((GROUP_UNROLL, rows, SW), F32),
                        pltpu.VMEM((GROUP_UNROLL, rows, SW), F32),
                        pltpu.VMEM((GROUP_UNROLL, rows, SW), F32),
                        pltpu.VMEM((gps, 2, ctx.n_seq, 2 * S5_STATE), F32)],
        compiler_params=pltpu.CompilerParams(dimension_semantics=("arbitrary", "arbitrary"),
                                             vmem_limit_bytes=VMEM_LIMIT),
        name="s5_core",
    )(ua, m, wb, wct, lam, s0, *cast_weights)


def _mix_out_kernel(*refs, paths):
    n_paths = len(paths)
    x_hbm = refs[:n_paths]
    (v_ref, ubp_ref, ub_ref, ubn_ref, sg_ref, mod_ref, wglu_ref, bglu_ref, wpa_ref, wpool_ref,
     pscale_ref, wpb_ref, wout_ref, bout_ref, g1_ref, b1_ref, o_ref, xbuf, sem) = refs[n_paths:]
    x_slot = _fetch_x_block(x_hbm, xbuf, sem, paths)
    i = pl.program_id(0)
    t_block, run_mask = 0, 0
    for p in paths:
        j = jnp.clip(i - p.first_block, 0, p.n_blocks - 1)
        t_block = jnp.where(i >= p.first_block, (j % p.time_blocks) * TIME_BLOCK, t_block)
        run_mask = jnp.where(i >= p.first_block, p.pool_n - 1, run_mask)
    halo_rows = HALO * SEQ_TILE
    ext = jnp.concatenate([ubp_ref[...], ub_ref[...], ubn_ref[...]], axis=0).astype(F32)

    sub_rows = TOK_BLOCK // MIX_SLICES
    sub_time = TIME_BLOCK // MIX_SLICES
    slices = [slice(sub * sub_rows, (sub + 1) * sub_rows) for sub in range(MIX_SLICES)]

    def s5_branch(sub):
        tiles = sub_rows // SEQ_TILE
        v = v_ref[sub * tiles:(sub + 1) * tiles].reshape(sub_rows, D_MODEL)
        glu = v * _sigmoid(_dot(v.astype(BF16), wglu_ref[...]) + bglu_ref[...])
        return _dot(glu.astype(BF16), wpa_ref[...])

    def pool_branch(sub):
        pos = (t_block + sub * sub_time) & run_mask
        first = pos == 0
        last = pos + sub_time == run_mask + 1
        base = sub * sub_rows
        win = jnp.concatenate(
            [jnp.where(first, 0.0, ext[base:base + halo_rows]),
             ext[base + halo_rows:base + halo_rows + sub_rows],
             jnp.where(last, 0.0, ext[base + halo_rows + sub_rows:base + 2 * halo_rows + sub_rows])],
            axis=0)
        t_idx = lax.broadcasted_iota(jnp.int32, (sub_rows, POOL_GROUP), 0) // SEQ_TILE
        pooled = []
        for gi, w in enumerate(POOL_WINDOWS):
            cols = slice(gi * POOL_GROUP, (gi + 1) * POOL_GROUP)
            acc = jnp.zeros((sub_rows, POOL_GROUP), F32)
            for k in range(-(w // 2), w - w // 2):
                r0 = halo_rows + k * SEQ_TILE
                acc = acc + win[r0:r0 + sub_rows, cols]
            lo = jnp.where(first, jnp.maximum(t_idx - w // 2, 0), t_idx - w // 2)
            hi = jnp.where(last, jnp.minimum(t_idx - w // 2 + w, sub_time), t_idx - w // 2 + w)
            p = acc / (hi - lo).astype(F32) - win[halo_rows:halo_rows + sub_rows, cols]
            pooled.append(_dot(p.astype(BF16), wpool_ref[gi]))
        pm = jnp.concatenate(pooled, axis=1) * pscale_ref[...]
        return _dot(pm.astype(BF16), wpb_ref[...])

    ya = [s5_branch(sub) for sub in range(MIX_SLICES)]
    yb = [pool_branch(sub) for sub in range(MIX_SLICES)]
    tm = []
    for sub, rows in enumerate(slices):
        merged = (sg_ref[rows, 0:D_MODEL].astype(F32) * ya[sub]
                  + sg_ref[rows, D_MODEL:2 * D_MODEL].astype(F32) * yb[sub])
        tm.append(_dot(merged.astype(BF16), wout_ref[...]) + bout_ref[...])
    for sub, rows in enumerate(slices):
        x = xbuf[x_slot, sub * sub_time:(sub + 1) * sub_time].reshape(sub_rows, D_MODEL)
        y = DEEPNORM_ALPHA * x + _modulate(tm[sub], mod_ref[2])
        o_ref[rows, :] = _layer_norm(y) * g1_ref[...] + b1_ref[...]


def _mix_out(xs, v, ub, sg, paths, n_blocks, mods, wglu, bglu, wpa, wpool, pscale, wpb, wout,
             bout, g1, b1):
    const2 = lambda i: (0, 0)
    once = pl.Buffered(1)
    vec = pl.BlockSpec((1, D_MODEL), const2)
    sq = pl.BlockSpec((D_MODEL, D_MODEL), const2, pipeline_mode=once)
    rows = functools.partial(_row_spec, n_blocks=n_blocks)
    halo_rows = HALO * SEQ_TILE
    halos_per_block = TOK_BLOCK // halo_rows
    n_halo_blocks = n_blocks * halos_per_block
    halo = lambda index: pl.BlockSpec((halo_rows, D_POOL), lambda i: (index(i), 0))
    return pl.pallas_call(
        functools.partial(_mix_out_kernel, paths=paths),
        grid=(n_blocks,),
        in_specs=[pl.BlockSpec(memory_space=pl.ANY)] * len(xs) + [
            pl.BlockSpec((TIME_BLOCK, SEQ_TILE, D_MODEL), lambda i: (i, 0, 0)),
            halo(lambda i: jnp.maximum(i * halos_per_block - 1, 0)), rows(D_POOL),
            halo(lambda i: jnp.minimum((i + 1) * halos_per_block, n_halo_blocks - 1)),
            rows(2 * D_MODEL),
            pl.BlockSpec((N_MOD, SEQ_TILE, D_MODEL), lambda i: (0, _mod_block(i, paths), 0)),
            sq, vec, sq,
            pl.BlockSpec((len(POOL_WINDOWS), POOL_GROUP, POOL_GROUP), lambda i: (0, 0, 0)),
            pl.BlockSpec((1, D_POOL), const2),
            pl.BlockSpec((D_POOL, D_MODEL), const2, pipeline_mode=once),
            sq, vec, vec, vec,
        ],
        out_specs=rows(D_MODEL),
        out_shape=jax.ShapeDtypeStruct((n_blocks * TOK_BLOCK, D_MODEL), F32),
        scratch_shapes=[pltpu.VMEM((2, TIME_BLOCK, SEQ_TILE, D_MODEL), F32),
                        pltpu.SemaphoreType.DMA((2,))],
        compiler_params=pltpu.CompilerParams(dimension_semantics=("arbitrary",),
                                             vmem_limit_bytes=VMEM_LIMIT),
        name="mix_out",
    )(*xs, v, ub, ub, ub, sg, mods, wglu, bglu, wpa, wpool, pscale, wpb, wout, bout, g1, b1)


def _mlp_kernel(*refs, paths):
    n_paths = len(paths)
    x_ref, mod_ref, w1_ref, b1_ref, w2_ref, b2_ref, g2_ref, be2_ref = refs[:8]
    y_hbm = refs[8:8 + n_paths]
    obuf, sem = refs[8 + n_paths:]
    i = pl.program_id(0)
    n = pl.num_programs(0)
    slot = i % 2

    @pl.when(i >= 2)
    def _():
        _wait_block(y_hbm, obuf, sem, slot, paths, to_hbm=True)

    sub_time = TIME_BLOCK // MLP_SLICES
    for sub in range(MLP_SLICES):
        x = x_ref[sub * sub_time * SEQ_TILE:(sub + 1) * sub_time * SEQ_TILE, :]
        h = _modulate(_layer_norm(x), 1.0 + mod_ref[4], mod_ref[3]).astype(BF16)
        f = jnp.zeros(x.shape, F32)
        for k in range(D_FF // D_MODEL):
            cols = slice(k * D_MODEL, (k + 1) * D_MODEL)
            a = jnp.square(jax.nn.relu(_dot(h, w1_ref[:, cols]) + b1_ref[:, cols]))
            f = f + _dot(a.astype(BF16), w2_ref[cols, :])
        y = DEEPNORM_ALPHA * x + _modulate(f + b2_ref[...], mod_ref[5])
        obuf[slot, sub * sub_time:(sub + 1) * sub_time] = (
            _layer_norm(y) * g2_ref[...] + be2_ref[...]).reshape(sub_time, SEQ_TILE, D_MODEL)

    _start_block(y_hbm, obuf, sem, i, slot, paths, to_hbm=True)

    @pl.when(i == n - 1)
    def _():
        @pl.when(n >= 2)
        def _():
            _wait_block(y_hbm, obuf, sem, 1 - slot, paths, to_hbm=True)
        _wait_block(y_hbm, obuf, sem, slot, paths, to_hbm=True)


def _mlp(x1, paths, n_blocks, mods, w1, b1, w2, b2, g2, be2):
    const2 = lambda i: (0, 0)
    vec = pl.BlockSpec((1, D_MODEL), const2)
    once = pl.Buffered(1)
    return pl.pallas_call(
        functools.partial(_mlp_kernel, paths=paths),
        grid=(n_blocks,),
        in_specs=[
            _row_spec(D_MODEL, n_blocks),
            pl.BlockSpec((N_MOD, SEQ_TILE, D_MODEL), lambda i: (0, _mod_block(i, paths), 0)),
            pl.BlockSpec((D_MODEL, D_FF), const2, pipeline_mode=once),
            pl.BlockSpec((1, D_FF), const2),
            pl.BlockSpec((D_FF, D_MODEL), const2, pipeline_mode=once),
            vec, vec, vec,
        ],
        out_specs=[pl.BlockSpec(memory_space=pl.ANY)] * len(paths),
        out_shape=[jax.ShapeDtypeStruct((p.n_seq, p.seq_len, D_MODEL), F32) for p in paths],
        scratch_shapes=[pltpu.VMEM((2, TIME_BLOCK, SEQ_TILE, D_MODEL), F32),
                        pltpu.SemaphoreType.DMA((2,))],
        compiler_params=pltpu.CompilerParams(dimension_semantics=("arbitrary",),
                                             vmem_limit_bytes=VMEM_LIMIT),
        name="mlp",
    )(x1, mods, w1, b1, w2, b2, g2, be2)


def _state_cols(x_f, x_b):
    return jnp.concatenate([x_f, x_b, x_f, x_b], axis=-1)


def kernel(x_prompt, x_sample, state_s5, c, c_ctx, w_ada, b_ada, w_in, b_in, s5_lam_re, s5_lam_im, s5_log_dt, s5_b_re, s5_b_im, s5_c_re, s5_c_im, s5_d, w_glu, b_glu, w_proj_a, w_pool, pool_scale, w_proj_b, w_out, b_out, ln1_g, ln1_b, w_mlp1, b_mlp1, w_mlp2, b_mlp2, ln2_g, ln2_b):
    assert w_in.shape[0] == 1, "single-layer backbone"
    n_ctx = x_prompt.shape[0]
    n_lat = x_sample.shape[0]
    g, p, hh = N_GROUPS, S5_STATE, S5_GROUP
    paths, n_blocks = _paths([(x_prompt.shape, x_prompt.shape[1], n_lat // SEQ_TILE, False),
                              (x_sample.shape, GRID_W, 0, True)])
    mods = _mods(c, c_ctx[None, :], w_ada[0], b_ada[0][None, :])

    dirs = lambda x: _state_cols(x[0], x[1])
    rows = jnp.stack(
        [dirs(s5_lam_re[0]), dirs(s5_lam_im[0]),
         dirs(jnp.broadcast_to(s5_log_dt[0][:, :, None], (2, g, p))),
         jnp.tile(s5_d[0, 0].reshape(g, hh), (1, CHUNK)),
         jnp.tile(s5_d[0, 1].reshape(g, hh), (1, CHUNK))], axis=1)
    mats = jnp.stack(
        [dirs(s5_b_re[0].transpose(0, 1, 3, 2)), dirs(s5_b_im[0].transpose(0, 1, 3, 2)),
         dirs(s5_c_re[0]), dirs(s5_c_im[0])], axis=1)
    mt_mat, wb_mat, wct_mat, lam16, w_in_b = _s5_prep(rows, mats, [w_in[0]])

    xs = (x_prompt, x_sample)
    ua, ub, sg = _in_proj(xs, paths, n_blocks, mods, w_in_b, b_in[0][None, :])

    st = state_s5[:, 0].astype(F32)
    s0 = jnp.concatenate([st[:, 0], st[:, 1]], axis=-1).transpose(2, 1, 0, 3)
    v, fin, w_glu_b, w_pa_b, w_pool_b, w_pb_b, w_out_b, w1_b, w2_b = _s5_core(
        ua, paths, mt_mat, wb_mat, wct_mat, lam16, s0,
        [w_glu[0], w_proj_a[0], w_pool[0].reshape(D_POOL, POOL_GROUP), w_proj_b[0], w_out[0],
         w_mlp1[0], w_mlp2[0]])
    new_state = fin.reshape(n_ctx, 1, 2, 2, g, p)

    x1 = _mix_out(xs, v, ub, sg, paths, n_blocks, mods,
                  w_glu_b, b_glu[0][None, :], w_pa_b, w_pool_b.reshape(w_pool.shape[1:]),
                  pool_scale[0][None, :], w_pb_b, w_out_b, b_out[0][None, :],
                  ln1_g[0][None, :], ln1_b[0][None, :])
    y_p, y_s = _mlp(x1, paths, n_blocks, mods, w1_b, b_mlp1[0][None, :], w2_b,
                    b_mlp2[0][None, :], ln2_g[0][None, :], ln2_b[0][None, :])
    return (y_p, y_s, new_state)
```

```python
import functools
from typing import NamedTuple

import jax
import jax.numpy as jnp
from jax import lax
from jax.experimental import pallas as pl
from jax.experimental.pallas import tpu as pltpu

F32 = jnp.float32
BF16 = jnp.bfloat16

D_MODEL = 1024
S5_GROUP = 16
N_GROUPS = D_MODEL // S5_GROUP
S5_STATE = 64
D_POOL = D_MODEL // 2
POOL_WINDOWS = (2, 4, 8, 16)
POOL_GROUP = D_POOL // len(POOL_WINDOWS)
D_IN = D_MODEL + D_POOL + 2 * D_MODEL
D_REST = D_IN - D_MODEL
D_FF = 4 * D_MODEL
N_MOD = 6
GRID_W = 64
DEEPNORM_ALPHA = 2.0 ** 0.25
LN_EPS = 1e-6

CHUNK = 16
CW = CHUNK * S5_GROUP
SW = 4 * S5_STATE
LANES = 128
N_STRIPS = D_MODEL // LANES
GROUPS_PER_STRIP = LANES // S5_GROUP
GROUP_UNROLL = 8
N_POW = 32
SEQ_TILE = 8
TIME_BLOCK = 128
TOK_BLOCK = SEQ_TILE * TIME_BLOCK
HALO = max(POOL_WINDOWS) // 2
IN_SLICES = 4
MLP_SLICES = 4
MIX_SLICES = 4
VMEM_LIMIT = 56 * 1024 * 1024


def _time_major(x3):
    return jnp.swapaxes(x3, 0, 1).reshape(x3.shape[1] * SEQ_TILE, x3.shape[-1])


def _seq_major(x2):
    return jnp.swapaxes(x2.reshape(x2.shape[0] // SEQ_TILE, SEQ_TILE, x2.shape[-1]), 0, 1)


def _modulate(a, mul, add=None):
    a3 = a.reshape(a.shape[0] // SEQ_TILE, SEQ_TILE, a.shape[-1]) * mul[None]
    if add is not None:
        a3 = a3 + add[None]
    return a3.reshape(a.shape)


def _layer_norm(x):
    mu = jnp.mean(x, axis=-1, keepdims=True)
    xc = x - mu
    var = jnp.mean(xc * xc, axis=-1, keepdims=True)
    return xc * lax.rsqrt(var + LN_EPS)


def _sigmoid(x):
    return 0.5 * jnp.tanh(0.5 * x) + 0.5


def _dot(a, b):
    return jnp.dot(a, b, preferred_element_type=F32)


def _dot_nt(a, b, precision=None):
    return lax.dot_general(a, b, (((1,), (1,)), ((), ())), precision=precision,
                           preferred_element_type=F32)


def _mods_kernel(c_ref, cctx_ref, w_ref, b_ref, o_ref):
    cvec = jnp.concatenate([c_ref[...], jnp.broadcast_to(cctx_ref[...], (SEQ_TILE, D_MODEL))],
                           axis=0)
    o_ref[0] = _dot(jax.nn.silu(cvec).astype(BF16), w_ref[...].astype(BF16)) + b_ref[...]


def _mods(c, c_ctx, w_ada, b_ada):
    n_lat = c.shape[0]
    assert n_lat % SEQ_TILE == 0
    rows = n_lat + SEQ_TILE
    n_out = w_ada.shape[1]
    return pl.pallas_call(
        _mods_kernel,
        grid=(n_out // D_MODEL,),
        in_specs=[
            pl.BlockSpec((n_lat, D_MODEL), lambda j: (0, 0)),
            pl.BlockSpec((1, D_MODEL), lambda j: (0, 0)),
            pl.BlockSpec((D_MODEL, D_MODEL), lambda j: (0, j)),
            pl.BlockSpec((1, D_MODEL), lambda j: (0, j)),
        ],
        out_specs=pl.BlockSpec((1, rows, D_MODEL), lambda j: (j, 0, 0)),
        out_shape=jax.ShapeDtypeStruct((n_out // D_MODEL, rows, D_MODEL), F32),
        name="mods",
    )(c, c_ctx, w_ada, b_ada)


def _cast_specs(weights, n_steps, step_index):
    specs, shapes = [], []
    for w in weights:
        rows = w.shape[0] // n_steps
        assert rows * n_steps == w.shape[0] and rows % 16 == 0
        specs.append(pl.BlockSpec((rows, w.shape[1]), lambda *g: (step_index(*g), 0)))
        shapes.append(jax.ShapeDtypeStruct(w.shape, BF16))
    return specs, shapes


def _cast_chunks(src_refs, dst_refs):
    for src, dst in zip(src_refs, dst_refs):
        dst[...] = src[...].astype(BF16)


def _s5_prep_kernel(*refs, n_cast):
    rows_ref, mats_ref = refs[:2]
    mt_ref, wb_ref, wct_ref, lam_ref = refs[2 + n_cast:6 + n_cast]
    _cast_chunks(refs[2:2 + n_cast], refs[6 + n_cast:6 + 2 * n_cast])
    _s5_prep_body(rows_ref, mats_ref, mt_ref, wb_ref, wct_ref, lam_ref, *refs[6 + 2 * n_cast:])


def _s5_prep_body(rows_ref, mats_ref, mt_ref, wb_ref, wct_ref, lam_ref,
                  pwr_ref, pwi_ref, ge_ref):
    pg = GROUPS_PER_STRIP
    lre = rows_ref[:, 0:1, :]
    lim = rows_ref[:, 1:2, :]
    dt = jnp.exp(rows_ref[:, 2:3, :])
    a = lre * dt
    b = lim * dt
    col = lax.broadcasted_iota(jnp.int32, (1, 1, SW), 2)
    is_im = col >= 2 * S5_STATE
    is_b = (col & (2 * S5_STATE - 1)) >= S5_STATE

    mag = jnp.exp(a)
    sq_r = mag * jnp.cos(b)
    sq_i = mag * jnp.sin(b)
    pwr_ref[:, 0:1, :] = jnp.ones((pg, 1, SW), F32)
    pwi_ref[:, 0:1, :] = jnp.zeros((pg, 1, SW), F32)
    m = 1
    while m < N_POW:
        lo_r = pwr_ref[:, 0:m, :]
        lo_i = pwi_ref[:, 0:m, :]
        pwr_ref[:, m:2 * m, :] = lo_r * sq_r - lo_i * sq_i
        pwi_ref[:, m:2 * m, :] = lo_r * sq_i + lo_i * sq_r
        sq_r, sq_i = sq_r * sq_r - sq_i * sq_i, 2.0 * sq_r * sq_i
        m *= 2

    def power(kf, kb):
        def row(ref, kk):
            return jnp.zeros((pg, 1, SW), F32) if kk is None else ref[:, kk:kk + 1, :]
        if kf == kb:
            return row(pwr_ref, kf), row(pwi_ref, kf)
        return (jnp.where(is_b, row(pwr_ref, kb), row(pwr_ref, kf)),
                jnp.where(is_b, row(pwi_ref, kb), row(pwi_ref, kf)))

    lbr = pwr_ref[:, 1:2, :]
    lbi = pwi_ref[:, 1:2, :]
    den = lre * lre + lim * lim
    nr = lbr - 1.0
    cr = (nr * lre + lbi * lim) / den
    ci = (lbi * lre - nr * lim) / den
    bre = mats_ref[:, 0]
    bim = mats_ref[:, 1]
    bbr = cr * bre - ci * bim
    bbi = cr * bim + ci * bre
    bx = jnp.where(is_im, bbi, bbr)
    by = jnp.where(is_im, bbr, -bbi)
    cre = mats_ref[:, 2]
    cim = mats_ref[:, 3]
    cx = jnp.where(is_im, -cim, cre)
    cy = jnp.where(is_im, -cre, -cim)

    for t in range(CHUNK):
        rows = slice(t * S5_GROUP, (t + 1) * S5_GROUP)
        pr, pi = power(CHUNK - 1 - t, t)
        wb_ref[:, rows, :] = (pr * bx + pi * by).astype(BF16)
        pr, pi = power(t + 1, CHUNK - t)
        wct_ref[:, rows, :] = (pr * cx + pi * cy).astype(BF16)

    for j in range(2 * CHUNK):
        rows = slice(j * S5_GROUP, (j + 1) * S5_GROUP)
        if j == 2 * CHUNK - 1:
            ge_ref[:, rows, :] = jnp.zeros((pg, S5_GROUP, SW), BF16)
            continue
        pr, pi = power(j - (CHUNK - 1) if j >= CHUNK - 1 else None,
                       (CHUNK - 1) - j if j <= CHUNK - 1 else None)
        ge_ref[:, rows, :] = (pr * cx + pi * cy).astype(BF16)

    dsum = rows_ref[:, 3:4, :] + rows_ref[:, 4:5, :]
    r16 = lax.broadcasted_iota(jnp.int32, (S5_GROUP, CW), 0)
    c16 = lax.broadcasted_iota(jnp.int32, (S5_GROUP, CW), 1)
    for gi in range(pg):
        e = _dot_nt(bx[gi].astype(BF16), ge_ref[gi])
        blocks = []
        for tau in range(CHUNK):
            start = (CHUNK - 1 - tau) * S5_GROUP
            blocks.append(e[:, start:start + CW]
                          + jnp.where(c16 == r16 + tau * S5_GROUP, dsum[gi], 0.0))
        mt_ref[gi] = jnp.concatenate(blocks, axis=0).T.astype(BF16)

    lam_ref[...] = jnp.concatenate(
        [pwr_ref[:, CHUNK:CHUNK + 1, :], pwi_ref[:, CHUNK:CHUNK + 1, :],
         jnp.zeros((pg, 6, SW), F32)], axis=1)


def _s5_prep(rows, mats, cast_weights):
    g = N_GROUPS
    pg = GROUPS_PER_STRIP
    n_steps = g // pg
    mat_out = pl.BlockSpec((pg, CW, SW), lambda i: (i, 0, 0))
    cast_specs, cast_shapes = _cast_specs(cast_weights, n_steps, lambda i: i)
    return pl.pallas_call(
        functools.partial(_s5_prep_kernel, n_cast=len(cast_weights)),
        grid=(n_steps,),
        in_specs=[pl.BlockSpec((pg,) + rows.shape[1:], lambda i: (i, 0, 0)),
                  pl.BlockSpec((pg,) + mats.shape[1:], lambda i: (i, 0, 0, 0))] + cast_specs,
        out_specs=[mat_out, mat_out, mat_out,
                   pl.BlockSpec((pg, 8, SW), lambda i: (i, 0, 0))] + cast_specs,
        out_shape=[jax.ShapeDtypeStruct((g, CW, CW), BF16),
                   jax.ShapeDtypeStruct((g, CW, SW), BF16),
                   jax.ShapeDtypeStruct((g, CW, SW), BF16),
                   jax.ShapeDtypeStruct((g, 8, SW), F32)] + cast_shapes,
        scratch_shapes=[pltpu.VMEM((pg, N_POW, SW), F32), pltpu.VMEM((pg, N_POW, SW), F32),
                        pltpu.VMEM((pg, 2 * CW, SW), BF16)],
        name="s5_prep",
    )(rows, mats, *cast_weights)


class _Path(NamedTuple):
    n_seq: int
    seq_len: int
    pool_n: int
    first_block: int
    mod_block: int
    mod_per_seq: bool

    @property
    def time_blocks(self):
        return self.seq_len // TIME_BLOCK

    @property
    def n_blocks(self):
        return (self.n_seq // SEQ_TILE) * self.time_blocks


def _paths(specs):
    paths, first = [], 0
    for (n_seq, seq_len, _), pool_n, mod_block, mod_per_seq in specs:
        assert n_seq % SEQ_TILE == 0 and seq_len % TIME_BLOCK == 0
        sub_time = TIME_BLOCK // MIX_SLICES
        assert pool_n % sub_time == 0 and seq_len % pool_n == 0 and HALO <= sub_time
        assert pool_n & (pool_n - 1) == 0
        paths.append(_Path(n_seq, seq_len, pool_n, first, mod_block, mod_per_seq))
        first += paths[-1].n_blocks
    return tuple(paths), first


def _block_coords(i, paths):
    coords = []
    for k, p in enumerate(paths):
        end = p.first_block + p.n_blocks
        inside = (i >= p.first_block) & (i < end)
        j = jnp.clip(i - p.first_block, 0, p.n_blocks - 1)
        coords.append((inside, (j // p.time_blocks) * SEQ_TILE, (j % p.time_blocks) * TIME_BLOCK))
    return coords


def _mod_block(i, paths):
    blk = 0
    for p in paths:
        j = jnp.clip(i - p.first_block, 0, p.n_blocks - 1)
        seq_tile = j // p.time_blocks if p.mod_per_seq else 0
        blk = jnp.where(i >= p.first_block, p.mod_block + seq_tile, blk)
    return blk


def _row_spec(width, n_blocks, shift=0):
    return pl.BlockSpec((TOK_BLOCK, width),
                        lambda i: (jnp.clip(i + shift, 0, n_blocks - 1), 0))


def _block_copies(hbm_refs, buf, sem, i, slot, paths, to_hbm):
    for (inside, seq0, t0), hbm in zip(_block_coords(i, paths), hbm_refs):
        copies = []
        for s in range(SEQ_TILE):
            rows = hbm.at[seq0 + s, pl.ds(t0, TIME_BLOCK), :]
            tile_rows = buf.at[slot, :, s, :]
            src, dst = (tile_rows, rows) if to_hbm else (rows, tile_rows)
            copies.append(pltpu.make_async_copy(src, dst, sem.at[slot]))
        yield inside, copies


def _start_block(hbm_refs, buf, sem, i, slot, paths, to_hbm):
    for inside, copies in _block_copies(hbm_refs, buf, sem, i, slot, paths, to_hbm):
        @pl.when(inside)
        def _():
            for c in copies:
                c.start()


def _wait_block(hbm_refs, buf, sem, slot, paths, to_hbm):
    _, copies = next(_block_copies(hbm_refs, buf, sem, 0, slot, paths, to_hbm))
    for c in copies:
        c.wait()


def _fetch_x_block(x_hbm, xbuf, sem, paths):
    i = pl.program_id(0)
    slot = i % 2

    @pl.when(i == 0)
    def _():
        _start_block(x_hbm, xbuf, sem, i, slot, paths, to_hbm=False)

    @pl.when(i + 1 < pl.num_programs(0))
    def _():
        _start_block(x_hbm, xbuf, sem, i + 1, 1 - slot, paths, to_hbm=False)

    _wait_block(x_hbm, xbuf, sem, slot, paths, to_hbm=False)
    return slot


def _in_proj_kernel(*refs, paths):
    n_paths = len(paths)
    x_hbm = refs[:n_paths]
    mod_ref, w_ref, b_ref, ua_ref, ub_ref, sg_ref, xbuf, sem = refs[n_paths:]
    slot = _fetch_x_block(x_hbm, xbuf, sem, paths)
    sub_time = TIME_BLOCK // IN_SLICES
    for sub in range(IN_SLICES):
        times = slice(sub * sub_time, (sub + 1) * sub_time)
        rows = slice(sub * sub_time * SEQ_TILE, (sub + 1) * sub_time * SEQ_TILE)
        x = xbuf[slot, times].reshape(sub_time * SEQ_TILE, D_MODEL)
        h = _modulate(_layer_norm(x), 1.0 + mod_ref[1], mod_ref[0])
        z = _dot(h.astype(BF16), w_ref[...]) + b_ref[...]
        ua_ref[times] = z[:, :D_MODEL].reshape(sub_time, SEQ_TILE, D_MODEL)
        ub_ref[rows, :] = z[:, D_MODEL:D_MODEL + D_POOL].astype(BF16)
        sg_ref[rows, :] = _sigmoid(z[:, D_MODEL + D_POOL:]).astype(BF16)


def _in_proj(xs, paths, n_blocks, mods, w_in, b_in):
    n_tok = n_blocks * TOK_BLOCK
    const = lambda i: (0, 0)
    once = pl.Buffered(1)
    return pl.pallas_call(
        functools.partial(_in_proj_kernel, paths=paths),
        grid=(n_blocks,),
        in_specs=[pl.BlockSpec(memory_space=pl.ANY)] * len(xs) + [
            pl.BlockSpec((N_MOD, SEQ_TILE, D_MODEL), lambda i: (0, _mod_block(i, paths), 0)),
            pl.BlockSpec((D_MODEL, D_IN), const, pipeline_mode=once),
            pl.BlockSpec((1, D_IN), const),
        ],
        out_specs=[pl.BlockSpec((TIME_BLOCK, SEQ_TILE, D_MODEL), lambda i: (i, 0, 0)),
                   _row_spec(D_POOL, n_blocks), _row_spec(2 * D_MODEL, n_blocks)],
        out_shape=[jax.ShapeDtypeStruct((n_blocks * TIME_BLOCK, SEQ_TILE, D_MODEL), F32),
                   jax.ShapeDtypeStruct((n_tok, D_POOL), BF16),
                   jax.ShapeDtypeStruct((n_tok, 2 * D_MODEL), BF16)],
        scratch_shapes=[pltpu.VMEM((2, TIME_BLOCK, SEQ_TILE, D_MODEL), F32),
                        pltpu.SemaphoreType.DMA((2,))],
        compiler_params=pltpu.CompilerParams(dimension_semantics=("arbitrary",),
                                             vmem_limit_bytes=VMEM_LIMIT),
        name="in_proj",
    )(*xs, mods, w_in, b_in)


def _s5_core_kernel(*refs, paths, n_cast):
    ua_ref, mt_ref, wb_ref, wct_ref, lam_ref, s0_ref = refs[:6]
    v_ref, fin_ref = refs[6 + n_cast:8 + n_cast]
    _cast_chunks(refs[6:6 + n_cast], refs[8 + n_cast:8 + 2 * n_cast])
    xt_ref, yt_ref, sloc_ref, sprf_ref, sprb_ref, fsc_ref = refs[8 + 2 * n_cast:]
    for k, p in enumerate(paths):
        @pl.when(pl.program_id(1) == k)
        def _():
            _s5_strip(ua_ref, mt_ref, wb_ref, wct_ref, lam_ref, s0_ref, v_ref, fin_ref,
                      xt_ref, yt_ref, sloc_ref, sprf_ref, sprb_ref, fsc_ref,
                      n_seq=p.n_seq, seq_len=p.seq_len, has_init=k > 0)


def _s5_strip(ua_ref, mt_ref, wb_ref, wct_ref, lam_ref, s0_ref, v_ref, fin_ref,
              xt_ref, yt_ref, sloc_ref, sprf_ref, sprb_ref, fsc_ref, *, n_seq, seq_len, has_init):
    n_chunk = seq_len // CHUNK
    n_sb = n_seq // SEQ_TILE
    half = 2 * S5_STATE

    for tau in range(CHUNK):
        x_tau = jnp.concatenate([ua_ref[sb * seq_len + c * CHUNK + tau]
                                 for c in range(n_chunk) for sb in range(n_sb)], axis=0)
        xt = x_tau.astype(BF16).T
        for gi in range(GROUPS_PER_STRIP):
            xt_ref[gi, tau * S5_GROUP:(tau + 1) * S5_GROUP, :] = xt[gi * S5_GROUP:(gi + 1) * S5_GROUP, :]

    is_f = lax.broadcasted_iota(jnp.int32, (1, half), 1) < S5_STATE
    col = lax.broadcasted_iota(jnp.int32, (1, SW), 1)
    col_is_f = (col & (half - 1)) < S5_STATE

    def one_group(gi, slot):
        xg = xt_ref[gi]
        u = xg.T
        sloc_ref[slot] = _dot(u, wb_ref[gi])
        ar = lam_ref[gi, 0:1, 0:half]
        ai = lam_ref[gi, 1:2, 0:half]
        if has_init:
            s_re = s0_ref[gi, 0]
            s_im = s0_ref[gi, 1]
        else:
            s_re = jnp.zeros((n_seq, half), F32)
            s_im = jnp.zeros((n_seq, half), F32)
        for c in range(n_chunk):
            rf = pl.ds(c * n_seq, n_seq)
            rb = pl.ds((n_chunk - 1 - c) * n_seq, n_seq)
            sprf_ref[slot, rf, 0:half] = s_re
            sprf_ref[slot, rf, half:SW] = s_im
            sprb_ref[slot, rb, 0:half] = s_re
            sprb_ref[slot, rb, half:SW] = s_im
            l_re = jnp.where(is_f, sloc_ref[slot, rf, 0:half], sloc_ref[slot, rb, 0:half])
            l_im = jnp.where(is_f, sloc_ref[slot, rf, half:SW], sloc_ref[slot, rb, half:SW])
            s_re, s_im = (ar * s_re - ai * s_im + l_re,
                          ar * s_im + ai * s_re + l_im)
        if not has_init:
            fsc_ref[gi, 0] = s_re
            fsc_ref[gi, 1] = s_im
        sprev = jnp.where(col_is_f, sprf_ref[slot], sprb_ref[slot]).astype(BF16)
        yt = _dot(mt_ref[gi], xg) + _dot_nt(wct_ref[gi], sprev)
        yt_ref[gi] = jax.nn.gelu(yt)

    def group_body(i, carry):
        for slot in range(GROUP_UNROLL):
            one_group(i * GROUP_UNROLL + slot, slot)
        return carry

    lax.fori_loop(0, GROUPS_PER_STRIP // GROUP_UNROLL, group_body, 0)

    if not has_init:
        for part in range(2):
            fs = jnp.swapaxes(fsc_ref[:, part], 0, 1)
            fin_ref[:, part] = fs[:, :, 0:S5_STATE]
            fin_ref[:, 2 + part] = fs[:, :, S5_STATE:half]

    for t in range(CHUNK):
        vt = yt_ref[:, t * S5_GROUP:(t + 1) * S5_GROUP, :].reshape(LANES, n_seq * n_chunk)
        v_t = vt.T
        for c in range(n_chunk):
            for sb in range(n_sb):
                r0 = (c * n_sb + sb) * SEQ_TILE
                v_ref[sb * seq_len + c * CHUNK + t] = v_t[r0:r0 + SEQ_TILE, :]


def _s5_core(ua, paths, m, wb, wct, lam, s0, cast_weights):
    ctx, lat = paths
    n_steps = N_STRIPS * len(paths)
    cast_specs, cast_shapes = _cast_specs(cast_weights, n_steps, lambda j, k: j * len(paths) + k)
    tiles = ctx.n_blocks * TIME_BLOCK
    rows = tiles * SEQ_TILE // CHUNK
    assert lat.n_blocks * TIME_BLOCK == tiles and s0.shape[2] == lat.n_seq
    gps = GROUPS_PER_STRIP
    strip = pl.BlockSpec((tiles, SEQ_TILE, LANES), lambda j, k: (k, 0, j))
    mat = pl.BlockSpec((gps, CW, SW), lambda j, k: (j, 0, 0))
    return pl.pallas_call(
        functools.partial(_s5_core_kernel, paths=paths, n_cast=len(cast_weights)),
        grid=(N_STRIPS, len(paths)),
        in_specs=[strip, mat, mat, mat, pl.BlockSpec((gps, 8, SW), lambda j, k: (j, 0, 0)),
                  pl.BlockSpec((gps, 2, lat.n_seq, 2 * S5_STATE), lambda j, k: (j, 0, 0, 0))
                  ] + cast_specs,
        out_specs=[strip,
                   pl.BlockSpec((ctx.n_seq, 4, gps, S5_STATE), lambda j, k: (0, 0, j, 0))
                   ] + cast_specs,
        out_shape=[jax.ShapeDtypeStruct(ua.shape, F32),
                   jax.ShapeDtypeStruct((ctx.n_seq, 4, N_GROUPS, S5_STATE), F32)] + cast_shapes,
        scratch_shapes=[pltpu.VMEM((gps, CW, rows), BF16), pltpu.VMEM((gps, CW, rows), F32),
                        pltpu.VMEM((GROUP_UNROLL, rows, SW), F32),
                        pltpu.VMEM((GROUP_UNROLL, rows, SW), F32),
                        pltpu.VMEM((GROUP_UNROLL, rows, SW), F32),
                        pltpu.VMEM((gps, 2, ctx.n_seq, 2 * S5_STATE), F32)],
        compiler_params=pltpu.CompilerParams(dimension_semantics=("arbitrary", "arbitrary"),
                                             vmem_limit_bytes=VMEM_LIMIT),
        name="s5_core",
    )(ua, m, wb, wct, lam, s0, *cast_weights)


def _mix_out_kernel(*refs, paths):
    n_paths = len(paths)
    x_hbm = refs[:n_paths]
    (v_ref, ubp_ref, ub_ref, ubn_ref, sg_ref, mod_ref, wglu_ref, bglu_ref, wpa_ref, wpool_ref,
     pscale_ref, wpb_ref, wout_ref, bout_ref, g1_ref, b1_ref, o_ref, xbuf, sem) = refs[n_paths:]
    x_slot = _fetch_x_block(x_hbm, xbuf, sem, paths)
    i = pl.program_id(0)
    t_block, run_mask = 0, 0
    for p in paths:
        j = jnp.clip(i - p.first_block, 0, p.n_blocks - 1)
        t_block = jnp.where(i >= p.first_block, (j % p.time_blocks) * TIME_BLOCK, t_block)
        run_mask = jnp.where(i >= p.first_block, p.pool_n - 1, run_mask)
    halo_rows = HALO * SEQ_TILE
    ext = jnp.concatenate([ubp_ref[...], ub_ref[...], ubn_ref[...]], axis=0).astype(F32)

    sub_rows = TOK_BLOCK // MIX_SLICES
    sub_time = TIME_BLOCK // MIX_SLICES
    slices = [slice(sub * sub_rows, (sub + 1) * sub_rows) for sub in range(MIX_SLICES)]

    def s5_branch(sub):
        tiles = sub_rows // SEQ_TILE
        v = v_ref[sub * tiles:(sub + 1) * tiles].reshape(sub_rows, D_MODEL)
        glu = v * _sigmoid(_dot(v.astype(BF16), wglu_ref[...]) + bglu_ref[...])
        return _dot(glu.astype(BF16), wpa_ref[...])

    def pool_branch(sub):
        pos = (t_block + sub * sub_time) & run_mask
        first = pos == 0
        last = pos + sub_time == run_mask + 1
        base = sub * sub_rows
        win = jnp.concatenate(
            [jnp.where(first, 0.0, ext[base:base + halo_rows]),
             ext[base + halo_rows:base + halo_rows + sub_rows],
             jnp.where(last, 0.0, ext[base + halo_rows + sub_rows:base + 2 * halo_rows + sub_rows])],
            axis=0)
        t_idx = lax.broadcasted_iota(jnp.int32, (sub_rows, POOL_GROUP), 0) // SEQ_TILE
        pooled = []
        for gi, w in enumerate(POOL_WINDOWS):
            cols = slice(gi * POOL_GROUP, (gi + 1) * POOL_GROUP)
            acc = jnp.zeros((sub_rows, POOL_GROUP), F32)
            for k in range(-(w // 2), w - w // 2):
                r0 = halo_rows + k * SEQ_TILE
                acc = acc + win[r0:r0 + sub_rows, cols]
            lo = jnp.where(first, jnp.maximum(t_idx - w // 2, 0), t_idx - w // 2)
            hi = jnp.where(last, jnp.minimum(t_idx - w // 2 + w, sub_time), t_idx - w // 2 + w)
            p = acc / (hi - lo).astype(F32) - win[halo_rows:halo_rows + sub_rows, cols]
            pooled.append(_dot(p.astype(BF16), wpool_ref[gi]))
        pm = jnp.concatenate(pooled, axis=1) * pscale_ref[...]
        return _dot(pm.astype(BF16), wpb_ref[...])

    ya = [s5_branch(sub) for sub in range(MIX_SLICES)]
    yb = [pool_branch(sub) for sub in range(MIX_SLICES)]
    tm = []
    for sub, rows in enumerate(slices):
        merged = (sg_ref[rows, 0:D_MODEL].astype(F32) * ya[sub]
                  + sg_ref[rows, D_MODEL:2 * D_MODEL].astype(F32) * yb[sub])
        tm.append(_dot(merged.astype(BF16), wout_ref[...]) + bout_ref[...])
    for sub, rows in enumerate(slices):
        x = xbuf[x_slot, sub * sub_time:(sub + 1) * sub_time].reshape(sub_rows, D_MODEL)
        y = DEEPNORM_ALPHA * x + _modulate(tm[sub], mod_ref[2])
        o_ref[rows, :] = _layer_norm(y) * g1_ref[...] + b1_ref[...]


def _mix_out(xs, v, ub, sg, paths, n_blocks, mods, wglu, bglu, wpa, wpool, pscale, wpb, wout,
             bout, g1, b1):
    const2 = lambda i: (0, 0)
    once = pl.Buffered(1)
    vec = pl.BlockSpec((1, D_MODEL), const2)
    sq = pl.BlockSpec((D_MODEL, D_MODEL), const2, pipeline_mode=once)
    rows = functools.partial(_row_spec, n_blocks=n_blocks)
    halo_rows = HALO * SEQ_TILE
    halos_per_block = TOK_BLOCK // halo_rows
    n_halo_blocks = n_blocks * halos_per_block
    halo = lambda index: pl.BlockSpec((halo_rows, D_POOL), lambda i: (index(i), 0))
    return pl.pallas_call(
        functools.partial(_mix_out_kernel, paths=paths),
        grid=(n_blocks,),
        in_specs=[pl.BlockSpec(memory_space=pl.ANY)] * len(xs) + [
            pl.BlockSpec((TIME_BLOCK, SEQ_TILE, D_MODEL), lambda i: (i, 0, 0)),
            halo(lambda i: jnp.maximum(i * halos_per_block - 1, 0)), rows(D_POOL),
            halo(lambda i: jnp.minimum((i + 1) * halos_per_block, n_halo_blocks - 1)),
            rows(2 * D_MODEL),
            pl.BlockSpec((N_MOD, SEQ_TILE, D_MODEL), lambda i: (0, _mod_block(i, paths), 0)),
            sq, vec, sq,
            pl.BlockSpec((len(POOL_WINDOWS), POOL_GROUP, POOL_GROUP), lambda i: (0, 0, 0)),
            pl.BlockSpec((1, D_POOL), const2),
            pl.BlockSpec((D_POOL, D_MODEL), const2, pipeline_mode=once),
            sq, vec, vec, vec,
        ],
        out_specs=rows(D_MODEL),
        out_shape=jax.ShapeDtypeStruct((n_blocks * TOK_BLOCK, D_MODEL), F32),
        scratch_shapes=[pltpu.VMEM((2, TIME_BLOCK, SEQ_TILE, D_MODEL), F32),
                        pltpu.SemaphoreType.DMA((2,))],
        compiler_params=pltpu.CompilerParams(dimension_semantics=("arbitrary",),
                                             vmem_limit_bytes=VMEM_LIMIT),
        name="mix_out",
    )(*xs, v, ub, ub, ub, sg, mods, wglu, bglu, wpa, wpool, pscale, wpb, wout, bout, g1, b1)


def _mlp_kernel(*refs, paths):
    n_paths = len(paths)
    x_ref, mod_ref, w1_ref, b1_ref, w2_ref, b2_ref, g2_ref, be2_ref = refs[:8]
    y_hbm = refs[8:8 + n_paths]
    obuf, sem = refs[8 + n_paths:]
    i = pl.program_id(0)
    n = pl.num_programs(0)
    slot = i % 2

    @pl.when(i >= 2)
    def _():
        _wait_block(y_hbm, obuf, sem, slot, paths, to_hbm=True)

    sub_time = TIME_BLOCK // MLP_SLICES
    for sub in range(MLP_SLICES):
        x = x_ref[sub * sub_time * SEQ_TILE:(sub + 1) * sub_time * SEQ_TILE, :]
        h = _modulate(_layer_norm(x), 1.0 + mod_ref[4], mod_ref[3]).astype(BF16)
        f = jnp.zeros(x.shape, F32)
        for k in range(D_FF // D_MODEL):
            cols = slice(k * D_MODEL, (k + 1) * D_MODEL)
            a = jnp.square(jax.nn.relu(_dot(h, w1_ref[:, cols]) + b1_ref[:, cols]))
            f = f + _dot(a.astype(BF16), w2_ref[cols, :])
        y = DEEPNORM_ALPHA * x + _modulate(f + b2_ref[...], mod_ref[5])
        obuf[slot, sub * sub_time:(sub + 1) * sub_time] = (
            _layer_norm(y) * g2_ref[...] + be2_ref[...]).reshape(sub_time, SEQ_TILE, D_MODEL)

    _start_block(y_hbm, obuf, sem, i, slot, paths, to_hbm=True)

    @pl.when(i == n - 1)
    def _():
        @pl.when(n >= 2)
        def _():
            _wait_block(y_hbm, obuf, sem, 1 - slot, paths, to_hbm=True)
        _wait_block(y_hbm, obuf, sem, slot, paths, to_hbm=True)


def _mlp(x1, paths, n_blocks, mods, w1, b1, w2, b2, g2, be2):
    const2 = lambda i: (0, 0)
    vec = pl.BlockSpec((1, D_MODEL), const2)
    once = pl.Buffered(1)
    return pl.pallas_call(
        functools.partial(_mlp_kernel, paths=paths),
        grid=(n_blocks,),
        in_specs=[
            _row_spec(D_MODEL, n_blocks),
            pl.BlockSpec((N_MOD, SEQ_TILE, D_MODEL), lambda i: (0, _mod_block(i, paths), 0)),
            pl.BlockSpec((D_MODEL, D_FF), const2, pipeline_mode=once),
            pl.BlockSpec((1, D_FF), const2),
            pl.BlockSpec((D_FF, D_MODEL), const2, pipeline_mode=once),
            vec, vec, vec,
        ],
        out_specs=[pl.BlockSpec(memory_space=pl.ANY)] * len(paths),
        out_shape=[jax.ShapeDtypeStruct((p.n_seq, p.seq_len, D_MODEL), F32) for p in paths],
        scratch_shapes=[pltpu.VMEM((2, TIME_BLOCK, SEQ_TILE, D_MODEL), F32),
                        pltpu.SemaphoreType.DMA((2,))],
        compiler_params=pltpu.CompilerParams(dimension_semantics=("arbitrary",),
                                             vmem_limit_bytes=VMEM_LIMIT),
        name="mlp",
    )(x1, mods, w1, b1, w2, b2, g2, be2)


def _state_cols(x_f, x_b):
    return jnp.concatenate([x_f, x_b, x_f, x_b], axis=-1)


def kernel(x_prompt, x_sample, state_s5, c, c_ctx, w_ada, b_ada, w_in, b_in, s5_lam_re, s5_lam_im, s5_log_dt, s5_b_re, s5_b_im, s5_c_re, s5_c_im, s5_d, w_glu, b_glu, w_proj_a, w_pool, pool_scale, w_proj_b, w_out, b_out, ln1_g, ln1_b, w_mlp1, b_mlp1, w_mlp2, b_mlp2, ln2_g, ln2_b):
    assert w_in.shape[0] == 1, "single-layer backbone"
    n_ctx = x_prompt.shape[0]
    n_lat = x_sample.shape[0]
    g, p, hh = N_GROUPS, S5_STATE, S5_GROUP
    paths, n_blocks = _paths([(x_prompt.shape, x_prompt.shape[1], n_lat // SEQ_TILE, False),
                              (x_sample.shape, GRID_W, 0, True)])
    mods = _mods(c, c_ctx[None, :], w_ada[0], b_ada[0][None, :])

    dirs = lambda x: _state_cols(x[0], x[1])
    rows = jnp.stack(
        [dirs(s5_lam_re[0]), dirs(s5_lam_im[0]),
         dirs(jnp.broadcast_to(s5_log_dt[0][:, :, None], (2, g, p))),
         jnp.tile(s5_d[0, 0].reshape(g, hh), (1, CHUNK)),
         jnp.tile(s5_d[0, 1].reshape(g, hh), (1, CHUNK))], axis=1)
    mats = jnp.stack(
        [dirs(s5_b_re[0].transpose(0, 1, 3, 2)), dirs(s5_b_im[0].transpose(0, 1, 3, 2)),
         dirs(s5_c_re[0]), dirs(s5_c_im[0])], axis=1)
    mt_mat, wb_mat, wct_mat, lam16, w_in_b = _s5_prep(rows, mats, [w_in[0]])

    xs = (x_prompt, x_sample)
    ua, ub, sg = _in_proj(xs, paths, n_blocks, mods, w_in_b, b_in[0][None, :])

    st = state_s5[:, 0].astype(F32)
    s0 = jnp.concatenate([st[:, 0], st[:, 1]], axis=-1).transpose(2, 1, 0, 3)
    v, fin, w_glu_b, w_pa_b, w_pool_b, w_pb_b, w_out_b, w1_b, w2_b = _s5_core(
        ua, paths, mt_mat, wb_mat, wct_mat, lam16, s0,
        [w_glu[0], w_proj_a[0], w_pool[0].reshape(D_POOL, POOL_GROUP), w_proj_b[0], w_out[0],
         w_mlp1[0], w_mlp2[0]])
    new_state = fin.reshape(n_ctx, 1, 2, 2, g, p)

    x1 = _mix_out(xs, v, ub, sg, paths, n_blocks, mods,
                  w_glu_b, b_glu[0][None, :], w_pa_b, w_pool_b.reshape(w_pool.shape[1:]),
                  pool_scale[0][None, :], w_pb_b, w_out_b, b_out[0][None, :],
                  ln1_g[0][None, :], ln1_b[0][None, :])
    y_p, y_s = _mlp(x1, paths, n_blocks, mods, w1_b, b_mlp1[0][None, :], w2_b,
                    b_mlp2[0][None, :], ln2_g[0][None, :], ln2_b[0][None, :])
    return (y_p, y_s, new_state)
```

```python
import functools
from typing import NamedTuple

import jax
import jax.numpy as jnp
from jax import lax
from jax.experimental import pallas as pl
from jax.experimental.pallas import tpu as pltpu

F32 = jnp.float32
BF16 = jnp.bfloat16

D_MODEL = 1024
S5_GROUP = 16
N_GROUPS = D_MODEL // S5_GROUP
S5_STATE = 64
D_POOL = D_MODEL // 2
POOL_WINDOWS = (2, 4, 8, 16)
POOL_GROUP = D_POOL // len(POOL_WINDOWS)
D_IN = D_MODEL + D_POOL + 2 * D_MODEL
D_FF = 4 * D_MODEL
N_MOD = 6
GRID_W = 64
DEEPNORM_ALPHA = 2.0 ** 0.25
LN_EPS = 1e-6

CHUNK = 16
CW = CHUNK * S5_GROUP
SW = 4 * S5_STATE
LANES = 128
N_STRIPS = D_MODEL // LANES
GROUPS_PER_STRIP = LANES // S5_GROUP
GROUP_UNROLL = 8
N_POW = 32
SEQ_TILE = 8
TIME_BLOCK = 128
TOK_BLOCK = SEQ_TILE * TIME_BLOCK
HALO = max(POOL_WINDOWS) // 2
IN_SLICES = 4
MLP_SLICES = 4
MIX_SLICES = 4
VMEM_LIMIT = 56 * 1024 * 1024


def _modulate(a, mul, add=None):
    a3 = a.reshape(a.shape[0] // SEQ_TILE, SEQ_TILE, a.shape[-1]) * mul[None]
    if add is not None:
        a3 = a3 + add[None]
    return a3.reshape(a.shape)


def _layer_norm(x):
    mu = jnp.mean(x, axis=-1, keepdims=True)
    xc = x - mu
    var = jnp.mean(xc * xc, axis=-1, keepdims=True)
    return xc * lax.rsqrt(var + LN_EPS)


def _sigmoid(x):
    return 0.5 * jnp.tanh(0.5 * x) + 0.5


def _dot(a, b):
    return jnp.dot(a, b, preferred_element_type=F32)


def _dot_nt(a, b):
    return lax.dot_general(a, b, (((1,), (1,)), ((), ())), preferred_element_type=F32)


def _mods_kernel(c_ref, cctx_ref, w_ref, b_ref, o_ref):
    cvec = jnp.concatenate([c_ref[...], jnp.broadcast_to(cctx_ref[...], (SEQ_TILE, D_MODEL))],
                           axis=0)
    o_ref[0] = _dot(jax.nn.silu(cvec).astype(BF16), w_ref[...].astype(BF16)) + b_ref[...]


def _mods(c, c_ctx, w_ada, b_ada):
    n_lat = c.shape[0]
    assert n_lat % SEQ_TILE == 0
    rows = n_lat + SEQ_TILE
    n_out = w_ada.shape[1]
    return pl.pallas_call(
        _mods_kernel,
        grid=(n_out // D_MODEL,),
        in_specs=[
            pl.BlockSpec((n_lat, D_MODEL), lambda j: (0, 0)),
            pl.BlockSpec((1, D_MODEL), lambda j: (0, 0)),
            pl.BlockSpec((D_MODEL, D_MODEL), lambda j: (0, j)),
            pl.BlockSpec((1, D_MODEL), lambda j: (0, j)),
        ],
        out_specs=pl.BlockSpec((1, rows, D_MODEL), lambda j: (j, 0, 0)),
        out_shape=jax.ShapeDtypeStruct((n_out // D_MODEL, rows, D_MODEL), F32),
        name="mods",
    )(c, c_ctx, w_ada, b_ada)


def _cast_specs(weights, n_steps, step_index):
    specs, shapes = [], []
    for w in weights:
        rows = w.shape[0] // n_steps
        assert rows * n_steps == w.shape[0] and rows % 16 == 0
        specs.append(pl.BlockSpec((rows, w.shape[1]), lambda *g: (step_index(*g), 0)))
        shapes.append(jax.ShapeDtypeStruct(w.shape, BF16))
    return specs, shapes


def _cast_chunks(src_refs, dst_refs):
    for src, dst in zip(src_refs, dst_refs):
        dst[...] = src[...].astype(BF16)


def _s5_prep_kernel(*refs, n_cast):
    rows_ref, mats_ref = refs[:2]
    mt_ref, wb_ref, wct_ref, lam_ref = refs[2 + n_cast:6 + n_cast]
    _cast_chunks(refs[2:2 + n_cast], refs[6 + n_cast:6 + 2 * n_cast])
    _s5_prep_body(rows_ref, mats_ref, mt_ref, wb_ref, wct_ref, lam_ref, *refs[6 + 2 * n_cast:])


def _s5_prep_body(rows_ref, mats_ref, mt_ref, wb_ref, wct_ref, lam_ref,
                  pwr_ref, pwi_ref, ge_ref):
    pg = GROUPS_PER_STRIP
    lre = rows_ref[:, 0:1, :]
    lim = rows_ref[:, 1:2, :]
    dt = jnp.exp(rows_ref[:, 2:3, :])
    a = lre * dt
    b = lim * dt
    col = lax.broadcasted_iota(jnp.int32, (1, 1, SW), 2)
    is_im = col >= 2 * S5_STATE
    is_b = (col & (2 * S5_STATE - 1)) >= S5_STATE

    mag = jnp.exp(a)
    sq_r = mag * jnp.cos(b)
    sq_i = mag * jnp.sin(b)
    pwr_ref[:, 0:1, :] = jnp.ones((pg, 1, SW), F32)
    pwi_ref[:, 0:1, :] = jnp.zeros((pg, 1, SW), F32)
    m = 1
    while m < N_POW:
        lo_r = pwr_ref[:, 0:m, :]
        lo_i = pwi_ref[:, 0:m, :]
        pwr_ref[:, m:2 * m, :] = lo_r * sq_r - lo_i * sq_i
        pwi_ref[:, m:2 * m, :] = lo_r * sq_i + lo_i * sq_r
        sq_r, sq_i = sq_r * sq_r - sq_i * sq_i, 2.0 * sq_r * sq_i
        m *= 2

    def power(kf, kb):
        def row(ref, kk):
            return jnp.zeros((pg, 1, SW), F32) if kk is None else ref[:, kk:kk + 1, :]
        if kf == kb:
            return row(pwr_ref, kf), row(pwi_ref, kf)
        return (jnp.where(is_b, row(pwr_ref, kb), row(pwr_ref, kf)),
                jnp.where(is_b, row(pwi_ref, kb), row(pwi_ref, kf)))

    lbr = pwr_ref[:, 1:2, :]
    lbi = pwi_ref[:, 1:2, :]
    den = lre * lre + lim * lim
    nr = lbr - 1.0
    cr = (nr * lre + lbi * lim) / den
    ci = (lbi * lre - nr * lim) / den
    bre = mats_ref[:, 0]
    bim = mats_ref[:, 1]
    bbr = cr * bre - ci * bim
    bbi = cr * bim + ci * bre
    bx = jnp.where(is_im, bbi, bbr)
    by = jnp.where(is_im, bbr, -bbi)
    cre = mats_ref[:, 2]
    cim = mats_ref[:, 3]
    cx = jnp.where(is_im, -cim, cre)
    cy = jnp.where(is_im, -cre, -cim)

    for t in range(CHUNK):
        rows = slice(t * S5_GROUP, (t + 1) * S5_GROUP)
        pr, pi = power(CHUNK - 1 - t, t)
        wb_ref[:, rows, :] = (pr * bx + pi * by).astype(BF16)
        pr, pi = power(t + 1, CHUNK - t)
        wct_ref[:, rows, :] = (pr * cx + pi * cy).astype(BF16)

    for j in range(2 * CHUNK):
        rows = slice(j * S5_GROUP, (j + 1) * S5_GROUP)
        if j == 2 * CHUNK - 1:
            ge_ref[:, rows, :] = jnp.zeros((pg, S5_GROUP, SW), BF16)
            continue
        pr, pi = power(j - (CHUNK - 1) if j >= CHUNK - 1 else None,
                       (CHUNK - 1) - j if j <= CHUNK - 1 else None)
        ge_ref[:, rows, :] = (pr * cx + pi * cy).astype(BF16)

    dsum = rows_ref[:, 3:4, :] + rows_ref[:, 4:5, :]
    r16 = lax.broadcasted_iota(jnp.int32, (S5_GROUP, CW), 0)
    c16 = lax.broadcasted_iota(jnp.int32, (S5_GROUP, CW), 1)
    for gi in range(pg):
        e = _dot_nt(bx[gi].astype(BF16), ge_ref[gi])
        blocks = []
        for tau in range(CHUNK):
            start = (CHUNK - 1 - tau) * S5_GROUP
            blocks.append(e[:, start:start + CW]
                          + jnp.where(c16 == r16 + tau * S5_GROUP, dsum[gi], 0.0))
        mt_ref[gi] = jnp.concatenate(blocks, axis=0).T.astype(BF16)

    lam_ref[...] = jnp.concatenate(
        [pwr_ref[:, CHUNK:CHUNK + 1, :], pwi_ref[:, CHUNK:CHUNK + 1, :],
         jnp.zeros((pg, 6, SW), F32)], axis=1)


def _s5_prep(rows, mats, cast_weights):
    g = N_GROUPS
    pg = GROUPS_PER_STRIP
    n_steps = g // pg
    mat_out = pl.BlockSpec((pg, CW, SW), lambda i: (i, 0, 0))
    cast_specs, cast_shapes = _cast_specs(cast_weights, n_steps, lambda i: i)
    return pl.pallas_call(
        functools.partial(_s5_prep_kernel, n_cast=len(cast_weights)),
        grid=(n_steps,),
        in_specs=[pl.BlockSpec((pg,) + rows.shape[1:], lambda i: (i, 0, 0)),
                  pl.BlockSpec((pg,) + mats.shape[1:], lambda i: (i, 0, 0, 0))] + cast_specs,
        out_specs=[mat_out, mat_out, mat_out,
                   pl.BlockSpec((pg, 8, SW), lambda i: (i, 0, 0))] + cast_specs,
        out_shape=[jax.ShapeDtypeStruct((g, CW, CW), BF16),
                   jax.ShapeDtypeStruct((g, CW, SW), BF16),
                   jax.ShapeDtypeStruct((g, CW, SW), BF16),
                   jax.ShapeDtypeStruct((g, 8, SW), F32)] + cast_shapes,
        scratch_shapes=[pltpu.VMEM((pg, N_POW, SW), F32), pltpu.VMEM((pg, N_POW, SW), F32),
                        pltpu.VMEM((pg, 2 * CW, SW), BF16)],
        name="s5_prep",
    )(rows, mats, *cast_weights)


class _Path(NamedTuple):
    n_seq: int
    seq_len: int
    pool_n: int
    first_block: int
    mod_block: int
    mod_per_seq: bool

    @property
    def time_blocks(self):
        return self.seq_len // TIME_BLOCK

    @property
    def n_blocks(self):
        return (self.n_seq // SEQ_TILE) * self.time_blocks


def _paths(specs):
    paths, first = [], 0
    for (n_seq, seq_len, _), pool_n, mod_block, mod_per_seq in specs:
        assert n_seq % SEQ_TILE == 0 and seq_len % TIME_BLOCK == 0
        sub_time = TIME_BLOCK // MIX_SLICES
        assert pool_n % sub_time == 0 and seq_len % pool_n == 0 and HALO <= sub_time
        assert pool_n & (pool_n - 1) == 0
        paths.append(_Path(n_seq, seq_len, pool_n, first, mod_block, mod_per_seq))
        first += paths[-1].n_blocks
    return tuple(paths), first


def _block_coords(i, paths):
    coords = []
    for k, p in enumerate(paths):
        end = p.first_block + p.n_blocks
        inside = (i >= p.first_block) & (i < end)
        j = jnp.clip(i - p.first_block, 0, p.n_blocks - 1)
        coords.append((inside, (j // p.time_blocks) * SEQ_TILE, (j % p.time_blocks) * TIME_BLOCK))
    return coords


def _mod_block(i, paths):
    blk = 0
    for p in paths:
        j = jnp.clip(i - p.first_block, 0, p.n_blocks - 1)
        seq_tile = j // p.time_blocks if p.mod_per_seq else 0
        blk = jnp.where(i >= p.first_block, p.mod_block + seq_tile, blk)
    return blk


def _row_spec(width, n_blocks, shift=0):
    return pl.BlockSpec((TOK_BLOCK, width),
                        lambda i: (jnp.clip(i + shift, 0, n_blocks - 1), 0))


def _block_copies(hbm_refs, buf, sem, i, slot, paths, to_hbm):
    for (inside, seq0, t0), hbm in zip(_block_coords(i, paths), hbm_refs):
        copies = []
        for s in range(SEQ_TILE):
            rows = hbm.at[seq0 + s, pl.ds(t0, TIME_BLOCK), :]
            tile_rows = buf.at[slot, :, s, :]
            src, dst = (tile_rows, rows) if to_hbm else (rows, tile_rows)
            copies.append(pltpu.make_async_copy(src, dst, sem.at[slot]))
        yield inside, copies


def _start_block(hbm_refs, buf, sem, i, slot, paths, to_hbm):
    for inside, copies in _block_copies(hbm_refs, buf, sem, i, slot, paths, to_hbm):
        @pl.when(inside)
        def _():
            for c in copies:
                c.start()


def _wait_block(hbm_refs, buf, sem, slot, paths, to_hbm):
    _, copies = next(_block_copies(hbm_refs, buf, sem, 0, slot, paths, to_hbm))
    for c in copies:
        c.wait()


def _fetch_x_block(x_hbm, xbuf, sem, paths):
    i = pl.program_id(0)
    slot = i % 2

    @pl.when(i == 0)
    def _():
        _start_block(x_hbm, xbuf, sem, i, slot, paths, to_hbm=False)

    @pl.when(i + 1 < pl.num_programs(0))
    def _():
        _start_block(x_hbm, xbuf, sem, i + 1, 1 - slot, paths, to_hbm=False)

    _wait_block(x_hbm, xbuf, sem, slot, paths, to_hbm=False)
    return slot


def _in_proj_kernel(*refs, paths):
    n_paths = len(paths)
    x_hbm = refs[:n_paths]
    mod_ref, w_ref, b_ref, ua_ref, ub_ref, sg_ref, xbuf, sem = refs[n_paths:]
    slot = _fetch_x_block(x_hbm, xbuf, sem, paths)
    sub_time = TIME_BLOCK // IN_SLICES
    for sub in range(IN_SLICES):
        times = slice(sub * sub_time, (sub + 1) * sub_time)
        rows = slice(sub * sub_time * SEQ_TILE, (sub + 1) * sub_time * SEQ_TILE)
        x = xbuf[slot, times].reshape(sub_time * SEQ_TILE, D_MODEL)
        h = _modulate(_layer_norm(x), 1.0 + mod_ref[1], mod_ref[0])
        z = _dot(h.astype(BF16), w_ref[...]) + b_ref[...]
        ua_ref[times] = z[:, :D_MODEL].reshape(sub_time, SEQ_TILE, D_MODEL)
        ub_ref[rows, :] = z[:, D_MODEL:D_MODEL + D_POOL].astype(BF16)
        sg_ref[rows, :] = _sigmoid(z[:, D_MODEL + D_POOL:]).astype(BF16)


def _in_proj(xs, paths, n_blocks, mods, w_in, b_in):
    n_tok = n_blocks * TOK_BLOCK
    const = lambda i: (0, 0)
    once = pl.Buffered(1)
    return pl.pallas_call(
        functools.partial(_in_proj_kernel, paths=paths),
        grid=(n_blocks,),
        in_specs=[pl.BlockSpec(memory_space=pl.ANY)] * len(xs) + [
            pl.BlockSpec((N_MOD, SEQ_TILE, D_MODEL), lambda i: (0, _mod_block(i, paths), 0)),
            pl.BlockSpec((D_MODEL, D_IN), const, pipeline_mode=once),
            pl.BlockSpec((1, D_IN), const),
        ],
        out_specs=[pl.BlockSpec((TIME_BLOCK, SEQ_TILE, D_MODEL), lambda i: (i, 0, 0)),
                   _row_spec(D_POOL, n_blocks), _row_spec(2 * D_MODEL, n_blocks)],
        out_shape=[jax.ShapeDtypeStruct((n_blocks * TIME_BLOCK, SEQ_TILE, D_MODEL), F32),
                   jax.ShapeDtypeStruct((n_tok, D_POOL), BF16),
                   jax.ShapeDtypeStruct((n_tok, 2 * D_MODEL), BF16)],
        scratch_shapes=[pltpu.VMEM((2, TIME_BLOCK, SEQ_TILE, D_MODEL), F32),
                        pltpu.SemaphoreType.DMA((2,))],
        compiler_params=pltpu.CompilerParams(dimension_semantics=("arbitrary",),
                                             vmem_limit_bytes=VMEM_LIMIT),
        name="in_proj",
    )(*xs, mods, w_in, b_in)


def _s5_core_kernel(*refs, paths, n_cast):
    ua_ref, mt_ref, wb_ref, wct_ref, lam_ref, s0_ref = refs[:6]
    v_ref, fin_ref = refs[6 + n_cast:8 + n_cast]
    _cast_chunks(refs[6:6 + n_cast], refs[8 + n_cast:8 + 2 * n_cast])
    xt_ref, yt_ref, sloc_ref, sprf_ref, sprb_ref, fsc_ref = refs[8 + 2 * n_cast:]
    for k, p in enumerate(paths):
        @pl.when(pl.program_id(1) == k)
        def _():
            _s5_strip(ua_ref, mt_ref, wb_ref, wct_ref, lam_ref, s0_ref, v_ref, fin_ref,
                      xt_ref, yt_ref, sloc_ref, sprf_ref, sprb_ref, fsc_ref,
                      n_seq=p.n_seq, seq_len=p.seq_len, has_init=k > 0)


def _s5_strip(ua_ref, mt_ref, wb_ref, wct_ref, lam_ref, s0_ref, v_ref, fin_ref,
              xt_ref, yt_ref, sloc_ref, sprf_ref, sprb_ref, fsc_ref, *, n_seq, seq_len, has_init):
    n_chunk = seq_len // CHUNK
    n_sb = n_seq // SEQ_TILE
    half = 2 * S5_STATE

    for tau in range(CHUNK):
        x_tau = jnp.concatenate([ua_ref[sb * seq_len + c * CHUNK + tau]
                                 for c in range(n_chunk) for sb in range(n_sb)], axis=0)
        xt = x_tau.astype(BF16).T
        for gi in range(GROUPS_PER_STRIP):
            xt_ref[gi, tau * S5_GROUP:(tau + 1) * S5_GROUP, :] = xt[gi * S5_GROUP:(gi + 1) * S5_GROUP, :]

    is_f = lax.broadcasted_iota(jnp.int32, (1, half), 1) < S5_STATE
    col = lax.broadcasted_iota(jnp.int32, (1, SW), 1)
    col_is_f = (col & (half - 1)) < S5_STATE

    def one_group(gi, slot):
        xg = xt_ref[gi]
        u = xg.T
        sloc_ref[slot] = _dot(u, wb_ref[gi])
        ar = lam_ref[gi, 0:1, 0:half]
        ai = lam_ref[gi, 1:2, 0:half]
        if has_init:
            s_re = s0_ref[gi, 0]
            s_im = s0_ref[gi, 1]
        else:
            s_re = jnp.zeros((n_seq, half), F32)
            s_im = jnp.zeros((n_seq, half), F32)
        for c in range(n_chunk):
            rf = pl.ds(c * n_seq, n_seq)
            rb = pl.ds((n_chunk - 1 - c) * n_seq, n_seq)
            sprf_ref[slot, rf, 0:half] = s_re
            sprf_ref[slot, rf, half:SW] = s_im
            sprb_ref[slot, rb, 0:half] = s_re
            sprb_ref[slot, rb, half:SW] = s_im
            l_re = jnp.where(is_f, sloc_ref[slot, rf, 0:half], sloc_ref[slot, rb, 0:half])
            l_im = jnp.where(is_f, sloc_ref[slot, rf, half:SW], sloc_ref[slot, rb, half:SW])
            s_re, s_im = (ar * s_re - ai * s_im + l_re,
                          ar * s_im + ai * s_re + l_im)
        if not has_init:
            fsc_ref[gi, 0] = s_re
            fsc_ref[gi, 1] = s_im
        sprev = jnp.where(col_is_f, sprf_ref[slot], sprb_ref[slot]).astype(BF16)
        yt = _dot(mt_ref[gi], xg) + _dot_nt(wct_ref[gi], sprev)
        yt_ref[gi] = jax.nn.gelu(yt)

    def group_body(i, carry):
        for slot in range(GROUP_UNROLL):
            one_group(i * GROUP_UNROLL + slot, slot)
        return carry

    lax.fori_loop(0, GROUPS_PER_STRIP // GROUP_UNROLL, group_body, 0)

    if not has_init:
        for part in range(2):
            fs = jnp.swapaxes(fsc_ref[:, part], 0, 1)
            fin_ref[:, part] = fs[:, :, 0:S5_STATE]
            fin_ref[:, 2 + part] = fs[:, :, S5_STATE:half]

    for t in range(CHUNK):
        vt = yt_ref[:, t * S5_GROUP:(t + 1) * S5_GROUP, :].reshape(LANES, n_seq * n_chunk)
        v_t = vt.T
        for c in range(n_chunk):
            for sb in range(n_sb):
                r0 = (c * n_sb + sb) * SEQ_TILE
                v_ref[sb * seq_len + c * CHUNK + t] = v_t[r0:r0 + SEQ_TILE, :]


def _s5_core(ua, paths, m, wb, wct, lam, s0, cast_weights):
    ctx, lat = paths
    n_steps = N_STRIPS * len(paths)
    cast_specs, cast_shapes = _cast_specs(cast_weights, n_steps, lambda j, k: j * len(paths) + k)
    tiles = ctx.n_blocks * TIME_BLOCK
    rows = tiles * SEQ_TILE // CHUNK
    assert lat.n_blocks * TIME_BLOCK == tiles and s0.shape[2] == lat.n_seq
    gps = GROUPS_PER_STRIP
    strip = pl.BlockSpec((tiles, SEQ_TILE, LANES), lambda j, k: (k, 0, j))
    mat = pl.BlockSpec((gps, CW, SW), lambda j, k: (j, 0, 0))
    return pl.pallas_call(
        functools.partial(_s5_core_kernel, paths=paths, n_cast=len(cast_weights)),
        grid=(N_STRIPS, len(paths)),
        in_specs=[strip, mat, mat, mat, pl.BlockSpec((gps, 8, SW), lambda j, k: (j, 0, 0)),
                  pl.BlockSpec((gps, 2, lat.n_seq, 2 * S5_STATE), lambda j, k: (j, 0, 0, 0))
                  ] + cast_specs,
        out_specs=[strip,
                   pl.BlockSpec((ctx.n_seq, 4, gps, S5_STATE), lambda j, k: (0, 0, j, 0))
                   ] + cast_specs,
        out_shape=[jax.ShapeDtypeStruct(ua.shape, F32),
                   jax.ShapeDtypeStruct((ctx.n_seq, 4, N_GROUPS, S5_STATE), F32)] + cast_shapes,
        scratch_shapes=[pltpu.VMEM((gps, CW, rows), BF16), pltpu.VMEM((gps, CW, rows), F32),
                        pltpu.VMEM((GROUP_UNROLL, rows, SW), F32),
                        pltpu.VMEM((GROUP_UNROLL, rows, SW), F32),
                        pltpu.VMEM((GROUP_UNROLL, rows, SW), F32),
                        pltpu.VMEM((gps, 2, ctx.n_seq, 2 * S5_STATE), F32)],
        compiler_params=pltpu.CompilerParams(dimension_semantics=("arbitrary", "arbitrary"),
                                             vmem_limit_bytes=VMEM_LIMIT),
        name="s5_core",
    )(ua, m, wb, wct, lam, s0, *cast_weights)


def _mix_out_kernel(*refs, paths):
    n_paths = len(paths)
    x_hbm = refs[:n_paths]
    (v_ref, ubp_ref, ub_ref, ubn_ref, sg_ref, mod_ref, wglu_ref, bglu_ref, wpa_ref, wpool_ref,
     pscale_ref, wpb_ref, wout_ref, bout_ref, g1_ref, b1_ref, o_ref, xbuf, sem) = refs[n_paths:]
    x_slot = _fetch_x_block(x_hbm, xbuf, sem, paths)
    i = pl.program_id(0)
    t_block, run_mask = 0, 0
    for p in paths:
        j = jnp.clip(i - p.first_block, 0, p.n_blocks - 1)
        t_block = jnp.where(i >= p.first_block, (j % p.time_blocks) * TIME_BLOCK, t_block)
        run_mask = jnp.where(i >= p.first_block, p.pool_n - 1, run_mask)
    halo_rows = HALO * SEQ_TILE
    ext = jnp.concatenate([ubp_ref[...], ub_ref[...], ubn_ref[...]], axis=0).astype(F32)

    sub_rows = TOK_BLOCK // MIX_SLICES
    sub_time = TIME_BLOCK // MIX_SLICES
    slices = [slice(sub * sub_rows, (sub + 1) * sub_rows) for sub in range(MIX_SLICES)]

    def s5_branch(sub):
        tiles = sub_rows // SEQ_TILE
        v = v_ref[sub * tiles:(sub + 1) * tiles].reshape(sub_rows, D_MODEL)
        glu = v * _sigmoid(_dot(v.astype(BF16), wglu_ref[...]) + bglu_ref[...])
        return _dot(glu.astype(BF16), wpa_ref[...])

    def pool_branch(sub):
        pos = (t_block + sub * sub_time) & run_mask
        first = pos == 0
        last = pos + sub_time == run_mask + 1
        base = sub * sub_rows
        win = jnp.concatenate(
            [jnp.where(first, 0.0, ext[base:base + halo_rows]),
             ext[base + halo_rows:base + halo_rows + sub_rows],
             jnp.where(last, 0.0, ext[base + halo_rows + sub_rows:base + 2 * halo_rows + sub_rows])],
            axis=0)
        t_idx = lax.broadcasted_iota(jnp.int32, (sub_rows, POOL_GROUP), 0) // SEQ_TILE
        pooled = []
        for gi, w in enumerate(POOL_WINDOWS):
            cols = slice(gi * POOL_GROUP, (gi + 1) * POOL_GROUP)
            acc = jnp.zeros((sub_rows, POOL_GROUP), F32)
            for k in range(-(w // 2), w - w // 2):
                r0 = halo_rows + k * SEQ_TILE
                acc = acc + win[r0:r0 + sub_rows, cols]
            lo = jnp.where(first, jnp.maximum(t_idx - w // 2, 0), t_idx - w // 2)
            hi = jnp.where(last, jnp.minimum(t_idx - w // 2 + w, sub_time), t_idx - w // 2 + w)
            p = acc / (hi - lo).astype(F32) - win[halo_rows:halo_rows + sub_rows, cols]
            pooled.append(_dot(p.astype(BF16), wpool_ref[gi]))
        pm = jnp.concatenate(pooled, axis=1) * pscale_ref[...]
        return _dot(pm.astype(BF16), wpb_ref[...])

    ya = [s5_branch(sub) for sub in range(MIX_SLICES)]
    yb = [pool_branch(sub) for sub in range(MIX_SLICES)]
    tm = []
    for sub, rows in enumerate(slices):
        merged = (sg_ref[rows, 0:D_MODEL].astype(F32) * ya[sub]
                  + sg_ref[rows, D_MODEL:2 * D_MODEL].astype(F32) * yb[sub])
        tm.append(_dot(merged.astype(BF16), wout_ref[...]) + bout_ref[...])
    for sub, rows in enumerate(slices):
        x = xbuf[x_slot, sub * sub_time:(sub + 1) * sub_time].reshape(sub_rows, D_MODEL)
        y = DEEPNORM_ALPHA * x + _modulate(tm[sub], mod_ref[2])
        o_ref[rows, :] = _layer_norm(y) * g1_ref[...] + b1_ref[...]


def _mix_out(xs, v, ub, sg, paths, n_blocks, mods, wglu, bglu, wpa, wpool, pscale, wpb, wout,
             bout, g1, b1):
    const2 = lambda i: (0, 0)
    once = pl.Buffered(1)
    vec = pl.BlockSpec((1, D_MODEL), const2)
    sq = pl.BlockSpec((D_MODEL, D_MODEL), const2, pipeline_mode=once)
    rows = functools.partial(_row_spec, n_blocks=n_blocks)
    halo_rows = HALO * SEQ_TILE
    halos_per_block = TOK_BLOCK // halo_rows
    n_halo_blocks = n_blocks * halos_per_block
    halo = lambda index: pl.BlockSpec((halo_rows, D_POOL), lambda i: (index(i), 0))
    return pl.pallas_call(
        functools.partial(_mix_out_kernel, paths=paths),
        grid=(n_blocks,),
        in_specs=[pl.BlockSpec(memory_space=pl.ANY)] * len(xs) + [
            pl.BlockSpec((TIME_BLOCK, SEQ_TILE, D_MODEL), lambda i: (i, 0, 0)),
            halo(lambda i: jnp.maximum(i * halos_per_block - 1, 0)), rows(D_POOL),
            halo(lambda i: jnp.minimum((i + 1) * halos_per_block, n_halo_blocks - 1)),
            rows(2 * D_MODEL),
            pl.BlockSpec((N_MOD, SEQ_TILE, D_MODEL), lambda i: (0, _mod_block(i, paths), 0)),
            sq, vec, sq,
            pl.BlockSpec((len(POOL_WINDOWS), POOL_GROUP, POOL_GROUP), lambda i: (0, 0, 0)),
            pl.BlockSpec((1, D_POOL), const2),
            pl.BlockSpec((D_POOL, D_MODEL), const2, pipeline_mode=once),
            sq, vec, vec, vec,
        ],
        out_specs=rows(D_MODEL),
        out_shape=jax.ShapeDtypeStruct((n_blocks * TOK_BLOCK, D_MODEL), F32),
        scratch_shapes=[pltpu.VMEM((2, TIME_BLOCK, SEQ_TILE, D_MODEL), F32),
                        pltpu.SemaphoreType.DMA((2,))],
        compiler_params=pltpu.CompilerParams(dimension_semantics=("arbitrary",),
                                             vmem_limit_bytes=VMEM_LIMIT),
        name="mix_out",
    )(*xs, v, ub, ub, ub, sg, mods, wglu, bglu, wpa, wpool, pscale, wpb, wout, bout, g1, b1)


def _mlp_kernel(*refs, paths):
    n_paths = len(paths)
    x_ref, mod_ref, w1_ref, b1_ref, w2_ref, b2_ref, g2_ref, be2_ref = refs[:8]
    y_hbm = refs[8:8 + n_paths]
    obuf, sem = refs[8 + n_paths:]
    i = pl.program_id(0)
    n = pl.num_programs(0)
    slot = i % 2

    @pl.when(i >= 2)
    def _():
        _wait_block(y_hbm, obuf, sem, slot, paths, to_hbm=True)

    sub_time = TIME_BLOCK // MLP_SLICES
    for sub in range(MLP_SLICES):
        x = x_ref[sub * sub_time * SEQ_TILE:(sub + 1) * sub_time * SEQ_TILE, :]
        h = _modulate(_layer_norm(x), 1.0 + mod_ref[4], mod_ref[3]).astype(BF16)
        f = jnp.zeros(x.shape, F32)
        for k in range(D_FF // D_MODEL):
            cols = slice(k * D_MODEL, (k + 1) * D_MODEL)
            a = jnp.square(jax.nn.relu(_dot(h, w1_ref[:, cols]) + b1_ref[:, cols]))
            f = f + _dot(a.astype(BF16), w2_ref[cols, :])
        y = DEEPNORM_ALPHA * x + _modulate(f + b2_ref[...], mod_ref[5])
        obuf[slot, sub * sub_time:(sub + 1) * sub_time] = (
            _layer_norm(y) * g2_ref[...] + be2_ref[...]).reshape(sub_time, SEQ_TILE, D_MODEL)

    _start_block(y_hbm, obuf, sem, i, slot, paths, to_hbm=True)

    @pl.when(i == n - 1)
    def _():
        @pl.when(n >= 2)
        def _():
            _wait_block(y_hbm, obuf, sem, 1 - slot, paths, to_hbm=True)
        _wait_block(y_hbm, obuf, sem, slot, paths, to_hbm=True)


def _mlp(x1, paths, n_blocks, mods, w1, b1, w2, b2, g2, be2):
    const2 = lambda i: (0, 0)
    vec = pl.BlockSpec((1, D_MODEL), const2)
    once = pl.Buffered(1)
    return pl.pallas_call(
        functools.partial(_mlp_kernel, paths=paths),
        grid=(n_blocks,),
        in_specs=[
            _row_spec(D_MODEL, n_blocks),
            pl.BlockSpec((N_MOD, SEQ_TILE, D_MODEL), lambda i: (0, _mod_block(i, paths), 0)),
            pl.BlockSpec((D_MODEL, D_FF), const2, pipeline_mode=once),
            pl.BlockSpec((1, D_FF), const2),
            pl.BlockSpec((D_FF, D_MODEL), const2, pipeline_mode=once),
            vec, vec, vec,
        ],
        out_specs=[pl.BlockSpec(memory_space=pl.ANY)] * len(paths),
        out_shape=[jax.ShapeDtypeStruct((p.n_seq, p.seq_len, D_MODEL), F32) for p in paths],
        scratch_shapes=[pltpu.VMEM((2, TIME_BLOCK, SEQ_TILE, D_MODEL), F32),
                        pltpu.SemaphoreType.DMA((2,))],
        compiler_params=pltpu.CompilerParams(dimension_semantics=("arbitrary",),
                                             vmem_limit_bytes=VMEM_LIMIT),
        name="mlp",
    )(x1, mods, w1, b1, w2, b2, g2, be2)


def _state_cols(x_f, x_b):
    return jnp.concatenate([x_f, x_b, x_f, x_b], axis=-1)


def kernel(x_prompt, x_sample, state_s5, c, c_ctx, w_ada, b_ada, w_in, b_in, s5_lam_re, s5_lam_im, s5_log_dt, s5_b_re, s5_b_im, s5_c_re, s5_c_im, s5_d, w_glu, b_glu, w_proj_a, w_pool, pool_scale, w_proj_b, w_out, b_out, ln1_g, ln1_b, w_mlp1, b_mlp1, w_mlp2, b_mlp2, ln2_g, ln2_b):
    assert w_in.shape[0] == 1, "single-layer backbone"
    n_ctx = x_prompt.shape[0]
    n_lat = x_sample.shape[0]
    g, p, hh = N_GROUPS, S5_STATE, S5_GROUP
    paths, n_blocks = _paths([(x_prompt.shape, x_prompt.shape[1], n_lat // SEQ_TILE, False),
                              (x_sample.shape, GRID_W, 0, True)])
    mods = _mods(c, c_ctx[None, :], w_ada[0], b_ada[0][None, :])

    dirs = lambda x: _state_cols(x[0], x[1])
    rows = jnp.stack(
        [dirs(s5_lam_re[0]), dirs(s5_lam_im[0]),
         dirs(jnp.broadcast_to(s5_log_dt[0][:, :, None], (2, g, p))),
         jnp.tile(s5_d[0, 0].reshape(g, hh), (1, CHUNK)),
         jnp.tile(s5_d[0, 1].reshape(g, hh), (1, CHUNK))], axis=1)
    mats = jnp.stack(
        [dirs(s5_b_re[0].transpose(0, 1, 3, 2)), dirs(s5_b_im[0].transpose(0, 1, 3, 2)),
         dirs(s5_c_re[0]), dirs(s5_c_im[0])], axis=1)
    mt_mat, wb_mat, wct_mat, lam16, w_in_b = _s5_prep(rows, mats, [w_in[0]])

    xs = (x_prompt, x_sample)
    ua, ub, sg = _in_proj(xs, paths, n_blocks, mods, w_in_b, b_in[0][None, :])

    st = state_s5[:, 0].astype(F32)
    s0 = jnp.concatenate([st[:, 0], st[:, 1]], axis=-1).transpose(2, 1, 0, 3)
    v, fin, w_glu_b, w_pa_b, w_pool_b, w_pb_b, w_out_b, w1_b, w2_b = _s5_core(
        ua, paths, mt_mat, wb_mat, wct_mat, lam16, s0,
        [w_glu[0], w_proj_a[0], w_pool[0].reshape(D_POOL, POOL_GROUP), w_proj_b[0], w_out[0],
         w_mlp1[0], w_mlp2[0]])
    new_state = fin.reshape(n_ctx, 1, 2, 2, g, p)

    x1 = _mix_out(xs, v, ub, sg, paths, n_blocks, mods,
                  w_glu_b, b_glu[0][None, :], w_pa_b, w_pool_b.reshape(w_pool.shape[1:]),
                  pool_scale[0][None, :], w_pb_b, w_out_b, b_out[0][None, :],
                  ln1_g[0][None, :], ln1_b[0][None, :])
    y_p, y_s = _mlp(x1, paths, n_blocks, mods, w1_b, b_mlp1[0][None, :], w2_b,
                    b_mlp2[0][None, :], ln2_g[0][None, :], ln2_b[0][None, :])
    return (y_p, y_s, new_state)
```
